```python
import math
import jax, jax.numpy as jnp
from jax import lax
import numpy as np

D_MODEL = 2048
BATCH = 8
SEQ = 2048
DEPTH = 1
DEC_BATCH = 32
DEC_SEQ = 1
PAST_LEN = 8192
PAGE_SIZE = 128

RWKV_DIM = D_MODEL // 2
RWKV_HEAD_DIM = 64
RWKV_HEADS = RWKV_DIM // RWKV_HEAD_DIM
DECAY_LORA = 64
AAA_LORA = 64
GATE_LORA = 128
RWKV_PROJ = 3 * RWKV_DIM + DECAY_LORA + AAA_LORA + GATE_LORA
DA_DIM = D_MODEL - RWKV_DIM
DA_QK_DIM = 64
DA_V_DIM = 2 * DA_QK_DIM
DA_HEADS = DA_DIM // DA_V_DIM
DA_PROJ = 3 * DA_DIM
IN_PROJ = RWKV_PROJ + DA_PROJ
MEM_TOKENS = 256
CROSS_HEADS = 4
CROSS_HEAD_DIM = D_MODEL // CROSS_HEADS
D_FF = ((8 * D_MODEL + 767) // 768) * 256
Q_BLOCK = 128
RMS_EPS = 1e-6
GN_EPS = 64e-5
NEG_INF = -1e30

kernel_name = "hymba_rwkv7_diffattn_decoder_step"


def _rmsnorm(x, g):
    xf = x.astype(jnp.float32)
    y = xf * lax.rsqrt(jnp.mean(xf * xf, axis=-1, keepdims=True) + RMS_EPS)
    return (y * g.astype(jnp.float32)).astype(x.dtype)


def _rwkv7_group(p, prev_row, S0, lp):
    f32 = jnp.float32
    b, t, _ = p.shape
    H, N = RWKV_HEADS, RWKV_HEAD_DIM
    p_prev = jnp.concatenate([prev_row[:, None, :].astype(p.dtype), p[:, :-1]], axis=1)
    ps = p + (p_prev - p) * lp['mu']
    o1, o2, o3 = RWKV_DIM, 2 * RWKV_DIM, 3 * RWKV_DIM
    o4 = o3 + DECAY_LORA
    o5 = o4 + AAA_LORA
    r, k, v = ps[..., :o1], ps[..., o1:o2], ps[..., o2:o3]
    wd, ad, gd = ps[..., o3:o4], ps[..., o4:o5], ps[..., o5:]
    w_log = -jax.nn.softplus(-(lp['w0'] + jnp.tanh(wd) @ lp['w2']).astype(f32)) - 0.5
    decay = jnp.exp(-jnp.exp(w_log))
    a = jax.nn.sigmoid((lp['a0'] + ad @ lp['a2']).astype(f32))
    g = jax.nn.sigmoid(gd) @ lp['g2']
    hs = lambda z: z.reshape(b, t, H, N)
    kk = hs((k * lp['k_k']).astype(f32))
    kk = kk * lax.rsqrt(jnp.maximum(jnp.sum(kk * kk, axis=-1, keepdims=True), 1e-24))
    k = k.astype(f32) * (1.0 + (a - 1.0) * lp['k_a'])
    r4, k4, v4, w4, a4 = hs(r.astype(f32)), hs(k), hs(v.astype(f32)), hs(decay), hs(a)
    avec = -kk
    bvec = kk * a4

    def step(S, inp):
        rt, wt, kt, vt, at, bt = inp
        S = (S * wt[:, :, None, :]
             + jnp.einsum('bhij,bhj->bhi', S, at)[..., None] * bt[:, :, None, :]
             + vt[..., None] * kt[:, :, None, :])
        return S, jnp.einsum('bhij,bhj->bhi', S, rt)

    tm = lambda z: jnp.moveaxis(z, 1, 0)
    S_fin, y = lax.scan(step, S0.astype(f32), (tm(r4), tm(w4), tm(k4), tm(v4), tm(avec), tm(bvec)))
    y = jnp.moveaxis(y, 0, 1)
    mean = jnp.mean(y, axis=-1, keepdims=True)
    var = jnp.mean(jnp.square(y - mean), axis=-1, keepdims=True)
    y = ((y - mean) * lax.rsqrt(var + GN_EPS)).reshape(b, t, RWKV_DIM) * lp['lnx_g'] + lp['lnx_b']
    bonus = jnp.sum(r4 * k4 * lp['r_k'], axis=-1, keepdims=True) * v4
    y = (y + bonus.reshape(b, t, RWKV_DIM)) * g
    return y.astype(p.dtype), S_fin.astype(S0.dtype), p[:, -1]


def _diff_attend(q, segs, q_pos, lam):
    scores = []
    for k, _, kp in segs:
        s = jnp.einsum('bthcd,bshcd->bchts', q, k).astype(jnp.float32)
        scores.append(jnp.where(kp[None, :] <= q_pos[:, None], s, NEG_INF))
    p = jax.nn.softmax(jnp.concatenate(scores, axis=-1), axis=-1)
    w = p[:, 0] - lam * p[:, 1]
    out = None
    off = 0
    for k, v, kp in segs:
        n = kp.shape[0]
        o = jnp.einsum('bhts,bshe->bthe', w[..., off:off + n], v.astype(jnp.float32))
        out = o if out is None else out + o
        off += n
    return out


def _prompt_attend(q, k, v, lam):
    b, t = q.shape[:2]
    nb = t // Q_BLOCK
    qb = jnp.moveaxis(q.reshape(b, nb, Q_BLOCK, DA_HEADS, 2, DA_QK_DIM), 1, 0)
    kpos = jnp.arange(t)

    def one(args):
        i, qi = args
        return _diff_attend(qi, [(k, v, kpos)], i * Q_BLOCK + jnp.arange(Q_BLOCK), lam)

    o = lax.map(one, (jnp.arange(nb), qb))
    return jnp.moveaxis(o, 0, 1).reshape(b, t, DA_HEADS, DA_V_DIM)


def _make_sample_attend(k_past, v_past):
    def attend(q, k, v, lam):
        P = k_past.shape[1]
        t = q.shape[1]
        new_pos = P + jnp.arange(t)
        return _diff_attend(q, [(k_past, v_past, jnp.arange(P)), (k, v, new_pos)], new_pos, lam)
    return attend


def _cross_attend(h, mk, mv, w_cq, w_co):
    b, t, _ = h.shape
    q = (h @ w_cq).reshape(b, t, CROSS_HEADS, CROSS_HEAD_DIM) * CROSS_HEAD_DIM ** -0.5
    s = jnp.einsum('bthe,bmhe->bhtm', q, mk).astype(jnp.float32)
    p = jax.nn.softmax(s, axis=-1).astype(mv.dtype)
    o = jnp.einsum('bhtm,bmhe->bthe', p, mv).reshape(b, t, D_MODEL)
    return o @ w_co


def _layer(x, prev_row, S0, attend, mem_k, mem_v, lp, lam_init):
    b, t, _ = x.shape
    h = _rmsnorm(x, lp['norm_mix_g'])
    pin = h @ lp['w_in']
    pr = pin[..., :RWKV_PROJ]
    pq = pin[..., RWKV_PROJ:RWKV_PROJ + DA_DIM]
    pk = pin[..., RWKV_PROJ + DA_DIM:RWKV_PROJ + 2 * DA_DIM]
    pv = pin[..., RWKV_PROJ + 2 * DA_DIM:]
    y_r, S_new, last_row = _rwkv7_group(pr, prev_row, S0, lp)
    q = pq.reshape(b, t, DA_HEADS, 2, DA_QK_DIM) * DA_QK_DIM ** -0.5
    k = pk.reshape(b, t, DA_HEADS, 2, DA_QK_DIM)
    v = pv.reshape(b, t, DA_HEADS, DA_V_DIM)
    f32 = jnp.float32
    lam = (jnp.exp(jnp.sum(lp['lq1'].astype(f32) * lp['lk1'].astype(f32)))
           - jnp.exp(jnp.sum(lp['lq2'].astype(f32) * lp['lk2'].astype(f32))) + lam_init)
    o = attend(q, k, v, lam)
    o = _rmsnorm(o, lp['subln_g']) * (1.0 - lam_init)
    y_d = o.reshape(b, t, DA_DIM).astype(x.dtype)
    x = x + jnp.concatenate([y_r, y_d], axis=-1) @ lp['w_o']
    x = x + _cross_attend(_rmsnorm(x, lp['norm_cross_g']), mem_k, mem_v, lp['w_cq'], lp['w_co'])
    hf = _rmsnorm(x, lp['norm_ffn_g'])
    x = x + (jax.nn.silu(hf @ lp['w_gate']) * (hf @ lp['w_up'])) @ lp['w_down']
    return x, k.reshape(b, t, DA_HEADS, 2 * DA_QK_DIM), v, S_new, last_row


def setup_inputs(seed: int = 0) -> dict:
    key = jax.random.key(seed)
    ks = iter(jax.random.split(key, 64))
    f32 = jnp.float32
    nrm = lambda shape, scale: jax.random.normal(next(ks), shape, f32) * scale
    gain = lambda shape: 1.0 + 0.05 * jax.random.normal(next(ks), shape, f32)
    n_pages = PAST_LEN // PAGE_SIZE
    n_used = DEC_BATCH * n_pages
    n_pool = n_used + max(1, n_used // 4)
    page_table = jax.random.permutation(next(ks), n_pool)[:n_used].reshape(DEC_BATCH, n_pages).astype(jnp.int32)
    L = DEPTH
    return {
        'x_prompt': nrm((BATCH, SEQ, D_MODEL), 1.0),
        'x_sample': nrm((DEC_BATCH, DEC_SEQ, D_MODEL), 1.0),
        'cache_k': nrm((L, n_pool, PAGE_SIZE, DA_HEADS, 2 * DA_QK_DIM), 1.0),
        'cache_v': nrm((L, n_pool, PAGE_SIZE, DA_HEADS, DA_V_DIM), 1.0),
        'cache_mem_k': nrm((L, DEC_BATCH, MEM_TOKENS, CROSS_HEADS, CROSS_HEAD_DIM), 1.0),
        'cache_mem_v': nrm((L, DEC_BATCH, MEM_TOKENS, CROSS_HEADS, CROSS_HEAD_DIM), 1.0),
        'state_wkv': nrm((L, DEC_BATCH, RWKV_HEADS, RWKV_HEAD_DIM, RWKV_HEAD_DIM), 0.5),
        'state_shift': nrm((L, DEC_BATCH, RWKV_PROJ), 1.0),
        'page_table': page_table,
        'mem_prompt': nrm((BATCH, MEM_TOKENS, D_MODEL), 1.0),
        'norm_mix_g': gain((L, D_MODEL)),
        'w_in': nrm((L, D_MODEL, IN_PROJ), D_MODEL ** -0.5),
        'tok_shift_mu': jax.random.uniform(next(ks), (L, RWKV_PROJ), f32),
        'rwkv_w0': jax.random.uniform(next(ks), (L, RWKV_DIM), f32, -4.0, 1.0),
        'rwkv_w2': nrm((L, DECAY_LORA, RWKV_DIM), 0.5 * DECAY_LORA ** -0.5),
        'rwkv_a0': nrm((L, RWKV_DIM), 0.1),
        'rwkv_a2': nrm((L, AAA_LORA, RWKV_DIM), 0.5 * AAA_LORA ** -0.5),
        'rwkv_g2': nrm((L, GATE_LORA, RWKV_DIM), GATE_LORA ** -0.5),
        'rwkv_k_k': 0.85 + nrm((L, RWKV_DIM), 0.05),
        'rwkv_k_a': 1.0 + nrm((L, RWKV_DIM), 0.05),
        'rwkv_r_k': nrm((L, RWKV_HEADS, RWKV_HEAD_DIM), 0.1),
        'rwkv_lnx_g': gain((L, RWKV_DIM)),
        'rwkv_lnx_b': nrm((L, RWKV_DIM), 0.02),
        'diff_lq1': nrm((L, DA_QK_DIM), 0.1),
        'diff_lk1': nrm((L, DA_QK_DIM), 0.1),
        'diff_lq2': nrm((L, DA_QK_DIM), 0.1),
        'diff_lk2': nrm((L, DA_QK_DIM), 0.1),
        'diff_subln_g': gain((L, DA_V_DIM)),
        'w_o': nrm((L, D_MODEL, D_MODEL), D_MODEL ** -0.5),
        'norm_cross_g': gain((L, D_MODEL)),
        'norm_mem_g': gain((L, D_MODEL)),
        'w_cq': nrm((L, D_MODEL, D_MODEL), D_MODEL ** -0.5),
        'w_ck': nrm((L, D_MODEL, D_MODEL), D_MODEL ** -0.5),
        'w_cv': nrm((L, D_MODEL, D_MODEL), D_MODEL ** -0.5),
        'w_co': nrm((L, D_MODEL, D_MODEL), D_MODEL ** -0.5),
        'norm_ffn_g': gain((L, D_MODEL)),
        'w_gate': nrm((L, D_MODEL, D_FF), D_MODEL ** -0.5),
        'w_up': nrm((L, D_MODEL, D_FF), D_MODEL ** -0.5),
        'w_down': nrm((L, D_FF, D_MODEL), D_FF ** -0.5),
        'final_norm_g': gain((D_MODEL,)),
    }


def reference(x_prompt, x_sample, cache_k, cache_v, cache_mem_k, cache_mem_v, state_wkv, state_shift,
              page_table, mem_prompt, norm_mix_g, w_in, tok_shift_mu, rwkv_w0, rwkv_w2, rwkv_a0, rwkv_a2,
              rwkv_g2, rwkv_k_k, rwkv_k_a, rwkv_r_k, rwkv_lnx_g, rwkv_lnx_b, diff_lq1, diff_lk1, diff_lq2,
              diff_lk2, diff_subln_g, w_o, norm_cross_g, norm_mem_g, w_cq, w_ck, w_cv, w_co, norm_ffn_g,
              w_gate, w_up, w_down, final_norm_g):
    b = x_prompt.shape[0]
    db = x_sample.shape[0]
    past_len = page_table.shape[1] * cache_k.shape[2]
    k_p, v_p, k_s, v_s = [], [], [], []
    S_p, S_s, sh_p, sh_s, mk_p, mv_p = [], [], [], [], [], []
    xp, xs = x_prompt, x_sample
    for l in range(DEPTH):
        lam_init = 0.8 - 0.6 * math.exp(-0.3 * l)
        lp = dict(norm_mix_g=norm_mix_g[l], w_in=w_in[l], mu=tok_shift_mu[l], w0=rwkv_w0[l], w2=rwkv_w2[l],
                  a0=rwkv_a0[l], a2=rwkv_a2[l], g2=rwkv_g2[l], k_k=rwkv_k_k[l], k_a=rwkv_k_a[l],
                  r_k=rwkv_r_k[l], lnx_g=rwkv_lnx_g[l], lnx_b=rwkv_lnx_b[l], lq1=diff_lq1[l], lk1=diff_lk1[l],
                  lq2=diff_lq2[l], lk2=diff_lk2[l], subln_g=diff_subln_g[l], w_o=w_o[l],
                  norm_cross_g=norm_cross_g[l], w_cq=w_cq[l], w_co=w_co[l], norm_ffn_g=norm_ffn_g[l],
                  w_gate=w_gate[l], w_up=w_up[l], w_down=w_down[l])
        memn = _rmsnorm(mem_prompt, norm_mem_g[l])
        mk = (memn @ w_ck[l]).reshape(b, MEM_TOKENS, CROSS_HEADS, CROSS_HEAD_DIM)
        mv = (memn @ w_cv[l]).reshape(b, MEM_TOKENS, CROSS_HEADS, CROSS_HEAD_DIM)
        zero_row = jnp.zeros((b, RWKV_PROJ), x_prompt.dtype)
        zero_S = jnp.zeros((b, RWKV_HEADS, RWKV_HEAD_DIM, RWKV_HEAD_DIM), state_wkv.dtype)
        xp, kn, vn, Sn, rown = _layer(xp, zero_row, zero_S, _prompt_attend, mk, mv, lp, lam_init)
        k_p.append(kn)
        v_p.append(vn)
        S_p.append(Sn)
        sh_p.append(rown)
        mk_p.append(mk)
        mv_p.append(mv)
        k_past = cache_k[l][page_table].reshape(db, past_len, DA_HEADS, 2, DA_QK_DIM)
        v_past = cache_v[l][page_table].reshape(db, past_len, DA_HEADS, DA_V_DIM)
        xs, kn, vn, Sn, rown = _layer(xs, state_shift[l], state_wkv[l], _make_sample_attend(k_past, v_past),
                                      cache_mem_k[l], cache_mem_v[l], lp, lam_init)
        k_s.append(kn)
        v_s.append(vn)
        S_s.append(Sn)
        sh_s.append(rown)
    y_prompt = _rmsnorm(xp, final_norm_g)
    y_sample = _rmsnorm(xs, final_norm_g)
    return (y_prompt, y_sample, jnp.stack(k_p), jnp.stack(v_p), jnp.stack(k_s), jnp.stack(v_s),
            jnp.stack(S_p), jnp.stack(S_s), jnp.stack(sh_p), jnp.stack(sh_s), jnp.stack(mk_p), jnp.stack(mv_p))
```

```python
import functools
import math

import jax
import jax.numpy as jnp
from jax import lax
from jax.experimental import pallas as pl
from jax.experimental.pallas import tpu as pltpu

F32 = jnp.float32
BF16 = jnp.bfloat16

D_MODEL = 2048
RWKV_DIM = 1024
HEAD = 64
RWKV_HEADS = 16
DECAY_LORA = 64
AAA_LORA = 64
GATE_LORA = 128
RWKV_PROJ = 3 * RWKV_DIM + DECAY_LORA + AAA_LORA + GATE_LORA
DA_DIM = 1024
DA_QK = 64
DA_V = 128
DA_HEADS = 8
CROSS_HEADS = 4
CROSS_HD = D_MODEL // CROSS_HEADS
RMS_EPS = 1e-6
GN_EPS = 64e-5
NEG_INF = -1e30

LANES = 128
CHUNK = 64
PAIRS = RWKV_HEADS // 2
VMEM_LIMIT = 48 * 1024 * 1024

_NT = (((1,), (1,)), ((), ()))


def _cparams(sem):
    return pltpu.CompilerParams(dimension_semantics=sem, vmem_limit_bytes=VMEM_LIMIT)


def _dot(a, b):
    return jnp.dot(a, b, preferred_element_type=F32)


def _dot_nt(a, b):
    return lax.dot_general(a, b, _NT, preferred_element_type=F32)


def _rms_rows(x, g):
    x = x.astype(F32)
    ms = jnp.mean(x * x, axis=-1, keepdims=True)
    return x * lax.rsqrt(ms + RMS_EPS) * g


def _sigmoid(x):
    return 1.0 / (1.0 + jnp.exp(-x))


def _split_dot(x, m_bf16, terms, left=False):
    acc = None
    rem = x
    for t in range(terms):
        piece = rem.astype(BF16)
        d = _dot(m_bf16, piece) if left else _dot(piece, m_bf16)
        acc = d if acc is None else acc + d
        if t + 1 < terms:
            rem = rem - piece.astype(F32)
    return acc


def _head_ones():
    i = lax.broadcasted_iota(jnp.int32, (LANES, LANES), 0) // HEAD
    j = lax.broadcasted_iota(jnp.int32, (LANES, LANES), 1) // HEAD
    return jnp.where(i == j, 1.0, 0.0).astype(BF16)


def _mm_kernel(*refs, n_a, has_norm, swiglu, has_res, scale):
    refs = list(refs)
    a_refs = [refs.pop(0) for _ in range(n_a)]
    g_ref = refs.pop(0) if has_norm else None
    w_refs = [refs.pop(0) for _ in range(2 if swiglu else n_a)]
    res_ref = refs.pop(0) if has_res else None
    o_ref = refs.pop(0)
    if has_norm:
        xn_ref = refs.pop(0)

        @pl.when(pl.program_id(1) == 0)
        def _():
            xn_ref[...] = _rms_rows(a_refs[0][...], g_ref[...]).astype(BF16)

        lhs = [xn_ref[...]]
    else:
        lhs = [r[...] for r in a_refs]
    if swiglu:
        gate = _dot(lhs[0], w_refs[0][...])
        up = _dot(lhs[0], w_refs[1][...])
        acc = gate * _sigmoid(gate) * up
    else:
        acc = _dot(lhs[0], w_refs[0][...])
        for a, w in zip(lhs[1:], w_refs[1:]):
            acc = acc + _dot(a, w[...])
    if scale != 1.0:
        acc = acc * scale
    if has_res:
        acc = acc + res_ref[...]
    o_ref[...] = acc.astype(o_ref.dtype)


def _mm(a_list, w_list, *, norm_g=None, swiglu=False, res=None, scale=1.0, out_dtype=F32, tm=512, tn=512):
    m = a_list[0].shape[0]
    n = w_list[0].shape[1]
    tm = min(tm, m)
    tn = min(tn, n)
    assert m % tm == 0 and n % tn == 0
    has_norm = norm_g is not None
    in_specs = [pl.BlockSpec((tm, a.shape[1]), lambda i, j: (i, 0)) for a in a_list]
    args = list(a_list)
    if has_norm:
        in_specs.append(pl.BlockSpec((1, a_list[0].shape[1]), lambda i, j: (0, 0)))
        args.append(norm_g.reshape(1, -1))
    for w in w_list:
        in_specs.append(pl.BlockSpec((w.shape[0], tn), lambda i, j: (0, j)))
        args.append(w)
    if res is not None:
        in_specs.append(pl.BlockSpec((tm, tn), lambda i, j: (i, j)))
        args.append(res)
    scratch = [pltpu.VMEM((tm, a_list[0].shape[1]), BF16)] if has_norm else []
    return pl.pallas_call(
        functools.partial(_mm_kernel, n_a=len(a_list), has_norm=has_norm, swiglu=swiglu,
                          has_res=res is not None, scale=scale),
        grid=(m // tm, n // tn),
        in_specs=in_specs,
        out_specs=pl.BlockSpec((tm, tn), lambda i, j: (i, j)),
        out_shape=jax.ShapeDtypeStruct((m, n), out_dtype),
        scratch_shapes=scratch,
        compiler_params=_cparams(("parallel", "arbitrary")),
    )(*args)


def _da_proj_kernel(x_ref, g_ref, w_ref, q_ref, kf_ref, vf_ref, kb_ref, vb_ref, xn_ref):
    j = pl.program_id(1)

    @pl.when(j == 0)
    def _():
        xn_ref[...] = _rms_rows(x_ref[...], g_ref[...]).astype(BF16)

    acc = _dot(xn_ref[...], w_ref[...])

    @pl.when(j == 0)
    def _():
        q_ref[...] = (acc * DA_QK ** -0.5).astype(BF16)

    @pl.when(j == 1)
    def _():
        kf_ref[...] = acc
        kb_ref[...] = acc.astype(BF16)

    @pl.when(j == 2)
    def _():
        vf_ref[...] = acc
        vb_ref[...] = acc.astype(BF16)


def _da_proj(x, norm_g, w_da, tm=512):
    m, k = x.shape
    tm = min(tm, m)
    blk = pl.BlockSpec((tm, DA_DIM), lambda i, j: (i, 0))
    return pl.pallas_call(
        _da_proj_kernel,
        grid=(m // tm, 3),
        in_specs=[pl.BlockSpec((tm, k), lambda i, j: (i, 0)),
                  pl.BlockSpec((1, k), lambda i, j: (0, 0)),
                  pl.BlockSpec((k, DA_DIM), lambda i, j: (0, j))],
        out_specs=[blk] * 5,
        out_shape=[jax.ShapeDtypeStruct((m, DA_DIM), dt) for dt in (BF16, F32, F32, BF16, BF16)],
        scratch_shapes=[pltpu.VMEM((tm, k), BF16)],
        compiler_params=_cparams(("parallel", "arbitrary")),
    )(x, norm_g.reshape(1, -1), w_da)


def _mm_down_kernel(a_ref, w_ref, res_ref, g_ref, o_ref, acc_ref, *, final_norm):
    kk = pl.program_id(1)

    @pl.when(kk == 0)
    def _():
        acc_ref[...] = jnp.zeros_like(acc_ref)

    acc_ref[...] += _dot(a_ref[...], w_ref[...])

    @pl.when(kk == pl.num_programs(1) - 1)
    def _():
        x = acc_ref[...] + res_ref[...]
        o_ref[...] = _rms_rows(x, g_ref[...]) if final_norm else x


def _mm_down(a, w, res, final_g, final_norm, tm=512, tk=512):
    m, k = a.shape
    n = w.shape[1]
    tm = min(tm, m)
    assert m % tm == 0 and k % tk == 0
    return pl.pallas_call(
        functools.partial(_mm_down_kernel, final_norm=final_norm),
        grid=(m // tm, k // tk),
        in_specs=[pl.BlockSpec((tm, tk), lambda i, kk: (i, kk)),
                  pl.BlockSpec((tk, n), lambda i, kk: (kk, 0)),
                  pl.BlockSpec((tm, n), lambda i, kk: (i, 0)),
                  pl.BlockSpec((1, n), lambda i, kk: (0, 0))],
        out_specs=pl.BlockSpec((tm, n), lambda i, kk: (i, 0)),
        out_shape=jax.ShapeDtypeStruct((m, n), F32),
        scratch_shapes=[pltpu.VMEM((tm, n), F32)],
        compiler_params=_cparams(("parallel", "arbitrary")),
    )(a, w, res, final_g.reshape(1, -1))


def _prep_kernel(p_ref, prev_ref, mu_ref, w0_ref, w2_ref, a0_ref, a2_ref, g2_ref, kk_ref, ka_ref,
                 r_o, k_o, v_o, kk_o, a_o, lw_o, g_o, *, rolled):
    p = p_ref[...]
    if rolled:
        row = lax.broadcasted_iota(jnp.int32, p.shape, 0)
        p_prev = jnp.where(row == 0, prev_ref[0], pltpu.roll(p, 1, 0))
    else:
        p_prev = prev_ref[...]
    ps = p + (p_prev - p) * mu_ref[...]
    o1, o2, o3 = RWKV_DIM, 2 * RWKV_DIM, 3 * RWKV_DIM
    o4 = o3 + DECAY_LORA
    o5 = o4 + AAA_LORA
    r, k, v = ps[:, :o1], ps[:, o1:o2], ps[:, o2:o3]
    wd, ad, gd = ps[:, o3:o4], ps[:, o4:o5], ps[:, o5:]
    z = w0_ref[...] + _dot(jnp.tanh(wd).astype(BF16), w2_ref[...])
    w_log = -(jnp.maximum(-z, 0.0) + jnp.log(1.0 + jnp.exp(-jnp.abs(z)))) - 0.5
    lw_o[...] = -jnp.exp(w_log)
    a = _sigmoid(a0_ref[...] + _dot(ad.astype(BF16), a2_ref[...]))
    g_o[...] = _dot(_sigmoid(gd).astype(BF16), g2_ref[...])
    ones = _head_ones()
    kraw = k * kk_ref[...]
    for pr in range(PAIRS):
        sl = slice(pr * LANES, (pr + 1) * LANES)
        kp = kraw[:, sl]
        ss = _split_dot(kp * kp, ones, 2)
        kk_o[:, sl] = kp * lax.rsqrt(jnp.maximum(ss, 1e-24))
    r_o[...] = r
    v_o[...] = v
    a_o[...] = a
    k_o[...] = k * (1.0 + (a - 1.0) * ka_ref[...])


def _rwkv_prep(p, prev, lp, *, rolled, tt):
    m = p.shape[0]
    row = lambda x: x.reshape(1, -1)
    full = lambda shp: pl.BlockSpec(shp, lambda i: (0,) * len(shp))
    prev_spec = (pl.BlockSpec((1, 1, RWKV_PROJ), lambda i: (i, 0, 0)) if rolled
                 else pl.BlockSpec((tt, RWKV_PROJ), lambda i: (i, 0)))
    out_blk = pl.BlockSpec((tt, RWKV_DIM), lambda i: (i, 0))
    return pl.pallas_call(
        functools.partial(_prep_kernel, rolled=rolled),
        grid=(m // tt,),
        in_specs=[pl.BlockSpec((tt, RWKV_PROJ), lambda i: (i, 0)), prev_spec,
                  full((1, RWKV_PROJ)), full((1, RWKV_DIM)), full((DECAY_LORA, RWKV_DIM)),
                  full((1, RWKV_DIM)), full((AAA_LORA, RWKV_DIM)), full((GATE_LORA, RWKV_DIM)),
                  full((1, RWKV_DIM)), full((1, RWKV_DIM))],
        out_specs=[out_blk] * 7,
        out_shape=[jax.ShapeDtypeStruct((m, RWKV_DIM), F32)] * 7,
        compiler_params=_cparams(("parallel",)),
    )(p, prev, row(lp['mu']), row(lp['w0']), lp['w2'].astype(BF16), row(lp['a0']), lp['a2'].astype(BF16),
      lp['g2'].astype(BF16), row(lp['k_k']), row(lp['k_a']))


def _chunk_kernel(r_ref, k_ref, v_ref, kk_ref, a_ref, lw_ref, y_ref, s_out_ref, s_ref):
    c = pl.program_id(1)

    @pl.when(c == 0)
    def _():
        s_ref[...] = jnp.zeros_like(s_ref)

    lane = lax.broadcasted_iota(jnp.int32, (1, LANES), 1)
    m0 = jnp.where(lane < HEAD, 1.0, 0.0)
    m1 = 1.0 - m0
    ii = lax.broadcasted_iota(jnp.int32, (LANES, LANES), 0)
    jj = lax.broadcasted_iota(jnp.int32, (LANES, LANES), 1)
    strict = ii > jj
    incl = ii >= jj
    eye = jnp.where(ii == jj, 1.0, 0.0)
    ti = lax.broadcasted_iota(jnp.int32, (CHUNK, CHUNK), 0)
    tj = lax.broadcasted_iota(jnp.int32, (CHUNK, CHUNK), 1)
    tril = jnp.where(ti >= tj, 1.0, 0.0).astype(BF16)

    def stack(x):
        return jnp.concatenate([x * m0, x * m1], axis=0)

    for pr in range(PAIRS):
        sl = slice(pr * LANES, (pr + 1) * LANES)
        lw = lw_ref[:, sl]
        kk = kk_ref[:, sl]
        bb = kk * a_ref[:, sl]
        k2 = k_ref[:, sl]
        cum = _split_dot(lw, tril, 3, left=True)
        tot = cum[CHUNK - 1:CHUNK, :]
        inv_g = jnp.exp(-cum)
        to_end = jnp.exp(tot - cum)
        a_t = stack(-kk * jnp.exp(cum - lw)).astype(BF16)
        b_t = stack(bb * inv_g).astype(BF16)
        k_t = stack(k2 * inv_g).astype(BF16)
        r_f = stack(r_ref[:, sl] * jnp.exp(cum))
        r_t = r_f.astype(BF16)
        v_f = stack(v_ref[:, sl])
        v_s = v_f.astype(BF16)
        bk_end = jnp.concatenate([stack(bb * to_end), stack(k2 * to_end)], axis=0).astype(BF16)

        a_ab = jnp.where(strict, _dot_nt(a_t, b_t), 0.0)
        a_ak = jnp.where(strict, _dot_nt(a_t, k_t), 0.0).astype(BF16)
        a_rb = jnp.where(incl, _dot_nt(r_t, b_t), 0.0).astype(BF16)
        a_rk = jnp.where(incl, _dot_nt(r_t, k_t), 0.0).astype(BF16)
        inv = eye + a_ab
        power = a_ab.astype(BF16)
        for step in range(5):
            sq = _dot(power, power)
            power = sq.astype(BF16)
            inv = inv + _dot(inv.astype(BF16), power)
        inv_b = inv.astype(BF16)
        w1 = _dot(a_ak, v_s)
        pu = _dot(inv_b, jnp.concatenate([a_t, w1.astype(BF16)], axis=1))
        pu_b = pu.astype(BF16)
        qy = _dot(a_rb, pu_b)
        q_m = r_f + qy[:, :LANES]
        y_v = qy[:, LANES:] + _dot(a_rk, v_s)
        s_old = s_ref[pr]
        uy = _dot_nt(jnp.concatenate([pu_b[:, :LANES], q_m.astype(BF16)], axis=0), s_old.astype(BF16))
        u_m = uy[:LANES] + pu[:, LANES:]
        y_m = uy[LANES:] + y_v
        uv_t = jnp.concatenate([u_m, v_f], axis=0).T.astype(BF16)
        s_new = s_old * jnp.exp(tot) + _dot(uv_t, bk_end)
        s_ref[pr] = s_new
        y_ref[:, sl] = y_m[:CHUNK] + y_m[CHUNK:]

    @pl.when(c == pl.num_programs(1) - 1)
    def _():
        s_out_ref[0] = s_ref[...]


def _rwkv_chunk_scan(r, k2, v, kk, a, lw, batch):
    m = r.shape[0]
    nc = m // batch // CHUNK
    blk = pl.BlockSpec((CHUNK, RWKV_DIM), lambda b, c: (b * nc + c, 0))
    y, s_pairs = pl.pallas_call(
        _chunk_kernel,
        grid=(batch, nc),
        in_specs=[blk] * 6,
        out_specs=[blk, pl.BlockSpec((1, PAIRS, LANES, LANES), lambda b, c: (b, 0, 0, 0))],
        out_shape=[jax.ShapeDtypeStruct((m, RWKV_DIM), F32),
                   jax.ShapeDtypeStruct((batch, PAIRS, LANES, LANES), F32)],
        scratch_shapes=[pltpu.VMEM((PAIRS, LANES, LANES), F32)],
        compiler_params=_cparams(("parallel", "arbitrary")),
    )(r, k2, v, kk, a, lw)
    s_fin = jnp.stack([s_pairs[:, :, :HEAD, :HEAD], s_pairs[:, :, HEAD:, HEAD:]], axis=2)
    return y, s_fin.reshape(batch, RWKV_HEADS, HEAD, HEAD)


def _step_kernel(s_ref, r_ref, k_ref, kk_ref, a_ref, lw_ref, v_ref, s_o, y_o):
    s = s_ref[0]
    kk = kk_ref[0]
    sa = jnp.sum(s * (-kk), axis=-1, keepdims=True)
    s_new = s * jnp.exp(lw_ref[0]) + sa * (kk * a_ref[0]) + v_ref[0] * k_ref[0]
    s_o[0] = s_new
    y_o[0] = jnp.sum(s_new * r_ref[0], axis=-1, keepdims=True)


def _rwkv_step(s0, r, k2, v, kk, a, lw):
    b = s0.shape[0]
    rowv = lambda x: x.reshape(b, RWKV_HEADS, 1, HEAD)
    row_spec = pl.BlockSpec((1, RWKV_HEADS, 1, HEAD), lambda i: (i, 0, 0, 0))
    col_spec = pl.BlockSpec((1, RWKV_HEADS, HEAD, 1), lambda i: (i, 0, 0, 0))
    s_spec = pl.BlockSpec((1, RWKV_HEADS, HEAD, HEAD), lambda i: (i, 0, 0, 0))
    s_new, y = pl.pallas_call(
        _step_kernel,
        grid=(b,),
        in_specs=[s_spec] + [row_spec] * 5 + [col_spec],
        out_specs=[s_spec, col_spec],
        out_shape=[jax.ShapeDtypeStruct(s0.shape, F32), jax.ShapeDtypeStruct((b, RWKV_HEADS, HEAD, 1), F32)],
        compiler_params=_cparams(("parallel",)),
    )(s0, rowv(r), rowv(k2), rowv(kk), rowv(a), rowv(lw), v.reshape(b, RWKV_HEADS, HEAD, 1))
    return y.reshape(b, RWKV_DIM), s_new


def _post_kernel(y_ref, r_ref, k_ref, v_ref, g_ref, rk_ref, lg_ref, lb_ref, o_ref):
    ones = _head_ones()
    for pr in range(PAIRS):
        sl = slice(pr * LANES, (pr + 1) * LANES)
        y = y_ref[:, sl]
        mean = _split_dot(y, ones, 2) * (1.0 / HEAD)
        d = y - mean
        var = _split_dot(d * d, ones, 2) * (1.0 / HEAD)
        yn = d * lax.rsqrt(var + GN_EPS) * lg_ref[:, sl] + lb_ref[:, sl]
        bonus = _split_dot(r_ref[:, sl] * k_ref[:, sl] * rk_ref[:, sl], ones, 2) * v_ref[:, sl]
        o_ref[:, sl] = ((yn + bonus) * g_ref[:, sl]).astype(o_ref.dtype)


def _rwkv_post(y, r, k2, v, g, lp, tt):
    m = y.shape[0]
    blk = pl.BlockSpec((tt, RWKV_DIM), lambda i: (i, 0))
    par = pl.BlockSpec((1, RWKV_DIM), lambda i: (0, 0))
    return pl.pallas_call(
        _post_kernel,
        grid=(m // tt,),
        in_specs=[blk] * 5 + [par] * 3,
        out_specs=blk,
        out_shape=jax.ShapeDtypeStruct((m, RWKV_DIM), BF16),
        compiler_params=_cparams(("parallel",)),
    )(y, r, k2, v, g, lp['r_k'].reshape(1, -1), lp['lnx_g'].reshape(1, -1), lp['lnx_b'].reshape(1, -1))


def _lambda(lq1_ref, lk1_ref, lq2_ref, lk2_ref, lam_init):
    s1 = jnp.sum(lq1_ref[...] * lk1_ref[...], axis=-1, keepdims=True)
    s2 = jnp.sum(lq2_ref[...] * lk2_ref[...], axis=-1, keepdims=True)
    return jnp.exp(s1) - jnp.exp(s2) + lam_init


def _dattn_kernel(lq1_ref, lk1_ref, lq2_ref, lk2_ref, sg_ref, q_ref, k_ref, v_ref, o_ref,
                  m_ref, l_ref, acc_ref, *, tq, lam_init):
    qi = pl.program_id(2)
    q = q_ref[...]
    lane = lax.broadcasted_iota(jnp.int32, (1, LANES), 1)
    zero = jnp.zeros_like(q)
    q_maps = (jnp.where(lane < DA_QK, q, zero), jnp.where(lane >= DA_QK, q, zero))
    m_ref[...] = jnp.full_like(m_ref, NEG_INF)
    l_ref[...] = jnp.zeros_like(l_ref)
    acc_ref[...] = jnp.zeros_like(acc_ref)
    row = lax.broadcasted_iota(jnp.int32, (tq, tq), 0)
    col = lax.broadcasted_iota(jnp.int32, (tq, tq), 1)

    def block(ki, masked):
        start = pl.multiple_of(ki * tq, tq)
        ks = k_ref[pl.ds(start, tq), :]
        vs = v_ref[pl.ds(start, tq), :]
        for c in range(2):
            s = _dot_nt(q_maps[c], ks)
            if masked:
                s = jnp.where(col <= row, s, NEG_INF)
            m_old = m_ref[c]
            m_new = jnp.maximum(m_old, jnp.max(s, axis=-1, keepdims=True))
            alpha = jnp.exp(m_old - m_new)
            p = jnp.exp(s - m_new)
            l_ref[c] = alpha * l_ref[c] + jnp.sum(p, axis=-1, keepdims=True)
            acc_ref[c] = alpha * acc_ref[c] + _dot(p.astype(BF16), vs)
            m_ref[c] = m_new

    def body(ki, carry):
        block(ki, False)
        return carry

    lax.fori_loop(0, qi, body, 0)
    block(qi, True)
    lam = _lambda(lq1_ref, lk1_ref, lq2_ref, lk2_ref, lam_init)
    o = acc_ref[0] / l_ref[0] - lam * (acc_ref[1] / l_ref[1])
    o_ref[...] = (_rms_rows(o, sg_ref[...]) * (1.0 - lam_init)).astype(o_ref.dtype)


def _diff_attn_prompt(qb, kb, vb, lp, lam_init, batch, tq=256):
    m = qb.shape[0]
    t = m // batch
    tq = min(tq, t)
    nq = t // tq
    vec = pl.BlockSpec((1, DA_QK), lambda b, h, i: (0, 0))
    kv_spec = pl.BlockSpec((t, LANES), lambda b, h, i: (b, h))
    return pl.pallas_call(
        functools.partial(_dattn_kernel, tq=tq, lam_init=lam_init),
        grid=(batch, DA_HEADS, nq),
        in_specs=[vec] * 4 + [pl.BlockSpec((1, DA_V), lambda b, h, i: (0, 0)),
                              pl.BlockSpec((tq, LANES), lambda b, h, i: (b * nq + i, h)), kv_spec, kv_spec],
        out_specs=pl.BlockSpec((tq, LANES), lambda b, h, i: (b * nq + i, h)),
        out_shape=jax.ShapeDtypeStruct((m, DA_DIM), BF16),
        scratch_shapes=[pltpu.VMEM((2, tq, 1), F32), pltpu.VMEM((2, tq, 1), F32), pltpu.VMEM((2, tq, LANES), F32)],
        compiler_params=_cparams(("parallel", "parallel", "arbitrary")),
    )(*[lp[n].reshape(1, -1) for n in ('lq1', 'lk1', 'lq2', 'lk2')], lp['subln_g'].reshape(1, -1), qb, kb, vb)


def _paged_kernel(pt_ref, lq1_ref, lk1_ref, lq2_ref, lk2_ref, sg_ref, q_ref, kn_ref, vn_ref, kp_ref, vp_ref,
                  o_ref, qbd_ref, m_ref, l_ref, acc_ref, *, lam_init):
    del pt_ref
    pg = pl.program_id(1)
    nmap = 2 * DA_HEADS
    rown = lax.broadcasted_iota(jnp.int32, (nmap, DA_DIM), 0)
    lanen = lax.broadcasted_iota(jnp.int32, (nmap, DA_DIM), 1)

    @pl.when(pg == 0)
    def _():
        qbd_ref[...] = jnp.where(lanen // DA_QK == rown, q_ref[0], 0.0)
        m_ref[...] = jnp.full_like(m_ref, NEG_INF)
        l_ref[...] = jnp.zeros_like(l_ref)
        acc_ref[...] = jnp.zeros_like(acc_ref)

    def update(s, vals_fn):
        m_old = m_ref[...]
        m_new = jnp.maximum(m_old, jnp.max(s, axis=-1, keepdims=True))
        alpha = jnp.exp(m_old - m_new)
        p = jnp.exp(s - m_new)
        l_ref[...] = alpha * l_ref[...] + jnp.sum(p, axis=-1, keepdims=True)
        acc_ref[...] = alpha * acc_ref[...] + vals_fn(p)
        m_ref[...] = m_new

    qbd = qbd_ref[...]
    vpage = vp_ref[0].astype(BF16)
    update(_dot_nt(qbd.astype(BF16), kp_ref[0].astype(BF16)), lambda p: _dot(p.astype(BF16), vpage))

    @pl.when(pg == pl.num_programs(1) - 1)
    def _():
        s_new = jnp.sum(qbd * kn_ref[0], axis=-1, keepdims=True)
        update(s_new, lambda p: p * vn_ref[0])
        lam = _lambda(lq1_ref, lk1_ref, lq2_ref, lk2_ref, lam_init)
        out = acc_ref[...] / l_ref[...]
        own = lanen // DA_V == rown // 2
        coef = jnp.where(rown % 2 == 0, 1.0, -lam)
        o = jnp.sum(jnp.where(own, out * coef, 0.0), axis=0, keepdims=True)
        hrow = lax.broadcasted_iota(jnp.int32, (DA_HEADS, DA_DIM), 0)
        hlane = lax.broadcasted_iota(jnp.int32, (DA_HEADS, DA_DIM), 1) // DA_V
        mine = hrow == hlane
        ms = jnp.sum(jnp.where(mine, o * o, 0.0), axis=-1, keepdims=True) * (1.0 / DA_V)
        inv = jnp.sum(jnp.where(mine, lax.rsqrt(ms + RMS_EPS), 0.0), axis=0, keepdims=True)
        gain = jnp.concatenate([sg_ref[...]] * DA_HEADS, axis=1)
        o_ref[0] = (o * inv * gain * (1.0 - lam_init)).astype(o_ref.dtype)


def _diff_attn_paged(q, k_new, v_new, cache_k, cache_v, page_table, lp, lam_init):
    b, n_pages = page_table.shape
    page = cache_k.shape[1]
    vec = pl.BlockSpec((1, DA_QK), lambda i, p, pt: (0, 0))
    rowb = pl.BlockSpec((1, 1, DA_DIM), lambda i, p, pt: (i, 0, 0))
    pg_spec = pl.BlockSpec((1, page, DA_DIM), lambda i, p, pt: (pt[i * n_pages + p], 0, 0))
    nmap = 2 * DA_HEADS
    grid_spec = pltpu.PrefetchScalarGridSpec(
        num_scalar_prefetch=1,
        grid=(b, n_pages),
        in_specs=[vec] * 4 + [pl.BlockSpec((1, DA_V), lambda i, p, pt: (0, 0)), rowb, rowb, rowb, pg_spec, pg_spec],
        out_specs=rowb,
        scratch_shapes=[pltpu.VMEM((nmap, DA_DIM), F32), pltpu.VMEM((nmap, 1), F32), pltpu.VMEM((nmap, 1), F32),
                        pltpu.VMEM((nmap, DA_DIM), F32)],
    )
    r3 = lambda x: x.reshape(b, 1, DA_DIM)
    out = pl.pallas_call(
        functools.partial(_paged_kernel, lam_init=lam_init),
        grid_spec=grid_spec,
        out_shape=jax.ShapeDtypeStruct((b, 1, DA_DIM), BF16),
        compiler_params=_cparams(("parallel", "arbitrary")),
    )(page_table.reshape(-1), *[lp[n].reshape(1, -1) for n in ('lq1', 'lk1', 'lq2', 'lk2')],
      lp['subln_g'].reshape(1, -1), r3(q), r3(k_new), r3(v_new), cache_k, cache_v)
    return out.reshape(b, DA_DIM)


def _cross_kernel(q_ref, mk_ref, mv_ref, o_ref):
    s = _dot_nt(q_ref[0], mk_ref[0].astype(BF16))
    m = jnp.max(s, axis=-1, keepdims=True)
    p = jnp.exp(s - m)
    l = jnp.sum(p, axis=-1, keepdims=True)
    o_ref[0] = (_dot(p.astype(BF16), mv_ref[0].astype(BF16)) / l).astype(o_ref.dtype)


def _cross_attn(q, mk, mv, tq=512):
    b, t, _ = q.shape
    mt = mk.shape[1]
    tq = min(tq, t)
    q_spec = pl.BlockSpec((1, tq, CROSS_HD), lambda i, h, j: (i, j, h))
    m_spec = pl.BlockSpec((1, mt, CROSS_HD), lambda i, h, j: (i, 0, h))
    return pl.pallas_call(
        _cross_kernel,
        grid=(b, CROSS_HEADS, t // tq),
        in_specs=[q_spec, m_spec, m_spec],
        out_specs=q_spec,
        out_shape=jax.ShapeDtypeStruct(q.shape, BF16),
        compiler_params=_cparams(("parallel", "parallel", "parallel")),
    )(q, mk, mv)


def _layer(x, batch, lp, wts, lam_init, mk, mv, rwkv_fn, attn_fn, tm):
    m = x.shape[0]
    t = m // batch
    pr = _mm([x], [wts['in_r']], norm_g=lp['norm_mix_g'], tm=tm, tn=256)
    qb, kf, vf, kb, vb = _da_proj(x, lp['norm_mix_g'], wts['in_da'], tm=tm)
    y_r, s_new = rwkv_fn(pr)
    y_d = attn_fn(qb, kf, vf, kb, vb)
    x = _mm([y_r, y_d], [wts['o_r'], wts['o_d']], res=x, tm=tm)
    qc = _mm([x], [wts['cq']], norm_g=lp['norm_cross_g'], scale=CROSS_HD ** -0.5, out_dtype=BF16, tm=tm)
    tq = t
    qc3 = qc.reshape(batch, t, D_MODEL)
    if t < 8:
        qc3 = jnp.pad(qc3, ((0, 0), (0, 8 - t), (0, 0)))
    oc = _cross_attn(qc3, mk, mv)[:, :tq].reshape(m, D_MODEL)
    x = _mm([oc], [wts['co']], res=x, tm=tm)
    hmid = _mm([x], [wts['gate'], wts['up']], norm_g=lp['norm_ffn_g'], swiglu=True, out_dtype=BF16, tm=tm)
    x = _mm_down(hmid, wts['down'], x, wts['final_g'], wts['is_last'], tm=tm)
    last_row = pr.reshape(batch, t, RWKV_PROJ)[:, -1]
    return x, kf, vf, s_new, last_row


def kernel(x_prompt, x_sample, cache_k, cache_v, cache_mem_k, cache_mem_v, state_wkv, state_shift, page_table, mem_prompt, norm_mix_g, w_in, tok_shift_mu, rwkv_w0, rwkv_w2, rwkv_a0, rwkv_a2, rwkv_g2, rwkv_k_k, rwkv_k_a, rwkv_r_k, rwkv_lnx_g, rwkv_lnx_b, diff_lq1, diff_lk1, diff_lq2, diff_lk2, diff_subln_g, w_o, norm_cross_g, norm_mem_g, w_cq, w_ck, w_cv, w_co, norm_ffn_g, w_gate, w_up, w_down, final_norm_g):
    b, t, _ = x_prompt.shape
    db, dt, _ = x_sample.shape
    depth = w_in.shape[0]
    n_mem = mem_prompt.shape[1]
    n_pool, page = cache_k.shape[1], cache_k.shape[2]
    xp = x_prompt.reshape(b * t, D_MODEL)
    xs = x_sample.reshape(db * dt, D_MODEL)
    outs = {n: [] for n in ('k_p', 'v_p', 'k_s', 'v_s', 'S_p', 'S_s', 'sh_p', 'sh_s', 'mk_p', 'mv_p')}
    for l in range(depth):
        lam_init = 0.8 - 0.6 * math.exp(-0.3 * l)
        lp = dict(norm_mix_g=norm_mix_g[l], mu=tok_shift_mu[l], w0=rwkv_w0[l], w2=rwkv_w2[l], a0=rwkv_a0[l],
                  a2=rwkv_a2[l], g2=rwkv_g2[l], k_k=rwkv_k_k[l], k_a=rwkv_k_a[l], r_k=rwkv_r_k[l],
                  lnx_g=rwkv_lnx_g[l], lnx_b=rwkv_lnx_b[l], lq1=diff_lq1[l], lk1=diff_lk1[l], lq2=diff_lq2[l],
                  lk2=diff_lk2[l], subln_g=diff_subln_g[l], norm_cross_g=norm_cross_g[l], norm_ffn_g=norm_ffn_g[l])
        wl = w_in[l].astype(BF16)
        wts = dict(in_r=wl[:, :RWKV_PROJ], in_da=wl[:, RWKV_PROJ:], o_r=w_o[l][:RWKV_DIM].astype(BF16),
                   o_d=w_o[l][RWKV_DIM:].astype(BF16), cq=w_cq[l].astype(BF16), co=w_co[l].astype(BF16),
                   gate=w_gate[l].astype(BF16), up=w_up[l].astype(BF16), down=w_down[l].astype(BF16),
                   final_g=final_norm_g, is_last=l == depth - 1)

        mem2 = mem_prompt.reshape(b * n_mem, D_MODEL)
        mk = _mm([mem2], [w_ck[l].astype(BF16)], norm_g=norm_mem_g[l])
        mv = _mm([mem2], [w_cv[l].astype(BF16)], norm_g=norm_mem_g[l])
        tt = min(256, t)

        def rwkv_prompt(pr):
            tiles = pr.reshape(b, t // tt, tt, RWKV_PROJ)
            prev = jnp.concatenate([jnp.zeros((b, 1, RWKV_PROJ), F32), tiles[:, :-1, -1]], axis=1)
            r, k2, v, kk, a, lw, g = _rwkv_prep(pr, prev.reshape(-1, 1, RWKV_PROJ), lp, rolled=True, tt=tt)
            y, s_fin = _rwkv_chunk_scan(r, k2, v, kk, a, lw, b)
            return _rwkv_post(y, r, k2, v, g, lp, tt), s_fin

        def attn_prompt(qb, kf, vf, kb, vb):
            return _diff_attn_prompt(qb, kb, vb, lp, lam_init, b)

        xp, kf, vf, s_new, last = _layer(xp, b, lp, wts, lam_init, mk.reshape(b, n_mem, D_MODEL),
                                         mv.reshape(b, n_mem, D_MODEL), rwkv_prompt, attn_prompt, 512)
        outs['k_p'].append(kf.reshape(b, t, DA_HEADS, 2 * DA_QK))
        outs['v_p'].append(vf.reshape(b, t, DA_HEADS, DA_V))
        outs['S_p'].append(s_new)
        outs['sh_p'].append(last)
        outs['mk_p'].append(mk.reshape(b, n_mem, CROSS_HEADS, CROSS_HD))
        outs['mv_p'].append(mv.reshape(b, n_mem, CROSS_HEADS, CROSS_HD))

        assert dt == 1

        def rwkv_sample(pr):
            r, k2, v, kk, a, lw, g = _rwkv_prep(pr, state_shift[l], lp, rolled=False, tt=db)
            y, s_fin = _rwkv_step(state_wkv[l], r, k2, v, kk, a, lw)
            return _rwkv_post(y, r, k2, v, g, lp, db), s_fin

        def attn_sample(qb, kf, vf, kb, vb):
            return _diff_attn_paged(qb.astype(F32), kf, vf, cache_k[l].reshape(n_pool, page, DA_DIM),
                                    cache_v[l].reshape(n_pool, page, DA_DIM), page_table, lp, lam_init)

        xs, kf, vf, s_new, last = _layer(xs, db, lp, wts, lam_init, cache_mem_k[l].reshape(db, n_mem, D_MODEL),
                                         cache_mem_v[l].reshape(db, n_mem, D_MODEL), rwkv_sample, attn_sample, db)
        outs['k_s'].append(kf.reshape(db, dt, DA_HEADS, 2 * DA_QK))
        outs['v_s'].append(vf.reshape(db, dt, DA_HEADS, DA_V))
        outs['S_s'].append(s_new)
        outs['sh_s'].append(last)
    y_prompt = xp.reshape(b, t, D_MODEL)
    y_sample = xs.reshape(db, dt, D_MODEL)
    st = lambda n: jnp.stack(outs[n])
    return (y_prompt, y_sample, st('k_p'), st('v_p'), st('k_s'), st('v_s'), st('S_p'), st('S_s'),
            st('sh_p'), st('sh_s'), st('mk_p'), st('mv_p'))
```

```python
import functools
import math

import jax
import jax.numpy as jnp
from jax import lax
from jax.experimental import pallas as pl
from jax.experimental.pallas import tpu as pltpu

F32 = jnp.float32
BF16 = jnp.bfloat16

D_MODEL = 2048
RWKV_DIM = 1024
HEAD = 64
RWKV_HEADS = 16
DECAY_LORA = 64
AAA_LORA = 64
GATE_LORA = 128
RWKV_PROJ = 3 * RWKV_DIM + DECAY_LORA + AAA_LORA + GATE_LORA
DA_DIM = 1024
DA_QK = 64
DA_V = 128
DA_HEADS = 8
CROSS_HEADS = 4
CROSS_HD = D_MODEL // CROSS_HEADS
RMS_EPS = 1e-6
GN_EPS = 64e-5
NEG_INF = -1e30

LANES = 128
CHUNK = 64
PAIRS = RWKV_HEADS // 2
VMEM_LIMIT = 48 * 1024 * 1024

_NT = (((1,), (1,)), ((), ()))


def _cparams(sem):
    return pltpu.CompilerParams(dimension_semantics=sem, vmem_limit_bytes=VMEM_LIMIT)


def _dot(a, b):
    return jnp.dot(a, b, preferred_element_type=F32)


def _dot_nt(a, b):
    return lax.dot_general(a, b, _NT, preferred_element_type=F32)


def _rms_rows(x, g):
    x = x.astype(F32)
    ms = jnp.mean(x * x, axis=-1, keepdims=True)
    return x * lax.rsqrt(ms + RMS_EPS) * g


def _sigmoid(x):
    return 1.0 / (1.0 + jnp.exp(-x))


def _split_dot(x, m_bf16, terms, left=False):
    acc = None
    rem = x
    for t in range(terms):
        piece = rem.astype(BF16)
        d = _dot(m_bf16, piece) if left else _dot(piece, m_bf16)
        acc = d if acc is None else acc + d
        if t + 1 < terms:
            rem = rem - piece.astype(F32)
    return acc


def _head_ones():
    i = lax.broadcasted_iota(jnp.int32, (LANES, LANES), 0) // HEAD
    j = lax.broadcasted_iota(jnp.int32, (LANES, LANES), 1) // HEAD
    return jnp.where(i == j, 1.0, 0.0).astype(BF16)


def _mm_kernel(*refs, n_a, has_norm, swiglu, has_res, scale):
    refs = list(refs)
    a_refs = [refs.pop(0) for _ in range(n_a)]
    g_ref = refs.pop(0) if has_norm else None
    w_refs = [refs.pop(0) for _ in range(2 if swiglu else n_a)]
    res_ref = refs.pop(0) if has_res else None
    o_ref = refs.pop(0)
    if has_norm:
        xn_ref = refs.pop(0)

        @pl.when(pl.program_id(1) == 0)
        def _():
            xn_ref[...] = _rms_rows(a_refs[0][...], g_ref[...]).astype(BF16)

        lhs = [xn_ref[...]]
    else:
        lhs = [r[...] for r in a_refs]
    if swiglu:
        gate = _dot(lhs[0], w_refs[0][...])
        up = _dot(lhs[0], w_refs[1][...])
        acc = gate * _sigmoid(gate) * up
    else:
        acc = _dot(lhs[0], w_refs[0][...])
        for a, w in zip(lhs[1:], w_refs[1:]):
            acc = acc + _dot(a, w[...])
    if scale != 1.0:
        acc = acc * scale
    if has_res:
        acc = acc + res_ref[...]
    o_ref[...] = acc.astype(o_ref.dtype)


def _mm(a_list, w_list, *, name, norm_g=None, swiglu=False, res=None, scale=1.0, out_dtype=F32, tm=512, tn=512):
    m = a_list[0].shape[0]
    n = w_list[0].shape[1]
    tm = min(tm, m)
    tn = min(tn, n)
    assert m % tm == 0 and n % tn == 0
    has_norm = norm_g is not None
    in_specs = [pl.BlockSpec((tm, a.shape[1]), lambda i, j: (i, 0)) for a in a_list]
    args = list(a_list)
    if has_norm:
        in_specs.append(pl.BlockSpec((1, a_list[0].shape[1]), lambda i, j: (0, 0)))
        args.append(norm_g.reshape(1, -1))
    for w in w_list:
        in_specs.append(pl.BlockSpec((w.shape[0], tn), lambda i, j: (0, j)))
        args.append(w)
    if res is not None:
        in_specs.append(pl.BlockSpec((tm, tn), lambda i, j: (i, j)))
        args.append(res)
    scratch = [pltpu.VMEM((tm, a_list[0].shape[1]), BF16)] if has_norm else []
    return pl.pallas_call(
        functools.partial(_mm_kernel, n_a=len(a_list), has_norm=has_norm, swiglu=swiglu,
                          has_res=res is not None, scale=scale),
        grid=(m // tm, n // tn),
        in_specs=in_specs,
        out_specs=pl.BlockSpec((tm, tn), lambda i, j: (i, j)),
        out_shape=jax.ShapeDtypeStruct((m, n), out_dtype),
        scratch_shapes=scratch,
        compiler_params=_cparams(("parallel", "arbitrary")),
        name=name,
    )(*args)


def _da_proj_kernel(x_ref, g_ref, w_ref, q_ref, kf_ref, vf_ref, kb_ref, vb_ref, xn_ref):
    j = pl.program_id(1)

    @pl.when(j == 0)
    def _():
        xn_ref[...] = _rms_rows(x_ref[...], g_ref[...]).astype(BF16)

    acc = _dot(xn_ref[...], w_ref[...])

    @pl.when(j == 0)
    def _():
        q_ref[...] = (acc * DA_QK ** -0.5).astype(BF16)

    @pl.when(j == 1)
    def _():
        kf_ref[...] = acc
        kb_ref[...] = acc.astype(BF16)

    @pl.when(j == 2)
    def _():
        vf_ref[...] = acc
        vb_ref[...] = acc.astype(BF16)


def _da_proj(x, norm_g, w_da, tm=512):
    m, k = x.shape
    tm = min(tm, m)
    blk = pl.BlockSpec((tm, DA_DIM), lambda i, j: (i, 0))
    return pl.pallas_call(
        _da_proj_kernel,
        grid=(m // tm, 3),
        in_specs=[pl.BlockSpec((tm, k), lambda i, j: (i, 0)),
                  pl.BlockSpec((1, k), lambda i, j: (0, 0)),
                  pl.BlockSpec((k, DA_DIM), lambda i, j: (0, j))],
        out_specs=[blk] * 5,
        out_shape=[jax.ShapeDtypeStruct((m, DA_DIM), dt) for dt in (BF16, F32, F32, BF16, BF16)],
        scratch_shapes=[pltpu.VMEM((tm, k), BF16)],
        compiler_params=_cparams(("parallel", "arbitrary")),
        name="da_proj",
    )(x, norm_g.reshape(1, -1), w_da)


def _mm_down_kernel(a_ref, w_ref, res_ref, g_ref, o_ref, acc_ref, *, final_norm):
    kk = pl.program_id(1)

    @pl.when(kk == 0)
    def _():
        acc_ref[...] = jnp.zeros_like(acc_ref)

    acc_ref[...] += _dot(a_ref[...], w_ref[...])

    @pl.when(kk == pl.num_programs(1) - 1)
    def _():
        x = acc_ref[...] + res_ref[...]
        o_ref[...] = _rms_rows(x, g_ref[...]) if final_norm else x


def _mm_down(a, w, res, final_g, final_norm, tm=512, tk=512):
    m, k = a.shape
    n = w.shape[1]
    tm = min(tm, m)
    assert m % tm == 0 and k % tk == 0
    return pl.pallas_call(
        functools.partial(_mm_down_kernel, final_norm=final_norm),
        grid=(m // tm, k // tk),
        in_specs=[pl.BlockSpec((tm, tk), lambda i, kk: (i, kk)),
                  pl.BlockSpec((tk, n), lambda i, kk: (kk, 0)),
                  pl.BlockSpec((tm, n), lambda i, kk: (i, 0)),
                  pl.BlockSpec((1, n), lambda i, kk: (0, 0))],
        out_specs=pl.BlockSpec((tm, n), lambda i, kk: (i, 0)),
        out_shape=jax.ShapeDtypeStruct((m, n), F32),
        scratch_shapes=[pltpu.VMEM((tm, n), F32)],
        compiler_params=_cparams(("parallel", "arbitrary")),
        name="ffn_down",
    )(a, w, res, final_g.reshape(1, -1))


def _prep_kernel(p_ref, prev_ref, mu_ref, w0_ref, w2_ref, a0_ref, a2_ref, g2_ref, kk_ref, ka_ref,
                 r_o, k_o, v_o, kk_o, a_o, lw_o, g_o, *, rolled):
    p = p_ref[...]
    if rolled:
        row = lax.broadcasted_iota(jnp.int32, p.shape, 0)
        p_prev = jnp.where(row == 0, prev_ref[0], pltpu.roll(p, 1, 0))
    else:
        p_prev = prev_ref[...]
    ps = p + (p_prev - p) * mu_ref[...]
    o1, o2, o3 = RWKV_DIM, 2 * RWKV_DIM, 3 * RWKV_DIM
    o4 = o3 + DECAY_LORA
    o5 = o4 + AAA_LORA
    r, k, v = ps[:, :o1], ps[:, o1:o2], ps[:, o2:o3]
    wd, ad, gd = ps[:, o3:o4], ps[:, o4:o5], ps[:, o5:]
    z = w0_ref[...] + _dot(jnp.tanh(wd).astype(BF16), w2_ref[...])
    w_log = -(jnp.maximum(-z, 0.0) + jnp.log(1.0 + jnp.exp(-jnp.abs(z)))) - 0.5
    lw_o[...] = -jnp.exp(w_log)
    a = _sigmoid(a0_ref[...] + _dot(ad.astype(BF16), a2_ref[...]))
    g_o[...] = _dot(_sigmoid(gd).astype(BF16), g2_ref[...])
    ones = _head_ones()
    kraw = k * kk_ref[...]
    for pr in range(PAIRS):
        sl = slice(pr * LANES, (pr + 1) * LANES)
        kp = kraw[:, sl]
        ss = _split_dot(kp * kp, ones, 2)
        kk_o[:, sl] = kp * lax.rsqrt(jnp.maximum(ss, 1e-24))
    r_o[...] = r
    v_o[...] = v
    a_o[...] = a
    k_o[...] = k * (1.0 + (a - 1.0) * ka_ref[...])


def _rwkv_prep(p, prev, lp, *, rolled, tt):
    m = p.shape[0]
    row = lambda x: x.reshape(1, -1)
    full = lambda shp: pl.BlockSpec(shp, lambda i: (0,) * len(shp))
    prev_spec = (pl.BlockSpec((1, 1, RWKV_PROJ), lambda i: (i, 0, 0)) if rolled
                 else pl.BlockSpec((tt, RWKV_PROJ), lambda i: (i, 0)))
    out_blk = pl.BlockSpec((tt, RWKV_DIM), lambda i: (i, 0))
    return pl.pallas_call(
        functools.partial(_prep_kernel, rolled=rolled),
        grid=(m // tt,),
        in_specs=[pl.BlockSpec((tt, RWKV_PROJ), lambda i: (i, 0)), prev_spec,
                  full((1, RWKV_PROJ)), full((1, RWKV_DIM)), full((DECAY_LORA, RWKV_DIM)),
                  full((1, RWKV_DIM)), full((AAA_LORA, RWKV_DIM)), full((GATE_LORA, RWKV_DIM)),
                  full((1, RWKV_DIM)), full((1, RWKV_DIM))],
        out_specs=[out_blk] * 7,
        out_shape=[jax.ShapeDtypeStruct((m, RWKV_DIM), F32)] * 7,
        compiler_params=_cparams(("parallel",)),
        name="rwkv_prep",
    )(p, prev, row(lp['mu']), row(lp['w0']), lp['w2'].astype(BF16), row(lp['a0']), lp['a2'].astype(BF16),
      lp['g2'].astype(BF16), row(lp['k_k']), row(lp['k_a']))


def _chunk_kernel(r_ref, k_ref, v_ref, kk_ref, a_ref, lw_ref, y_ref, s_out_ref, s_ref):
    c = pl.program_id(1)

    @pl.when(c == 0)
    def _():
        s_ref[...] = jnp.zeros_like(s_ref)

    lane = lax.broadcasted_iota(jnp.int32, (1, LANES), 1)
    m0 = jnp.where(lane < HEAD, 1.0, 0.0)
    m1 = 1.0 - m0
    ii = lax.broadcasted_iota(jnp.int32, (LANES, LANES), 0)
    jj = lax.broadcasted_iota(jnp.int32, (LANES, LANES), 1)
    strict = ii > jj
    incl = ii >= jj
    eye = jnp.where(ii == jj, 1.0, 0.0)
    ti = lax.broadcasted_iota(jnp.int32, (CHUNK, CHUNK), 0)
    tj = lax.broadcasted_iota(jnp.int32, (CHUNK, CHUNK), 1)
    tril = jnp.where(ti >= tj, 1.0, 0.0).astype(BF16)

    def stack(x):
        return jnp.concatenate([x * m0, x * m1], axis=0)

    pairs = range(PAIRS)
    sls = [slice(pr * LANES, (pr + 1) * LANES) for pr in pairs]
    each = lambda fn, *cols: [fn(*xs) for xs in zip(*cols)]
    lw = [lw_ref[:, sl] for sl in sls]
    kk = [kk_ref[:, sl] for sl in sls]
    bb = [x * a_ref[:, sl] for x, sl in zip(kk, sls)]
    k2 = [k_ref[:, sl] for sl in sls]
    cum = [_split_dot(x, tril, 3, left=True) for x in lw]
    tot = [x[CHUNK - 1:CHUNK, :] for x in cum]
    inv_g = [jnp.exp(-x) for x in cum]
    to_end = each(lambda t, x: jnp.exp(t - x), tot, cum)
    a_t = each(lambda x, cm, l: stack(-x * jnp.exp(cm - l)).astype(BF16), kk, cum, lw)
    b_t = each(lambda x, g: stack(x * g).astype(BF16), bb, inv_g)
    k_t = each(lambda x, g: stack(x * g).astype(BF16), k2, inv_g)
    r_f = [stack(r_ref[:, sl] * jnp.exp(cm)) for sl, cm in zip(sls, cum)]
    r_t = [x.astype(BF16) for x in r_f]
    v_f = [stack(v_ref[:, sl]) for sl in sls]
    v_s = [x.astype(BF16) for x in v_f]
    bk_end = each(lambda x, y, e: jnp.concatenate([stack(x * e), stack(y * e)], axis=0).astype(BF16), bb, k2, to_end)

    a_ab = each(lambda x, y: jnp.where(strict, _dot_nt(x, y), 0.0), a_t, b_t)
    a_ak = each(lambda x, y: jnp.where(strict, _dot_nt(x, y), 0.0).astype(BF16), a_t, k_t)
    a_rb = each(lambda x, y: jnp.where(incl, _dot_nt(x, y), 0.0).astype(BF16), r_t, b_t)
    a_rk = each(lambda x, y: jnp.where(incl, _dot_nt(x, y), 0.0).astype(BF16), r_t, k_t)
    inv = [eye + x for x in a_ab]
    power = [x.astype(BF16) for x in a_ab]
    for _ in range(5):
        power = [_dot(x, x).astype(BF16) for x in power]
        inv = each(lambda x, p: x + _dot(x.astype(BF16), p), inv, power)
    w1 = each(lambda x, y: _dot(x, y).astype(BF16), a_ak, v_s)
    pu = each(lambda x, y, z: _dot(x.astype(BF16), jnp.concatenate([y, z], axis=1)), inv, a_t, w1)
    pu_b = [x.astype(BF16) for x in pu]
    qy = each(_dot, a_rb, pu_b)
    q_m = each(lambda x, y: (x + y[:, :LANES]).astype(BF16), r_f, qy)
    y_v = each(lambda x, y, z: x[:, LANES:] + _dot(y, z), qy, a_rk, v_s)
    s_old = [s_ref[pr] for pr in pairs]
    uy = each(lambda x, y, s: _dot_nt(jnp.concatenate([x[:, :LANES], y], axis=0), s.astype(BF16)), pu_b, q_m, s_old)
    uv_t = each(lambda x, y, z: jnp.concatenate([x[:LANES] + y[:, LANES:], z], axis=0).T.astype(BF16), uy, pu, v_f)
    s_new = each(lambda s, t, x, y: s * jnp.exp(t) + _dot(x, y), s_old, tot, uv_t, bk_end)
    for pr in pairs:
        s_ref[pr] = s_new[pr]
        y_m = uy[pr][LANES:] + y_v[pr]
        y_ref[:, sls[pr]] = y_m[:CHUNK] + y_m[CHUNK:]

    @pl.when(c == pl.num_programs(1) - 1)
    def _():
        s_out_ref[0] = s_ref[...]


def _rwkv_chunk_scan(r, k2, v, kk, a, lw, batch):
    m = r.shape[0]
    nc = m // batch // CHUNK
    blk = pl.BlockSpec((CHUNK, RWKV_DIM), lambda b, c: (b * nc + c, 0))
    y, s_pairs = pl.pallas_call(
        _chunk_kernel,
        grid=(batch, nc),
        in_specs=[blk] * 6,
        out_specs=[blk, pl.BlockSpec((1, PAIRS, LANES, LANES), lambda b, c: (b, 0, 0, 0))],
        out_shape=[jax.ShapeDtypeStruct((m, RWKV_DIM), F32),
                   jax.ShapeDtypeStruct((batch, PAIRS, LANES, LANES), F32)],
        scratch_shapes=[pltpu.VMEM((PAIRS, LANES, LANES), F32)],
        compiler_params=_cparams(("parallel", "arbitrary")),
        name="rwkv_chunk",
    )(r, k2, v, kk, a, lw)
    s_fin = jnp.stack([s_pairs[:, :, :HEAD, :HEAD], s_pairs[:, :, HEAD:, HEAD:]], axis=2)
    return y, s_fin.reshape(batch, RWKV_HEADS, HEAD, HEAD)


def _step_kernel(s_ref, r_ref, k_ref, kk_ref, a_ref, lw_ref, v_ref, s_o, y_o):
    s = s_ref[0]
    kk = kk_ref[0]
    sa = jnp.sum(s * (-kk), axis=-1, keepdims=True)
    s_new = s * jnp.exp(lw_ref[0]) + sa * (kk * a_ref[0]) + v_ref[0] * k_ref[0]
    s_o[0] = s_new
    y_o[0] = jnp.sum(s_new * r_ref[0], axis=-1, keepdims=True)


def _rwkv_step(s0, r, k2, v, kk, a, lw):
    b = s0.shape[0]
    rowv = lambda x: x.reshape(b, RWKV_HEADS, 1, HEAD)
    row_spec = pl.BlockSpec((1, RWKV_HEADS, 1, HEAD), lambda i: (i, 0, 0, 0))
    col_spec = pl.BlockSpec((1, RWKV_HEADS, HEAD, 1), lambda i: (i, 0, 0, 0))
    s_spec = pl.BlockSpec((1, RWKV_HEADS, HEAD, HEAD), lambda i: (i, 0, 0, 0))
    s_new, y = pl.pallas_call(
        _step_kernel,
        grid=(b,),
        in_specs=[s_spec] + [row_spec] * 5 + [col_spec],
        out_specs=[s_spec, col_spec],
        out_shape=[jax.ShapeDtypeStruct(s0.shape, F32), jax.ShapeDtypeStruct((b, RWKV_HEADS, HEAD, 1), F32)],
        compiler_params=_cparams(("parallel",)),
        name="rwkv_step",
    )(s0, rowv(r), rowv(k2), rowv(kk), rowv(a), rowv(lw), v.reshape(b, RWKV_HEADS, HEAD, 1))
    return y.reshape(b, RWKV_DIM), s_new


def _post_kernel(y_ref, r_ref, k_ref, v_ref, g_ref, rk_ref, lg_ref, lb_ref, o_ref):
    ones = _head_ones()
    for pr in range(PAIRS):
        sl = slice(pr * LANES, (pr + 1) * LANES)
        y = y_ref[:, sl]
        mean = _split_dot(y, ones, 2) * (1.0 / HEAD)
        d = y - mean
        var = _split_dot(d * d, ones, 2) * (1.0 / HEAD)
        yn = d * lax.rsqrt(var + GN_EPS) * lg_ref[:, sl] + lb_ref[:, sl]
        bonus = _split_dot(r_ref[:, sl] * k_ref[:, sl] * rk_ref[:, sl], ones, 2) * v_ref[:, sl]
        o_ref[:, sl] = ((yn + bonus) * g_ref[:, sl]).astype(o_ref.dtype)


def _rwkv_post(y, r, k2, v, g, lp, tt):
    m = y.shape[0]
    blk = pl.BlockSpec((tt, RWKV_DIM), lambda i: (i, 0))
    par = pl.BlockSpec((1, RWKV_DIM), lambda i: (0, 0))
    return pl.pallas_call(
        _post_kernel,
        grid=(m // tt,),
        in_specs=[blk] * 5 + [par] * 3,
        out_specs=blk,
        out_shape=jax.ShapeDtypeStruct((m, RWKV_DIM), BF16),
        compiler_params=_cparams(("parallel",)),
        name="rwkv_post",
    )(y, r, k2, v, g, lp['r_k'].reshape(1, -1), lp['lnx_g'].reshape(1, -1), lp['lnx_b'].reshape(1, -1))


def _lambda(lq1_ref, lk1_ref, lq2_ref, lk2_ref, lam_init):
    s1 = jnp.sum(lq1_ref[...] * lk1_ref[...], axis=-1, keepdims=True)
    s2 = jnp.sum(lq2_ref[...] * lk2_ref[...], axis=-1, keepdims=True)
    return jnp.exp(s1) - jnp.exp(s2) + lam_init


def _dattn_kernel(lq1_ref, lk1_ref, lq2_ref, lk2_ref, sg_ref, q_ref, k_ref, v_ref, o_ref,
                  m_ref, acc_ref, *, tq, lam_init):
    qi = pl.program_id(2)
    q = q_ref[...]
    lane = lax.broadcasted_iota(jnp.int32, (1, LANES), 1)
    zero = jnp.zeros_like(q)
    q_maps = (jnp.where(lane < DA_QK, q, zero), jnp.where(lane >= DA_QK, q, zero))
    m_ref[...] = jnp.full_like(m_ref, NEG_INF)
    acc_ref[...] = jnp.zeros_like(acc_ref)
    row = lax.broadcasted_iota(jnp.int32, (tq, tq), 0)
    col = lax.broadcasted_iota(jnp.int32, (tq, tq), 1)
    ones = jnp.ones((tq, LANES), BF16)

    def block(ki, masked):
        start = pl.multiple_of(ki * tq, tq)
        ks = k_ref[pl.ds(start, tq), :]
        v_aug = jnp.concatenate([v_ref[pl.ds(start, tq), :], ones], axis=1)
        s = [_dot_nt(qm, ks) for qm in q_maps]
        if masked:
            s = [jnp.where(col <= row, x, NEG_INF) for x in s]
        m_old = [m_ref[c] for c in range(2)]
        m_new = [jnp.maximum(mo, jnp.max(x, axis=-1, keepdims=True)) for mo, x in zip(m_old, s)]
        p = [jnp.exp(x - mn).astype(BF16) for x, mn in zip(s, m_new)]
        pv = [_dot(x, v_aug) for x in p]
        for c in range(2):
            acc_ref[c] = jnp.exp(m_old[c] - m_new[c]) * acc_ref[c] + pv[c]
            m_ref[c] = m_new[c]

    def body(ki, carry):
        block(ki, False)
        return carry

    lax.fori_loop(0, qi, body, 0)
    block(qi, True)
    lam = _lambda(lq1_ref, lk1_ref, lq2_ref, lk2_ref, lam_init)
    a0, a1 = acc_ref[0], acc_ref[1]
    o = a0[:, :LANES] / a0[:, LANES:] - lam * (a1[:, :LANES] / a1[:, LANES:])
    o_ref[...] = (_rms_rows(o, sg_ref[...]) * (1.0 - lam_init)).astype(o_ref.dtype)


def _diff_attn_prompt(qb, kb, vb, lp, lam_init, batch, tq=512):
    m = qb.shape[0]
    t = m // batch
    tq = min(tq, t)
    nq = t // tq
    vec = pl.BlockSpec((1, DA_QK), lambda b, h, i: (0, 0))
    kv_spec = pl.BlockSpec((t, LANES), lambda b, h, i: (b, h))
    return pl.pallas_call(
        functools.partial(_dattn_kernel, tq=tq, lam_init=lam_init),
        grid=(batch, DA_HEADS, nq),
        in_specs=[vec] * 4 + [pl.BlockSpec((1, DA_V), lambda b, h, i: (0, 0)),
                              pl.BlockSpec((tq, LANES), lambda b, h, i: (b * nq + i, h)), kv_spec, kv_spec],
        out_specs=pl.BlockSpec((tq, LANES), lambda b, h, i: (b * nq + i, h)),
        out_shape=jax.ShapeDtypeStruct((m, DA_DIM), BF16),
        scratch_shapes=[pltpu.VMEM((2, tq, 1), F32), pltpu.VMEM((2, tq, 2 * LANES), F32)],
        compiler_params=_cparams(("parallel", "parallel", "arbitrary")),
        name="diff_attn_prompt",
    )(*[lp[n].reshape(1, -1) for n in ('lq1', 'lk1', 'lq2', 'lk2')], lp['subln_g'].reshape(1, -1), qb, kb, vb)


def _paged_kernel(pt_ref, lq1_ref, lk1_ref, lq2_ref, lk2_ref, sg_ref, q_ref, kn_ref, vn_ref, *rest, pps, lam_init):
    del pt_ref
    k_refs, v_refs = rest[:pps], rest[pps:2 * pps]
    o_ref, m_ref, l_ref, acc_ref = rest[2 * pps:]
    pg = pl.program_id(1)

    @pl.when(pg == 0)
    def _():
        m_ref[...] = jnp.full_like(m_ref, NEG_INF)
        l_ref[...] = jnp.zeros_like(l_ref)
        acc_ref[...] = jnp.zeros_like(acc_ref)

    q = q_ref[0]
    lane_in = lax.broadcasted_iota(jnp.int32, (LANES, LANES), 0)
    half_ones = (jnp.where(lane_in < DA_QK, 1.0, 0.0).astype(BF16), jnp.where(lane_in >= DA_QK, 1.0, 0.0).astype(BF16))

    def update(k3, v3):
        n = k3.shape[0]
        prod = (k3 * q).reshape(n * DA_HEADS, LANES).astype(BF16)
        for c in range(2):
            s = _dot(prod, half_ones[c]).reshape(n, DA_HEADS, LANES)
            m_old = m_ref[c]
            m_new = jnp.maximum(m_old, jnp.max(s, axis=0))
            alpha = jnp.exp(m_old - m_new)
            p = jnp.exp(s - m_new)
            l_ref[c] = alpha * l_ref[c] + jnp.sum(p, axis=0)
            acc_ref[c] = alpha * acc_ref[c] + jnp.sum(p * v3, axis=0)
            m_ref[c] = m_new

    for k_ref, v_ref in zip(k_refs, v_refs):
        update(k_ref[0], v_ref[0])

    @pl.when(pg == pl.num_programs(1) - 1)
    def _():
        update(kn_ref[...], vn_ref[...])
        lam = _lambda(lq1_ref, lk1_ref, lq2_ref, lk2_ref, lam_init)
        o = acc_ref[0] / l_ref[0] - lam * (acc_ref[1] / l_ref[1])
        o_ref[0] = (_rms_rows(o, sg_ref[...]) * (1.0 - lam_init)).astype(o_ref.dtype)


def _diff_attn_paged(q, k_new, v_new, cache_k, cache_v, page_table, lp, lam_init):
    b, n_pages = page_table.shape
    page = cache_k.shape[1]
    pps = next(c for c in (4, 2, 1) if n_pages % c == 0)
    vec = pl.BlockSpec((1, DA_QK), lambda i, p, pt: (0, 0))
    rowb = pl.BlockSpec((1, DA_HEADS, LANES), lambda i, p, pt: (i, 0, 0))

    def page_spec(j):
        return pl.BlockSpec((1, page, DA_HEADS, LANES),
                            lambda i, p, pt: (pt[i * n_pages + p * pps + j], 0, 0, 0))

    grid_spec = pltpu.PrefetchScalarGridSpec(
        num_scalar_prefetch=1,
        grid=(b, n_pages // pps),
        in_specs=([vec] * 4 + [pl.BlockSpec((1, DA_V), lambda i, p, pt: (0, 0)), rowb, rowb, rowb]
                  + [page_spec(j) for j in range(pps)] * 2),
        out_specs=rowb,
        scratch_shapes=[pltpu.VMEM((2, DA_HEADS, LANES), F32)] * 3,
    )
    r3 = lambda x: x.reshape(b, DA_HEADS, LANES)
    out = pl.pallas_call(
        functools.partial(_paged_kernel, pps=pps, lam_init=lam_init),
        grid_spec=grid_spec,
        out_shape=jax.ShapeDtypeStruct((b, DA_HEADS, LANES), BF16),
        compiler_params=_cparams(("parallel", "arbitrary")),
        name="diff_attn_paged",
    )(page_table.reshape(-1), *[lp[n].reshape(1, -1) for n in ('lq1', 'lk1', 'lq2', 'lk2')],
      lp['subln_g'].reshape(1, -1), r3(q), r3(k_new), r3(v_new), *([cache_k] * pps), *([cache_v] * pps))
    return out.reshape(b, DA_DIM)


def _cross_kernel(q_ref, mk_ref, mv_ref, o_ref):
    s = _dot_nt(q_ref[0], mk_ref[0].astype(BF16))
    m = jnp.max(s, axis=-1, keepdims=True)
    p = jnp.exp(s - m)
    l = jnp.sum(p, axis=-1, keepdims=True)
    o_ref[0] = (_dot(p.astype(BF16), mv_ref[0].astype(BF16)) / l).astype(o_ref.dtype)


def _cross_attn(q, mk, mv, tq=512):
    b, t, _ = q.shape
    mt = mk.shape[1]
    tq = min(tq, t)
    q_spec = pl.BlockSpec((1, tq, CROSS_HD), lambda i, h, j: (i, j, h))
    m_spec = pl.BlockSpec((1, mt, CROSS_HD), lambda i, h, j: (i, 0, h))
    return pl.pallas_call(
        _cross_kernel,
        grid=(b, CROSS_HEADS, t // tq),
        in_specs=[q_spec, m_spec, m_spec],
        out_specs=q_spec,
        out_shape=jax.ShapeDtypeStruct(q.shape, BF16),
        compiler_params=_cparams(("parallel", "parallel", "parallel")),
        name="cross_attn",
    )(q, mk, mv)


def _layer(x, batch, lp, wts, lam_init, mk, mv, rwkv_fn, attn_fn):
    m = x.shape[0]
    t = m // batch
    pr = _mm([x], [wts['in_r']], name="in_proj_rwkv", norm_g=lp['norm_mix_g'], tm=512,
             tn=RWKV_PROJ // 2)
    qb, kf, vf, kb, vb = _da_proj(x, lp['norm_mix_g'], wts['in_da'])
    y_r, s_new = rwkv_fn(pr)
    y_d = attn_fn(qb, kf, vf, kb, vb)
    x = _mm([y_r, y_d], [wts['o_r'], wts['o_d']], name="out_proj", res=x, tm=1024, tn=1024)
    qc = _mm([x], [wts['cq']], name="cross_q", norm_g=lp['norm_cross_g'], scale=CROSS_HD ** -0.5, out_dtype=BF16,
             tm=1024, tn=1024)
    tq = t
    qc3 = qc.reshape(batch, t, D_MODEL)
    if t < 8:
        qc3 = jnp.pad(qc3, ((0, 0), (0, 8 - t), (0, 0)))
    oc = _cross_attn(qc3, mk, mv)[:, :tq].reshape(m, D_MODEL)
    x = _mm([oc], [wts['co']], name="cross_out", res=x, tm=1024, tn=1024)
    hmid = _mm([x], [wts['gate'], wts['up']], name="ffn_swiglu", norm_g=lp['norm_ffn_g'], swiglu=True,
               out_dtype=BF16, tm=1024, tn=512)
    x = _mm_down(hmid, wts['down'], x, wts['final_g'], wts['is_last'], tm=512, tk=hmid.shape[1] // 4)
    last_row = pr.reshape(batch, t, RWKV_PROJ)[:, -1]
    return x, kf, vf, s_new, last_row


def kernel(x_prompt, x_sample, cache_k, cache_v, cache_mem_k, cache_mem_v, state_wkv, state_shift, page_table, mem_prompt, norm_mix_g, w_in, tok_shift_mu, rwkv_w0, rwkv_w2, rwkv_a0, rwkv_a2, rwkv_g2, rwkv_k_k, rwkv_k_a, rwkv_r_k, rwkv_lnx_g, rwkv_lnx_b, diff_lq1, diff_lk1, diff_lq2, diff_lk2, diff_subln_g, w_o, norm_cross_g, norm_mem_g, w_cq, w_ck, w_cv, w_co, norm_ffn_g, w_gate, w_up, w_down, final_norm_g):
    b, t, _ = x_prompt.shape
    db, dt, _ = x_sample.shape
    depth = w_in.shape[0]
    n_mem = mem_prompt.shape[1]
    n_pool, page = cache_k.shape[1], cache_k.shape[2]
    xp = x_prompt.reshape(b * t, D_MODEL)
    xs = x_sample.reshape(db * dt, D_MODEL)
    outs = {n: [] for n in ('k_p', 'v_p', 'k_s', 'v_s', 'S_p', 'S_s', 'sh_p', 'sh_s', 'mk_p', 'mv_p')}
    for l in range(depth):
        lam_init = 0.8 - 0.6 * math.exp(-0.3 * l)
        lp = dict(norm_mix_g=norm_mix_g[l], mu=tok_shift_mu[l], w0=rwkv_w0[l], w2=rwkv_w2[l], a0=rwkv_a0[l],
                  a2=rwkv_a2[l], g2=rwkv_g2[l], k_k=rwkv_k_k[l], k_a=rwkv_k_a[l], r_k=rwkv_r_k[l],
                  lnx_g=rwkv_lnx_g[l], lnx_b=rwkv_lnx_b[l], lq1=diff_lq1[l], lk1=diff_lk1[l], lq2=diff_lq2[l],
                  lk2=diff_lk2[l], subln_g=diff_subln_g[l], norm_cross_g=norm_cross_g[l], norm_ffn_g=norm_ffn_g[l])
        wl = w_in[l].astype(BF16)
        wts = dict(in_r=wl[:, :RWKV_PROJ], in_da=wl[:, RWKV_PROJ:], o_r=w_o[l][:RWKV_DIM].astype(BF16),
                   o_d=w_o[l][RWKV_DIM:].astype(BF16), cq=w_cq[l].astype(BF16), co=w_co[l].astype(BF16),
                   gate=w_gate[l].astype(BF16), up=w_up[l].astype(BF16), down=w_down[l].astype(BF16),
                   final_g=final_norm_g, is_last=l == depth - 1)

        mem2 = mem_prompt.reshape(b * n_mem, D_MODEL)
        mk = _mm([mem2], [w_ck[l].astype(BF16)], name="mem_k", norm_g=norm_mem_g[l], tm=1024, tn=1024)
        mv = _mm([mem2], [w_cv[l].astype(BF16)], name="mem_v", norm_g=norm_mem_g[l], tm=1024, tn=1024)
        tt = min(256, t)

        def rwkv_prompt(pr):
            tiles = pr.reshape(b, t // tt, tt, RWKV_PROJ)
            prev = jnp.concatenate([jnp.zeros((b, 1, RWKV_PROJ), F32), tiles[:, :-1, -1]], axis=1)
            r, k2, v, kk, a, lw, g = _rwkv_prep(pr, prev.reshape(-1, 1, RWKV_PROJ), lp, rolled=True, tt=tt)
            y, s_fin = _rwkv_chunk_scan(r, k2, v, kk, a, lw, b)
            return _rwkv_post(y, r, k2, v, g, lp, tt), s_fin

        def attn_prompt(qb, kf, vf, kb, vb):
            return _diff_attn_prompt(qb, kb, vb, lp, lam_init, b)

        xp, kf, vf, s_new, last = _layer(xp, b, lp, wts, lam_init, mk.reshape(b, n_mem, D_MODEL),
                                         mv.reshape(b, n_mem, D_MODEL), rwkv_prompt, attn_prompt)
        outs['k_p'].append(kf.reshape(b, t, DA_HEADS, 2 * DA_QK))
        outs['v_p'].append(vf.reshape(b, t, DA_HEADS, DA_V))
        outs['S_p'].append(s_new)
        outs['sh_p'].append(last)
        outs['mk_p'].append(mk.reshape(b, n_mem, CROSS_HEADS, CROSS_HD))
        outs['mv_p'].append(mv.reshape(b, n_mem, CROSS_HEADS, CROSS_HD))

        assert dt == 1

        def rwkv_sample(pr):
            r, k2, v, kk, a, lw, g = _rwkv_prep(pr, state_shift[l], lp, rolled=False, tt=db)
            y, s_fin = _rwkv_step(state_wkv[l], r, k2, v, kk, a, lw)
            return _rwkv_post(y, r, k2, v, g, lp, db), s_fin

        def attn_sample(qb, kf, vf, kb, vb):
            return _diff_attn_paged(qb.astype(F32), kf, vf, cache_k[l], cache_v[l], page_table, lp, lam_init)

        xs, kf, vf, s_new, last = _layer(xs, db, lp, wts, lam_init, cache_mem_k[l].reshape(db, n_mem, D_MODEL),
                                         cache_mem_v[l].reshape(db, n_mem, D_MODEL), rwkv_sample, attn_sample)
        outs['k_s'].append(kf.reshape(db, dt, DA_HEADS, 2 * DA_QK))
        outs['v_s'].append(vf.reshape(db, dt, DA_HEADS, DA_V))
        outs['S_s'].append(s_new)
        outs['sh_s'].append(last)
    y_prompt = xp.reshape(b, t, D_MODEL)
    y_sample = xs.reshape(db, dt, D_MODEL)
    st = lambda n: jnp.stack(outs[n])
    return (y_prompt, y_sample, st('k_p'), st('v_p'), st('k_s'), st('v_s'), st('S_p'), st('S_s'),
            st('sh_p'), st('sh_s'), st('mk_p'), st('mv_p'))
```

```python
import functools
import math

import jax
import jax.numpy as jnp
from jax import lax
from jax.experimental import pallas as pl
from jax.experimental.pallas import tpu as pltpu

F32 = jnp.float32
BF16 = jnp.bfloat16

D_MODEL = 2048
RWKV_DIM = 1024
HEAD = 64
RWKV_HEADS = 16
DECAY_LORA = 64
AAA_LORA = 64
GATE_LORA = 128
RWKV_PROJ = 3 * RWKV_DIM + DECAY_LORA + AAA_LORA + GATE_LORA
DA_DIM = 1024
DA_QK = 64
DA_V = 128
DA_HEADS = 8
CROSS_HEADS = 4
CROSS_HD = D_MODEL // CROSS_HEADS
RMS_EPS = 1e-6
GN_EPS = 64e-5
NEG_INF = -1e30
LOG2_E = 1.4426950408889634

LANES = 128
CHUNK = 64
PAIRS = RWKV_HEADS // 2
VMEM_LIMIT = 48 * 1024 * 1024

_NT = (((1,), (1,)), ((), ()))


def _cparams(sem):
    return pltpu.CompilerParams(dimension_semantics=sem, vmem_limit_bytes=VMEM_LIMIT)


def _dot(a, b):
    return jnp.dot(a, b, preferred_element_type=F32)


def _dot_nt(a, b):
    return lax.dot_general(a, b, _NT, preferred_element_type=F32)


def _rms_rows(x, g):
    x = x.astype(F32)
    ms = jnp.mean(x * x, axis=-1, keepdims=True)
    return x * lax.rsqrt(ms + RMS_EPS) * g


def _sigmoid(x):
    return 1.0 / (1.0 + jnp.exp(-x))


def _split_dot(x, m_bf16, terms, left=False):
    acc = None
    rem = x
    for t in range(terms):
        piece = rem.astype(BF16)
        d = _dot(m_bf16, piece) if left else _dot(piece, m_bf16)
        acc = d if acc is None else acc + d
        if t + 1 < terms:
            rem = rem - piece.astype(F32)
    return acc


def _head_ones():
    i = lax.broadcasted_iota(jnp.int32, (LANES, LANES), 0) // HEAD
    j = lax.broadcasted_iota(jnp.int32, (LANES, LANES), 1) // HEAD
    return jnp.where(i == j, 1.0, 0.0).astype(BF16)


def _mm_kernel(*refs, n_a, has_norm, swiglu, has_res, scale):
    refs = list(refs)
    a_refs = [refs.pop(0) for _ in range(n_a)]
    g_ref = refs.pop(0) if has_norm else None
    w_refs = [refs.pop(0) for _ in range(2 if swiglu else n_a)]
    res_ref = refs.pop(0) if has_res else None
    o_ref = refs.pop(0)
    if has_norm:
        xn_ref = refs.pop(0)

        @pl.when(pl.program_id(1) == 0)
        def _():
            xn_ref[...] = _rms_rows(a_refs[0][...], g_ref[...]).astype(BF16)

        lhs = [xn_ref[...]]
    else:
        lhs = [r[...] for r in a_refs]
    if swiglu:
        gate = _dot(lhs[0], w_refs[0][...])
        up = _dot(lhs[0], w_refs[1][...])
        acc = gate * _sigmoid(gate) * up
    else:
        acc = _dot(lhs[0], w_refs[0][...])
        for a, w in zip(lhs[1:], w_refs[1:]):
            acc = acc + _dot(a, w[...])
    if scale != 1.0:
        acc = acc * scale
    if has_res:
        acc = acc + res_ref[...]
    o_ref[...] = acc.astype(o_ref.dtype)


def _mm(a_list, w_list, *, name, norm_g=None, swiglu=False, res=None, scale=1.0, out_dtype=F32, tm=512, tn=None):
    m = a_list[0].shape[0]
    n = w_list[0].shape[1]
    tm = min(tm, m)
    resident = tn is None
    tn = n if resident else min(tn, n)
    assert m % tm == 0 and n % tn == 0
    has_norm = norm_g is not None
    in_specs = [pl.BlockSpec((tm, a.shape[1]), lambda i, j: (i, 0)) for a in a_list]
    args = list(a_list)
    if has_norm:
        in_specs.append(pl.BlockSpec((1, a_list[0].shape[1]), lambda i, j: (0, 0)))
        args.append(norm_g.reshape(1, -1))
    for w in w_list:
        if resident:
            in_specs.append(pl.BlockSpec((w.shape[0], n), lambda i, j: (0, 0), pipeline_mode=pl.Buffered(1)))
        else:
            in_specs.append(pl.BlockSpec((w.shape[0], tn), lambda i, j: (0, j)))
        args.append(w)
    if res is not None:
        in_specs.append(pl.BlockSpec((tm, tn), lambda i, j: (i, j)))
        args.append(res)
    scratch = [pltpu.VMEM((tm, a_list[0].shape[1]), BF16)] if has_norm else []
    return pl.pallas_call(
        functools.partial(_mm_kernel, n_a=len(a_list), has_norm=has_norm, swiglu=swiglu,
                          has_res=res is not None, scale=scale),
        grid=(m // tm, n // tn),
        in_specs=in_specs,
        out_specs=pl.BlockSpec((tm, tn), lambda i, j: (i, j)),
        out_shape=jax.ShapeDtypeStruct((m, n), out_dtype),
        scratch_shapes=scratch,
        compiler_params=_cparams(("parallel", "arbitrary")),
        name=name,
    )(*args)


def _da_proj_kernel(x_ref, g_ref, w_ref, q_ref, kf_ref, vf_ref, kb_ref, vb_ref):
    xn = _rms_rows(x_ref[...], g_ref[...]).astype(BF16)
    q_ref[...] = (_dot(xn, w_ref[:, :DA_DIM]) * (DA_QK ** -0.5 * LOG2_E)).astype(BF16)
    k = _dot(xn, w_ref[:, DA_DIM:2 * DA_DIM])
    kf_ref[...] = k
    kb_ref[...] = k.astype(BF16)
    v = _dot(xn, w_ref[:, 2 * DA_DIM:])
    vf_ref[...] = v
    vb_ref[...] = v.astype(BF16)


def _da_proj(x, norm_g, w_da, tm=512):
    m, k = x.shape
    tm = min(tm, m)
    blk = pl.BlockSpec((tm, DA_DIM), lambda i: (i, 0))
    return pl.pallas_call(
        _da_proj_kernel,
        grid=(m // tm,),
        in_specs=[pl.BlockSpec((tm, k), lambda i: (i, 0)),
                  pl.BlockSpec((1, k), lambda i: (0, 0)),
                  pl.BlockSpec((k, 3 * DA_DIM), lambda i: (0, 0), pipeline_mode=pl.Buffered(1))],
        out_specs=[blk] * 5,
        out_shape=[jax.ShapeDtypeStruct((m, DA_DIM), dt) for dt in (BF16, F32, F32, BF16, BF16)],
        compiler_params=_cparams(("parallel",)),
        name="da_proj",
    )(x, norm_g.reshape(1, -1), w_da)


def _mm_down_kernel(a_ref, w_ref, res_ref, g_ref, o_ref, *, final_norm):
    kk = pl.program_id(1)

    @pl.when(kk == 0)
    def _():
        o_ref[...] = res_ref[...]

    o_ref[...] += _dot(a_ref[...], w_ref[...])

    if final_norm:
        @pl.when(kk == pl.num_programs(1) - 1)
        def _():
            o_ref[...] = _rms_rows(o_ref[...], g_ref[...])


def _mm_down(a, w, res, final_g, final_norm, tm=512, tk=512):
    m, k = a.shape
    n = w.shape[1]
    tm = min(tm, m)
    assert m % tm == 0 and k % tk == 0
    return pl.pallas_call(
        functools.partial(_mm_down_kernel, final_norm=final_norm),
        grid=(m // tm, k // tk),
        in_specs=[pl.BlockSpec((tm, tk), lambda i, kk: (i, kk)),
                  pl.BlockSpec((tk, n), lambda i, kk: (kk, 0)),
                  pl.BlockSpec((tm, n), lambda i, kk: (i, 0)),
                  pl.BlockSpec((1, n), lambda i, kk: (0, 0))],
        out_specs=pl.BlockSpec((tm, n), lambda i, kk: (i, 0)),
        out_shape=jax.ShapeDtypeStruct((m, n), F32),
        compiler_params=_cparams(("parallel", "arbitrary")),
        name="ffn_down",
    )(a, w, res, final_g.reshape(1, -1))


def _prep_kernel(p_ref, prev_ref, mu_ref, w0_ref, w2_ref, a0_ref, a2_ref, g2_ref, kk_ref, ka_ref,
                 r_o, k_o, v_o, kk_o, a_o, lw_o, g_o, *, rolled):
    p = p_ref[...]
    if rolled:
        row = lax.broadcasted_iota(jnp.int32, p.shape, 0)
        p_prev = jnp.where(row == 0, prev_ref[0], pltpu.roll(p, 1, 0))
    else:
        p_prev = prev_ref[...]
    ps = p + (p_prev - p) * mu_ref[...]
    o1, o2, o3 = RWKV_DIM, 2 * RWKV_DIM, 3 * RWKV_DIM
    o4 = o3 + DECAY_LORA
    o5 = o4 + AAA_LORA
    r, k, v = ps[:, :o1], ps[:, o1:o2], ps[:, o2:o3]
    wd, ad, gd = ps[:, o3:o4], ps[:, o4:o5], ps[:, o5:]
    z = w0_ref[...] + _dot(jnp.tanh(wd).astype(BF16), w2_ref[...])
    w_log = -(jnp.maximum(-z, 0.0) + jnp.log(1.0 + jnp.exp(-jnp.abs(z)))) - 0.5
    lw_o[...] = -jnp.exp(w_log)
    a = _sigmoid(a0_ref[...] + _dot(ad.astype(BF16), a2_ref[...]))
    g_o[...] = _dot(_sigmoid(gd).astype(BF16), g2_ref[...])
    ones = _head_ones()
    kraw = k * kk_ref[...]
    for pr in range(PAIRS):
        sl = slice(pr * LANES, (pr + 1) * LANES)
        kp = kraw[:, sl]
        ss = _split_dot(kp * kp, ones, 2)
        kk_o[:, sl] = kp * lax.rsqrt(jnp.maximum(ss, 1e-24))
    r_o[...] = r
    v_o[...] = v
    a_o[...] = a
    k_o[...] = k * (1.0 + (a - 1.0) * ka_ref[...])


def _rwkv_prep(p, prev, lp, *, rolled, tt):
    m = p.shape[0]
    row = lambda x: x.reshape(1, -1)
    full = lambda shp: pl.BlockSpec(shp, lambda i: (0,) * len(shp))
    prev_spec = (pl.BlockSpec((1, 1, RWKV_PROJ), lambda i: (i, 0, 0)) if rolled
                 else pl.BlockSpec((tt, RWKV_PROJ), lambda i: (i, 0)))
    out_blk = pl.BlockSpec((tt, RWKV_DIM), lambda i: (i, 0))
    return pl.pallas_call(
        functools.partial(_prep_kernel, rolled=rolled),
        grid=(m // tt,),
        in_specs=[pl.BlockSpec((tt, RWKV_PROJ), lambda i: (i, 0)), prev_spec,
                  full((1, RWKV_PROJ)), full((1, RWKV_DIM)), full((DECAY_LORA, RWKV_DIM)),
                  full((1, RWKV_DIM)), full((AAA_LORA, RWKV_DIM)), full((GATE_LORA, RWKV_DIM)),
                  full((1, RWKV_DIM)), full((1, RWKV_DIM))],
        out_specs=[out_blk] * 7,
        out_shape=[jax.ShapeDtypeStruct((m, RWKV_DIM), F32)] * 7,
        compiler_params=_cparams(("parallel",)),
        name="rwkv_prep",
    )(p, prev, row(lp['mu']), row(lp['w0']), lp['w2'].astype(BF16), row(lp['a0']), lp['a2'].astype(BF16),
      lp['g2'].astype(BF16), row(lp['k_k']), row(lp['k_a']))


def _chunk_kernel(r_ref, k_ref, v_ref, kk_ref, a_ref, lw_ref, y_ref, s_out_ref, s_ref):
    c = pl.program_id(1)

    @pl.when(c == 0)
    def _():
        s_ref[...] = jnp.zeros_like(s_ref)

    lane = lax.broadcasted_iota(jnp.int32, (1, LANES), 1)
    m0 = jnp.where(lane < HEAD, 1.0, 0.0)
    m1 = 1.0 - m0
    ii = lax.broadcasted_iota(jnp.int32, (LANES, LANES), 0)
    jj = lax.broadcasted_iota(jnp.int32, (LANES, LANES), 1)
    strict = ii > jj
    incl = ii >= jj
    eye = jnp.where(ii == jj, 1.0, 0.0)
    ti = lax.broadcasted_iota(jnp.int32, (CHUNK, CHUNK), 0)
    tj = lax.broadcasted_iota(jnp.int32, (CHUNK, CHUNK), 1)
    tril = jnp.where(ti >= tj, 1.0, 0.0).astype(BF16)

    def stack(x):
        return jnp.concatenate([x * m0, x * m1], axis=0)

    pairs = range(PAIRS)
    sls = [slice(pr * LANES, (pr + 1) * LANES) for pr in pairs]
    each = lambda fn, *cols: [fn(*xs) for xs in zip(*cols)]
    lw = [lw_ref[:, sl] for sl in sls]
    kk = [kk_ref[:, sl] for sl in sls]
    bb = [x * a_ref[:, sl] for x, sl in zip(kk, sls)]
    k2 = [k_ref[:, sl] for sl in sls]
    cum = [_split_dot(x, tril, 3, left=True) for x in lw]
    tot = [x[CHUNK - 1:CHUNK, :] for x in cum]
    inv_g = [jnp.exp(-x) for x in cum]
    to_end = each(lambda t, x: jnp.exp(t - x), tot, cum)
    a_t = each(lambda x, cm, l: stack(-x * jnp.exp(cm - l)).astype(BF16), kk, cum, lw)
    b_t = each(lambda x, g: stack(x * g).astype(BF16), bb, inv_g)
    k_t = each(lambda x, g: stack(x * g).astype(BF16), k2, inv_g)
    r_f = [stack(r_ref[:, sl] * jnp.exp(cm)) for sl, cm in zip(sls, cum)]
    r_t = [x.astype(BF16) for x in r_f]
    v_f = [stack(v_ref[:, sl]) for sl in sls]
    v_s = [x.astype(BF16) for x in v_f]
    bk_end = each(lambda x, y, e: jnp.concatenate([stack(x * e), stack(y * e)], axis=0).astype(BF16), bb, k2, to_end)

    a_ab = each(lambda x, y: jnp.where(strict, _dot_nt(x, y), 0.0), a_t, b_t)
    a_ak = each(lambda x, y: jnp.where(strict, _dot_nt(x, y), 0.0).astype(BF16), a_t, k_t)
    a_rb = each(lambda x, y: jnp.where(incl, _dot_nt(x, y), 0.0).astype(BF16), r_t, b_t)
    a_rk = each(lambda x, y: jnp.where(incl, _dot_nt(x, y), 0.0).astype(BF16), r_t, k_t)
    inv = [eye + x for x in a_ab]
    power = [x.astype(BF16) for x in a_ab]
    for _ in range(5):
        power = [_dot(x, x).astype(BF16) for x in power]
        inv = each(lambda x, p: x + _dot(x.astype(BF16), p), inv, power)
    w1 = each(lambda x, y: _dot(x, y).astype(BF16), a_ak, v_s)
    pu = each(lambda x, y, z: _dot(x.astype(BF16), jnp.concatenate([y, z], axis=1)), inv, a_t, w1)
    pu_b = [x.astype(BF16) for x in pu]
    qy = each(_dot, a_rb, pu_b)
    q_m = each(lambda x, y: (x + y[:, :LANES]).astype(BF16), r_f, qy)
    y_v = each(lambda x, y, z: x[:, LANES:] + _dot(y, z), qy, a_rk, v_s)
    s_old = [s_ref[pr] for pr in pairs]
    uy = each(lambda x, y, s: _dot_nt(jnp.concatenate([x[:, :LANES], y], axis=0), s.astype(BF16)), pu_b, q_m, s_old)
    uv_t = each(lambda x, y, z: jnp.concatenate([x[:LANES] + y[:, LANES:], z], axis=0).T.astype(BF16), uy, pu, v_f)
    s_new = each(lambda s, t, x, y: s * jnp.exp(t) + _dot(x, y), s_old, tot, uv_t, bk_end)
    for pr in pairs:
        s_ref[pr] = s_new[pr]
        y_m = uy[pr][LANES:] + y_v[pr]
        y_ref[:, sls[pr]] = y_m[:CHUNK] + y_m[CHUNK:]

    @pl.when(c == pl.num_programs(1) - 1)
    def _():
        s_out_ref[0] = s_ref[...]


def _rwkv_chunk_scan(r, k2, v, kk, a, lw, batch):
    m = r.shape[0]
    nc = m // batch // CHUNK
    blk = pl.BlockSpec((CHUNK, RWKV_DIM), lambda b, c: (b * nc + c, 0))
    y, s_pairs = pl.pallas_call(
        _chunk_kernel,
        grid=(batch, nc),
        in_specs=[blk] * 6,
        out_specs=[blk, pl.BlockSpec((1, PAIRS, LANES, LANES), lambda b, c: (b, 0, 0, 0))],
        out_shape=[jax.ShapeDtypeStruct((m, RWKV_DIM), F32),
                   jax.ShapeDtypeStruct((batch, PAIRS, LANES, LANES), F32)],
        scratch_shapes=[pltpu.VMEM((PAIRS, LANES, LANES), F32)],
        compiler_params=_cparams(("parallel", "arbitrary")),
        name="rwkv_chunk",
    )(r, k2, v, kk, a, lw)
    s_fin = jnp.stack([s_pairs[:, :, :HEAD, :HEAD], s_pairs[:, :, HEAD:, HEAD:]], axis=2)
    return y, s_fin.reshape(batch, RWKV_HEADS, HEAD, HEAD)


def _step_kernel(s_ref, r_ref, k_ref, kk_ref, a_ref, lw_ref, v_ref, s_o, y_o):
    s = s_ref[0]
    kk = kk_ref[0]
    sa = jnp.sum(s * (-kk), axis=-1, keepdims=True)
    s_new = s * jnp.exp(lw_ref[0]) + sa * (kk * a_ref[0]) + v_ref[0] * k_ref[0]
    s_o[0] = s_new
    y_o[0] = jnp.sum(s_new * r_ref[0], axis=-1, keepdims=True)


def _rwkv_step(s0, r, k2, v, kk, a, lw):
    b = s0.shape[0]
    rowv = lambda x: x.reshape(b, RWKV_HEADS, 1, HEAD)
    row_spec = pl.BlockSpec((1, RWKV_HEADS, 1, HEAD), lambda i: (i, 0, 0, 0))
    col_spec = pl.BlockSpec((1, RWKV_HEADS, HEAD, 1), lambda i: (i, 0, 0, 0))
    s_spec = pl.BlockSpec((1, RWKV_HEADS, HEAD, HEAD), lambda i: (i, 0, 0, 0))
    s_new, y = pl.pallas_call(
        _step_kernel,
        grid=(b,),
        in_specs=[s_spec] + [row_spec] * 5 + [col_spec],
        out_specs=[s_spec, col_spec],
        out_shape=[jax.ShapeDtypeStruct(s0.shape, F32), jax.ShapeDtypeStruct((b, RWKV_HEADS, HEAD, 1), F32)],
        compiler_params=_cparams(("parallel",)),
        name="rwkv_step",
    )(s0, rowv(r), rowv(k2), rowv(kk), rowv(a), rowv(lw), v.reshape(b, RWKV_HEADS, HEAD, 1))
    return y.reshape(b, RWKV_DIM), s_new


def _post_kernel(y_ref, r_ref, k_ref, v_ref, g_ref, rk_ref, lg_ref, lb_ref, o_ref):
    ones = _head_ones()
    for pr in range(PAIRS):
        sl = slice(pr * LANES, (pr + 1) * LANES)
        y = y_ref[:, sl]
        mean = _split_dot(y, ones, 2) * (1.0 / HEAD)
        d = y - mean
        var = _split_dot(d * d, ones, 2) * (1.0 / HEAD)
        yn = d * lax.rsqrt(var + GN_EPS) * lg_ref[:, sl] + lb_ref[:, sl]
        bonus = _split_dot(r_ref[:, sl] * k_ref[:, sl] * rk_ref[:, sl], ones, 2) * v_ref[:, sl]
        o_ref[:, sl] = ((yn + bonus) * g_ref[:, sl]).astype(o_ref.dtype)


def _rwkv_post(y, r, k2, v, g, lp, tt):
    m = y.shape[0]
    blk = pl.BlockSpec((tt, RWKV_DIM), lambda i: (i, 0))
    par = pl.BlockSpec((1, RWKV_DIM), lambda i: (0, 0))
    return pl.pallas_call(
        _post_kernel,
        grid=(m // tt,),
        in_specs=[blk] * 5 + [par] * 3,
        out_specs=blk,
        out_shape=jax.ShapeDtypeStruct((m, RWKV_DIM), BF16),
        compiler_params=_cparams(("parallel",)),
        name="rwkv_post",
    )(y, r, k2, v, g, lp['r_k'].reshape(1, -1), lp['lnx_g'].reshape(1, -1), lp['lnx_b'].reshape(1, -1))


def _lambda(lq1_ref, lk1_ref, lq2_ref, lk2_ref, lam_init):
    s1 = jnp.sum(lq1_ref[...] * lk1_ref[...], axis=-1, keepdims=True)
    s2 = jnp.sum(lq2_ref[...] * lk2_ref[...], axis=-1, keepdims=True)
    return jnp.exp(s1) - jnp.exp(s2) + lam_init


def _dattn_kernel(lq1_ref, lk1_ref, lq2_ref, lk2_ref, sg_ref, q_ref, k_ref, v_ref, o_ref,
                  m_ref, acc_ref, *, tq, lam_init):
    qi = pl.program_id(2)
    q = q_ref[...]
    lane = lax.broadcasted_iota(jnp.int32, (1, LANES), 1)
    zero = jnp.zeros_like(q)
    q_maps = (jnp.where(lane < DA_QK, q, zero), jnp.where(lane >= DA_QK, q, zero))
    m_ref[...] = jnp.full_like(m_ref, NEG_INF)
    acc_ref[...] = jnp.zeros_like(acc_ref)
    row = lax.broadcasted_iota(jnp.int32, (tq, tq), 0)
    col = lax.broadcasted_iota(jnp.int32, (tq, tq), 1)
    ones = jnp.ones((tq, LANES), BF16)

    def block(ki, masked):
        start = pl.multiple_of(ki * tq, tq)
        ks = k_ref[pl.ds(start, tq), :]
        v_aug = jnp.concatenate([v_ref[pl.ds(start, tq), :], ones], axis=1)
        s = [_dot_nt(qm, ks) for qm in q_maps]
        if masked:
            s = [jnp.where(col <= row, x, NEG_INF) for x in s]
        m_old = [m_ref[c] for c in range(2)]
        m_new = [jnp.maximum(mo, jnp.max(x, axis=-1, keepdims=True)) for mo, x in zip(m_old, s)]
        p = [jnp.exp2(x - mn).astype(BF16) for x, mn in zip(s, m_new)]
        pv = [_dot(x, v_aug) for x in p]
        for c in range(2):
            acc_ref[c] = jnp.exp2(m_old[c] - m_new[c]) * acc_ref[c] + pv[c]
            m_ref[c] = m_new[c]

    def body(ki, carry):
        block(ki, False)
        return carry

    lax.fori_loop(0, qi, body, 0)
    block(qi, True)
    lam = _lambda(lq1_ref, lk1_ref, lq2_ref, lk2_ref, lam_init)
    a0, a1 = acc_ref[0], acc_ref[1]
    o = a0[:, :LANES] / a0[:, LANES:] - lam * (a1[:, :LANES] / a1[:, LANES:])
    o_ref[...] = (_rms_rows(o, sg_ref[...]) * (1.0 - lam_init)).astype(o_ref.dtype)


def _diff_attn_prompt(qb, kb, vb, lp, lam_init, batch, tq=512):
    m = qb.shape[0]
    t = m // batch
    tq = min(tq, t)
    nq = t // tq
    vec = pl.BlockSpec((1, DA_QK), lambda b, h, i: (0, 0))
    kv_spec = pl.BlockSpec((t, LANES), lambda b, h, i: (b, h))
    return pl.pallas_call(
        functools.partial(_dattn_kernel, tq=tq, lam_init=lam_init),
        grid=(batch, DA_HEADS, nq),
        in_specs=[vec] * 4 + [pl.BlockSpec((1, DA_V), lambda b, h, i: (0, 0)),
                              pl.BlockSpec((tq, LANES), lambda b, h, i: (b * nq + i, h)), kv_spec, kv_spec],
        out_specs=pl.BlockSpec((tq, LANES), lambda b, h, i: (b * nq + i, h)),
        out_shape=jax.ShapeDtypeStruct((m, DA_DIM), BF16),
        scratch_shapes=[pltpu.VMEM((2, tq, 1), F32), pltpu.VMEM((2, tq, 2 * LANES), F32)],
        compiler_params=_cparams(("parallel", "parallel", "arbitrary")),
        name="diff_attn_prompt",
    )(*[lp[n].reshape(1, -1) for n in ('lq1', 'lk1', 'lq2', 'lk2')], lp['subln_g'].reshape(1, -1), qb, kb, vb)


def _paged_kernel(pt_ref, lq1_ref, lk1_ref, lq2_ref, lk2_ref, sg_ref, q_ref, kn_ref, vn_ref, *rest, pps, lam_init):
    del pt_ref
    k_refs, v_refs = rest[:pps], rest[pps:2 * pps]
    o_ref, m_ref, l_ref, acc_ref = rest[2 * pps:]
    pg = pl.program_id(1)

    @pl.when(pg == 0)
    def _():
        m_ref[...] = jnp.full_like(m_ref, NEG_INF)
        l_ref[...] = jnp.zeros_like(l_ref)
        acc_ref[...] = jnp.zeros_like(acc_ref)

    q = q_ref[0]
    lane_in = lax.broadcasted_iota(jnp.int32, (LANES, LANES), 0)
    pick = [jnp.where(lane_in == c * DA_QK, 1.0, 0.0).astype(BF16) for c in range(2)]
    half_sum = _head_ones()

    def map_lane(x, c):
        return x[:, c * DA_QK:c * DA_QK + 1]

    def update(k_list, v_list):
        n = k_list[0].shape[0]
        flat = lambda x: x.reshape(n * DA_HEADS, LANES).astype(BF16)
        tile = lambda x: x.reshape(n, DA_HEADS, LANES)
        s = [tile(_dot(flat(k3 * q), half_sum)) for k3 in k_list]
        m_old = m_ref[...]
        m_new = m_old
        for x in s:
            m_new = jnp.maximum(m_new, jnp.max(x, axis=0))
        alpha = jnp.exp2(m_old - m_new)
        p = [jnp.exp2(x - m_new) for x in s]
        l_ref[...] = alpha * l_ref[...] + sum(jnp.sum(x, axis=0) for x in p)
        m_ref[...] = m_new
        p_b = [flat(x) for x in p]
        for c in range(2):
            p_c = [tile(_dot(x, pick[c])) for x in p_b]
            acc_ref[c] = map_lane(alpha, c) * acc_ref[c] + sum(jnp.sum(x * v3, axis=0) for x, v3 in zip(p_c, v_list))

    update([r[0] for r in k_refs], [r[0] for r in v_refs])

    @pl.when(pg == pl.num_programs(1) - 1)
    def _():
        update([kn_ref[...]], [vn_ref[...]])
        lam = _lambda(lq1_ref, lk1_ref, lq2_ref, lk2_ref, lam_init)
        l = l_ref[...]
        o = acc_ref[0] / map_lane(l, 0) - lam * (acc_ref[1] / map_lane(l, 1))
        o_ref[0] = (_rms_rows(o, sg_ref[...]) * (1.0 - lam_init)).astype(o_ref.dtype)


def _diff_attn_paged(q, k_new, v_new, cache_k, cache_v, page_table, lp, lam_init):
    b, n_pages = page_table.shape
    page = cache_k.shape[1]
    pps = next(c for c in (4, 2, 1) if n_pages % c == 0)
    vec = pl.BlockSpec((1, DA_QK), lambda i, p, pt: (0, 0))
    rowb = pl.BlockSpec((1, DA_HEADS, LANES), lambda i, p, pt: (i, 0, 0))

    def page_spec(j):
        return pl.BlockSpec((1, page, DA_HEADS, LANES),
                            lambda i, p, pt: (pt[i * n_pages + p * pps + j], 0, 0, 0))

    grid_spec = pltpu.PrefetchScalarGridSpec(
        num_scalar_prefetch=1,
        grid=(b, n_pages // pps),
        in_specs=([vec] * 4 + [pl.BlockSpec((1, DA_V), lambda i, p, pt: (0, 0)), rowb, rowb, rowb]
                  + [page_spec(j) for j in range(pps)] * 2),
        out_specs=rowb,
        scratch_shapes=[pltpu.VMEM((DA_HEADS, LANES), F32)] * 2 + [pltpu.VMEM((2, DA_HEADS, LANES), F32)],
    )
    r3 = lambda x: x.reshape(b, DA_HEADS, LANES)
    out = pl.pallas_call(
        functools.partial(_paged_kernel, pps=pps, lam_init=lam_init),
        grid_spec=grid_spec,
        out_shape=jax.ShapeDtypeStruct((b, DA_HEADS, LANES), BF16),
        compiler_params=_cparams(("parallel", "arbitrary")),
        name="diff_attn_paged",
    )(page_table.reshape(-1), *[lp[n].reshape(1, -1) for n in ('lq1', 'lk1', 'lq2', 'lk2')],
      lp['subln_g'].reshape(1, -1), r3(q), r3(k_new), r3(v_new), *([cache_k] * pps), *([cache_v] * pps))
    return out.reshape(b, DA_DIM)


def _cross_kernel(q_ref, mk_ref, mv_ref, o_ref):
    for h in range(CROSS_HEADS):
        sl = slice(h * CROSS_HD, (h + 1) * CROSS_HD)
        s = _dot_nt(q_ref[0, :, sl], mk_ref[0, :, sl].astype(BF16))
        m = jnp.max(s, axis=-1, keepdims=True)
        p = jnp.exp(s - m)
        l = jnp.sum(p, axis=-1, keepdims=True)
        o_ref[0, :, sl] = (_dot(p.astype(BF16), mv_ref[0, :, sl].astype(BF16)) / l).astype(o_ref.dtype)


def _cross_attn(q, mk, mv, tq=512):
    b, t, _ = q.shape
    mt = mk.shape[1]
    tq = min(tq, t)
    q_spec = pl.BlockSpec((1, tq, D_MODEL), lambda i, j: (i, j, 0))
    m_spec = pl.BlockSpec((1, mt, D_MODEL), lambda i, j: (i, 0, 0))
    return pl.pallas_call(
        _cross_kernel,
        grid=(b, t // tq),
        in_specs=[q_spec, m_spec, m_spec],
        out_specs=q_spec,
        out_shape=jax.ShapeDtypeStruct(q.shape, BF16),
        compiler_params=_cparams(("parallel", "parallel")),
        name="cross_attn",
    )(q, mk, mv)


def _layer(x, batch, lp, wts, lam_init, mk, mv, rwkv_fn, attn_fn):
    m = x.shape[0]
    t = m // batch
    pr = _mm([x], [wts['in_r']], name="in_proj_rwkv", norm_g=lp['norm_mix_g'], tm=512)
    qb, kf, vf, kb, vb = _da_proj(x, lp['norm_mix_g'], wts['in_da'])
    y_r, s_new = rwkv_fn(pr)
    y_d = attn_fn(qb, kf, vf, kb, vb)
    x = _mm([y_r, y_d], [wts['o_r'], wts['o_d']], name="out_proj", res=x, tm=512)
    qc = _mm([x], [wts['cq']], name="cross_q", norm_g=lp['norm_cross_g'], scale=CROSS_HD ** -0.5, out_dtype=BF16,
             tm=1024)
    tq = t
    qc3 = qc.reshape(batch, t, D_MODEL)
    if t < 8:
        qc3 = jnp.pad(qc3, ((0, 0), (0, 8 - t), (0, 0)))
    oc = _cross_attn(qc3, mk, mv)[:, :tq].reshape(m, D_MODEL)
    x = _mm([oc], [wts['co']], name="cross_out", res=x, tm=512)
    hmid = _mm([x], [wts['gate'], wts['up']], name="ffn_swiglu", norm_g=lp['norm_ffn_g'], swiglu=True,
               out_dtype=BF16, tm=1024, tn=512)
    x = _mm_down(hmid, wts['down'], x, wts['final_g'], wts['is_last'], tm=1024, tk=512)
    last_row = pr.reshape(batch, t, RWKV_PROJ)[:, -1]
    return x, kf, vf, s_new, last_row


def kernel(x_prompt, x_sample, cache_k, cache_v, cache_mem_k, cache_mem_v, state_wkv, state_shift, page_table, mem_prompt, norm_mix_g, w_in, tok_shift_mu, rwkv_w0, rwkv_w2, rwkv_a0, rwkv_a2, rwkv_g2, rwkv_k_k, rwkv_k_a, rwkv_r_k, rwkv_lnx_g, rwkv_lnx_b, diff_lq1, diff_lk1, diff_lq2, diff_lk2, diff_subln_g, w_o, norm_cross_g, norm_mem_g, w_cq, w_ck, w_cv, w_co, norm_ffn_g, w_gate, w_up, w_down, final_norm_g):
    b, t, _ = x_prompt.shape
    db, dt, _ = x_sample.shape
    depth = w_in.shape[0]
    n_mem = mem_prompt.shape[1]
    n_pool, page = cache_k.shape[1], cache_k.shape[2]
    xp = x_prompt.reshape(b * t, D_MODEL)
    xs = x_sample.reshape(db * dt, D_MODEL)
    outs = {n: [] for n in ('k_p', 'v_p', 'k_s', 'v_s', 'S_p', 'S_s', 'sh_p', 'sh_s', 'mk_p', 'mv_p')}
    for l in range(depth):
        lam_init = 0.8 - 0.6 * math.exp(-0.3 * l)
        lp = dict(norm_mix_g=norm_mix_g[l], mu=tok_shift_mu[l], w0=rwkv_w0[l], w2=rwkv_w2[l], a0=rwkv_a0[l],
                  a2=rwkv_a2[l], g2=rwkv_g2[l], k_k=rwkv_k_k[l], k_a=rwkv_k_a[l], r_k=rwkv_r_k[l],
                  lnx_g=rwkv_lnx_g[l], lnx_b=rwkv_lnx_b[l], lq1=diff_lq1[l], lk1=diff_lk1[l], lq2=diff_lq2[l],
                  lk2=diff_lk2[l], subln_g=diff_subln_g[l], norm_cross_g=norm_cross_g[l], norm_ffn_g=norm_ffn_g[l])
        wl = w_in[l].astype(BF16)
        wts = dict(in_r=wl[:, :RWKV_PROJ], in_da=wl[:, RWKV_PROJ:], o_r=w_o[l][:RWKV_DIM].astype(BF16),
                   o_d=w_o[l][RWKV_DIM:].astype(BF16), cq=w_cq[l].astype(BF16), co=w_co[l].astype(BF16),
                   gate=w_gate[l].astype(BF16), up=w_up[l].astype(BF16), down=w_down[l].astype(BF16),
                   final_g=final_norm_g, is_last=l == depth - 1)

        mem2 = mem_prompt.reshape(b * n_mem, D_MODEL)
        mk = _mm([mem2], [w_ck[l].astype(BF16)], name="mem_k", norm_g=norm_mem_g[l], tm=1024)
        mv = _mm([mem2], [w_cv[l].astype(BF16)], name="mem_v", norm_g=norm_mem_g[l], tm=1024)
        tt = min(256, t)

        def rwkv_prompt(pr):
            tiles = pr.reshape(b, t // tt, tt, RWKV_PROJ)
            prev = jnp.concatenate([jnp.zeros((b, 1, RWKV_PROJ), F32), tiles[:, :-1, -1]], axis=1)
            r, k2, v, kk, a, lw, g = _rwkv_prep(pr, prev.reshape(-1, 1, RWKV_PROJ), lp, rolled=True, tt=tt)
            y, s_fin = _rwkv_chunk_scan(r, k2, v, kk, a, lw, b)
            return _rwkv_post(y, r, k2, v, g, lp, tt), s_fin

        def attn_prompt(qb, kf, vf, kb, vb):
            return _diff_attn_prompt(qb, kb, vb, lp, lam_init, b)

        xp, kf, vf, s_new, last = _layer(xp, b, lp, wts, lam_init, mk.reshape(b, n_mem, D_MODEL),
                                         mv.reshape(b, n_mem, D_MODEL), rwkv_prompt, attn_prompt)
        outs['k_p'].append(kf.reshape(b, t, DA_HEADS, 2 * DA_QK))
        outs['v_p'].append(vf.reshape(b, t, DA_HEADS, DA_V))
        outs['S_p'].append(s_new)
        outs['sh_p'].append(last)
        outs['mk_p'].append(mk.reshape(b, n_mem, CROSS_HEADS, CROSS_HD))
        outs['mv_p'].append(mv.reshape(b, n_mem, CROSS_HEADS, CROSS_HD))

        assert dt == 1

        def rwkv_sample(pr):
            r, k2, v, kk, a, lw, g = _rwkv_prep(pr, state_shift[l], lp, rolled=False, tt=db)
            y, s_fin = _rwkv_step(state_wkv[l], r, k2, v, kk, a, lw)
            return _rwkv_post(y, r, k2, v, g, lp, db), s_fin

        def attn_sample(qb, kf, vf, kb, vb):
            return _diff_attn_paged(qb.astype(F32), kf, vf, cache_k[l], cache_v[l], page_table, lp, lam_init)

        mem_ks = cache_mem_k[l].astype(BF16).reshape(db, n_mem, D_MODEL)
        mem_vs = cache_mem_v[l].astype(BF16).reshape(db, n_mem, D_MODEL)
        xs, kf, vf, s_new, last = _layer(xs, db, lp, wts, lam_init, mem_ks, mem_vs, rwkv_sample, attn_sample)
        outs['k_s'].append(kf.reshape(db, dt, DA_HEADS, 2 * DA_QK))
        outs['v_s'].append(vf.reshape(db, dt, DA_HEADS, DA_V))
        outs['S_s'].append(s_new)
        outs['sh_s'].append(last)
    y_prompt = xp.reshape(b, t, D_MODEL)
    y_sample = xs.reshape(db, dt, D_MODEL)
    st = lambda n: jnp.stack(outs[n])
    return (y_prompt, y_sample, st('k_p'), st('v_p'), st('k_s'), st('v_s'), st('S_p'), st('S_s'),
            st('sh_p'), st('sh_s'), st('mk_p'), st('mv_p'))
```

```python
import functools
import math

import jax
import jax.numpy as jnp
from jax import lax
from jax.experimental import pallas as pl
from jax.experimental.pallas import tpu as pltpu

F32 = jnp.float32
BF16 = jnp.bfloat16

D_MODEL = 2048
RWKV_DIM = 1024
HEAD = 64
RWKV_HEADS = 16
DECAY_LORA = 64
AAA_LORA = 64
GATE_LORA = 128
RWKV_PROJ = 3 * RWKV_DIM + DECAY_LORA + AAA_LORA + GATE_LORA
DA_DIM = 1024
DA_QK = 64
DA_V = 128
DA_HEADS = 8
CROSS_HEADS = 4
CROSS_HD = D_MODEL // CROSS_HEADS
RMS_EPS = 1e-6
GN_EPS = 64e-5
NEG_INF = -1e30
LOG2_E = 1.4426950408889634

LANES = 128
CHUNK = 64
PAIRS = RWKV_HEADS // 2
VMEM_LIMIT = 48 * 1024 * 1024

_NT = (((1,), (1,)), ((), ()))


def _cparams(sem):
    return pltpu.CompilerParams(dimension_semantics=sem, vmem_limit_bytes=VMEM_LIMIT)


def _dot(a, b):
    return jnp.dot(a, b, preferred_element_type=F32)


def _dot_nt(a, b):
    return lax.dot_general(a, b, _NT, preferred_element_type=F32)


def _rms_rows(x, g):
    x = x.astype(F32)
    ms = jnp.mean(x * x, axis=-1, keepdims=True)
    return x * lax.rsqrt(ms + RMS_EPS) * g


def _sigmoid(x):
    return 1.0 / (1.0 + jnp.exp(-x))


def _split_dot(x, m_bf16, terms, left=False):
    acc = None
    rem = x
    for t in range(terms):
        piece = rem.astype(BF16)
        d = _dot(m_bf16, piece) if left else _dot(piece, m_bf16)
        acc = d if acc is None else acc + d
        if t + 1 < terms:
            rem = rem - piece.astype(F32)
    return acc


def _head_ones():
    i = lax.broadcasted_iota(jnp.int32, (LANES, LANES), 0) // HEAD
    j = lax.broadcasted_iota(jnp.int32, (LANES, LANES), 1) // HEAD
    return jnp.where(i == j, 1.0, 0.0).astype(BF16)


def _mm_kernel(*refs, n_a, has_norm, swiglu, has_res, scale):
    refs = list(refs)
    a_refs = [refs.pop(0) for _ in range(n_a)]
    g_ref = refs.pop(0) if has_norm else None
    w_refs = [refs.pop(0) for _ in range(2 if swiglu else n_a)]
    res_ref = refs.pop(0) if has_res else None
    o_ref = refs.pop(0)
    if has_norm:
        xn_ref = refs.pop(0)

        @pl.when(pl.program_id(1) == 0)
        def _():
            xn_ref[...] = _rms_rows(a_refs[0][...], g_ref[...]).astype(BF16)

        lhs = [xn_ref[...]]
    else:
        lhs = [r[...] for r in a_refs]
    if swiglu:
        gate = _dot(lhs[0], w_refs[0][...])
        up = _dot(lhs[0], w_refs[1][...])
        acc = gate * _sigmoid(gate) * up
    else:
        acc = _dot(lhs[0], w_refs[0][...])
        for a, w in zip(lhs[1:], w_refs[1:]):
            acc = acc + _dot(a, w[...])
    if scale != 1.0:
        acc = acc * scale
    if has_res:
        acc = acc + res_ref[...]
    o_ref[...] = acc.astype(o_ref.dtype)


def _mm(a_list, w_list, *, name, norm_g=None, swiglu=False, res=None, scale=1.0, out_dtype=F32, tm=512, tn=None):
    m = a_list[0].shape[0]
    n = w_list[0].shape[1]
    tm = min(tm, m)
    resident = tn is None
    tn = n if resident else min(tn, n)
    assert m % tm == 0 and n % tn == 0
    has_norm = norm_g is not None
    in_specs = [pl.BlockSpec((tm, a.shape[1]), lambda i, j: (i, 0)) for a in a_list]
    args = list(a_list)
    if has_norm:
        in_specs.append(pl.BlockSpec((1, a_list[0].shape[1]), lambda i, j: (0, 0)))
        args.append(norm_g.reshape(1, -1))
    for w in w_list:
        if resident:
            in_specs.append(pl.BlockSpec((w.shape[0], n), lambda i, j: (0, 0), pipeline_mode=pl.Buffered(1)))
        else:
            in_specs.append(pl.BlockSpec((w.shape[0], tn), lambda i, j: (0, j)))
        args.append(w)
    if res is not None:
        in_specs.append(pl.BlockSpec((tm, tn), lambda i, j: (i, j)))
        args.append(res)
    scratch = [pltpu.VMEM((tm, a_list[0].shape[1]), BF16)] if has_norm else []
    return pl.pallas_call(
        functools.partial(_mm_kernel, n_a=len(a_list), has_norm=has_norm, swiglu=swiglu,
                          has_res=res is not None, scale=scale),
        grid=(m // tm, n // tn),
        in_specs=in_specs,
        out_specs=pl.BlockSpec((tm, tn), lambda i, j: (i, j)),
        out_shape=jax.ShapeDtypeStruct((m, n), out_dtype),
        scratch_shapes=scratch,
        compiler_params=_cparams(("parallel", "arbitrary")),
        name=name,
    )(*args)


def _da_proj_kernel(x_ref, g_ref, w_ref, q_ref, kf_ref, vf_ref, kb_ref, vb_ref):
    xn = _rms_rows(x_ref[...], g_ref[...]).astype(BF16)
    q_ref[...] = (_dot(xn, w_ref[:, :DA_DIM]) * (DA_QK ** -0.5 * LOG2_E)).astype(BF16)
    k = _dot(xn, w_ref[:, DA_DIM:2 * DA_DIM])
    kf_ref[...] = k
    kb_ref[...] = k.astype(BF16)
    v = _dot(xn, w_ref[:, 2 * DA_DIM:])
    vf_ref[...] = v
    vb_ref[...] = v.astype(BF16)


def _da_proj(x, norm_g, w_da, tm=512):
    m, k = x.shape
    tm = min(tm, m)
    blk = pl.BlockSpec((tm, DA_DIM), lambda i: (i, 0))
    return pl.pallas_call(
        _da_proj_kernel,
        grid=(m // tm,),
        in_specs=[pl.BlockSpec((tm, k), lambda i: (i, 0)),
                  pl.BlockSpec((1, k), lambda i: (0, 0)),
                  pl.BlockSpec((k, 3 * DA_DIM), lambda i: (0, 0), pipeline_mode=pl.Buffered(1))],
        out_specs=[blk] * 5,
        out_shape=[jax.ShapeDtypeStruct((m, DA_DIM), dt) for dt in (BF16, F32, F32, BF16, BF16)],
        compiler_params=_cparams(("parallel",)),
        name="da_proj",
    )(x, norm_g.reshape(1, -1), w_da)


def _mm_down_kernel(a_ref, w_ref, res_ref, g_ref, o_ref, *, final_norm):
    kk = pl.program_id(1)

    @pl.when(kk == 0)
    def _():
        o_ref[...] = res_ref[...]

    o_ref[...] += _dot(a_ref[...], w_ref[...])

    if final_norm:
        @pl.when(kk == pl.num_programs(1) - 1)
        def _():
            o_ref[...] = _rms_rows(o_ref[...], g_ref[...])


def _mm_down(a, w, res, final_g, final_norm, tm=512, tk=512):
    m, k = a.shape
    n = w.shape[1]
    tm = min(tm, m)
    assert m % tm == 0 and k % tk == 0
    return pl.pallas_call(
        functools.partial(_mm_down_kernel, final_norm=final_norm),
        grid=(m // tm, k // tk),
        in_specs=[pl.BlockSpec((tm, tk), lambda i, kk: (i, kk)),
                  pl.BlockSpec((tk, n), lambda i, kk: (kk, 0)),
                  pl.BlockSpec((tm, n), lambda i, kk: (i, 0)),
                  pl.BlockSpec((1, n), lambda i, kk: (0, 0))],
        out_specs=pl.BlockSpec((tm, n), lambda i, kk: (i, 0)),
        out_shape=jax.ShapeDtypeStruct((m, n), F32),
        compiler_params=_cparams(("parallel", "arbitrary")),
        name="ffn_down",
    )(a, w, res, final_g.reshape(1, -1))


def _prep_kernel(p_ref, prev_ref, mu_ref, w0_ref, w2_ref, a0_ref, a2_ref, g2_ref, kk_ref, ka_ref,
                 r_o, k_o, v_o, kk_o, a_o, lw_o, g_o, *, rolled):
    p = p_ref[...]
    if rolled:
        row = lax.broadcasted_iota(jnp.int32, p.shape, 0)
        p_prev = jnp.where(row == 0, prev_ref[0], pltpu.roll(p, 1, 0))
    else:
        p_prev = prev_ref[...]
    ps = p + (p_prev - p) * mu_ref[...]
    o1, o2, o3 = RWKV_DIM, 2 * RWKV_DIM, 3 * RWKV_DIM
    o4 = o3 + DECAY_LORA
    o5 = o4 + AAA_LORA
    r, k, v = ps[:, :o1], ps[:, o1:o2], ps[:, o2:o3]
    wd, ad, gd = ps[:, o3:o4], ps[:, o4:o5], ps[:, o5:]
    z = w0_ref[...] + _dot(jnp.tanh(wd).astype(BF16), w2_ref[...])
    w_log = -(jnp.maximum(-z, 0.0) + jnp.log(1.0 + jnp.exp(-jnp.abs(z)))) - 0.5
    lw_o[...] = -jnp.exp(w_log)
    a = _sigmoid(a0_ref[...] + _dot(ad.astype(BF16), a2_ref[...]))
    g_o[...] = _dot(_sigmoid(gd).astype(BF16), g2_ref[...])
    ones = _head_ones()
    kraw = k * kk_ref[...]
    for pr in range(PAIRS):
        sl = slice(pr * LANES, (pr + 1) * LANES)
        kp = kraw[:, sl]
        ss = _split_dot(kp * kp, ones, 2)
        kk_o[:, sl] = kp * lax.rsqrt(jnp.maximum(ss, 1e-24))
    r_o[...] = r
    v_o[...] = v
    a_o[...] = a
    k_o[...] = k * (1.0 + (a - 1.0) * ka_ref[...])


def _rwkv_prep(p, prev, lp, *, rolled, tt):
    m = p.shape[0]
    row = lambda x: x.reshape(1, -1)
    full = lambda shp: pl.BlockSpec(shp, lambda i: (0,) * len(shp))
    prev_spec = (pl.BlockSpec((1, 1, RWKV_PROJ), lambda i: (i, 0, 0)) if rolled
                 else pl.BlockSpec((tt, RWKV_PROJ), lambda i: (i, 0)))
    out_blk = pl.BlockSpec((tt, RWKV_DIM), lambda i: (i, 0))
    return pl.pallas_call(
        functools.partial(_prep_kernel, rolled=rolled),
        grid=(m // tt,),
        in_specs=[pl.BlockSpec((tt, RWKV_PROJ), lambda i: (i, 0)), prev_spec,
                  full((1, RWKV_PROJ)), full((1, RWKV_DIM)), full((DECAY_LORA, RWKV_DIM)),
                  full((1, RWKV_DIM)), full((AAA_LORA, RWKV_DIM)), full((GATE_LORA, RWKV_DIM)),
                  full((1, RWKV_DIM)), full((1, RWKV_DIM))],
        out_specs=[out_blk] * 7,
        out_shape=[jax.ShapeDtypeStruct((m, RWKV_DIM), F32)] * 7,
        compiler_params=_cparams(("parallel",)),
        name="rwkv_prep",
    )(p, prev, row(lp['mu']), row(lp['w0']), lp['w2'].astype(BF16), row(lp['a0']), lp['a2'].astype(BF16),
      lp['g2'].astype(BF16), row(lp['k_k']), row(lp['k_a']))


def _chunk_kernel(r_ref, k_ref, v_ref, kk_ref, a_ref, lw_ref, y_ref, s_out_ref, s_ref):
    c = pl.program_id(1)

    @pl.when(c == 0)
    def _():
        s_ref[...] = jnp.zeros_like(s_ref)

    lane = lax.broadcasted_iota(jnp.int32, (1, LANES), 1)
    m0 = jnp.where(lane < HEAD, 1.0, 0.0)
    m1 = 1.0 - m0
    ii = lax.broadcasted_iota(jnp.int32, (LANES, LANES), 0)
    jj = lax.broadcasted_iota(jnp.int32, (LANES, LANES), 1)
    strict = ii > jj
    incl = ii >= jj
    eye = jnp.where(ii == jj, 1.0, 0.0)
    ti = lax.broadcasted_iota(jnp.int32, (CHUNK, CHUNK), 0)
    tj = lax.broadcasted_iota(jnp.int32, (CHUNK, CHUNK), 1)
    tril = jnp.where(ti >= tj, 1.0, 0.0).astype(BF16)

    def stack(x):
        return jnp.concatenate([x * m0, x * m1], axis=0)

    pairs = range(PAIRS)
    sls = [slice(pr * LANES, (pr + 1) * LANES) for pr in pairs]
    each = lambda fn, *cols: [fn(*xs) for xs in zip(*cols)]
    lw = [lw_ref[:, sl] for sl in sls]
    kk = [kk_ref[:, sl] for sl in sls]
    bb = [x * a_ref[:, sl] for x, sl in zip(kk, sls)]
    k2 = [k_ref[:, sl] for sl in sls]
    cum = [_split_dot(x, tril, 3, left=True) for x in lw]
    tot = [x[CHUNK - 1:CHUNK, :] for x in cum]
    inv_g = [jnp.exp(-x) for x in cum]
    to_end = each(lambda t, x: jnp.exp(t - x), tot, cum)
    a_t = each(lambda x, cm, l: stack(-x * jnp.exp(cm - l)).astype(BF16), kk, cum, lw)
    b_t = each(lambda x, g: stack(x * g).astype(BF16), bb, inv_g)
    k_t = each(lambda x, g: stack(x * g).astype(BF16), k2, inv_g)
    r_f = [stack(r_ref[:, sl] * jnp.exp(cm)) for sl, cm in zip(sls, cum)]
    r_t = [x.astype(BF16) for x in r_f]
    v_f = [stack(v_ref[:, sl]) for sl in sls]
    v_s = [x.astype(BF16) for x in v_f]
    bk_end = each(lambda x, y, e: jnp.concatenate([stack(x * e), stack(y * e)], axis=0).astype(BF16), bb, k2, to_end)

    a_ab = each(lambda x, y: jnp.where(strict, _dot_nt(x, y), 0.0), a_t, b_t)
    a_ak = each(lambda x, y: jnp.where(strict, _dot_nt(x, y), 0.0).astype(BF16), a_t, k_t)
    a_rb = each(lambda x, y: jnp.where(incl, _dot_nt(x, y), 0.0).astype(BF16), r_t, b_t)
    a_rk = each(lambda x, y: jnp.where(incl, _dot_nt(x, y), 0.0).astype(BF16), r_t, k_t)
    inv = [eye + x for x in a_ab]
    power = [x.astype(BF16) for x in a_ab]
    for _ in range(5):
        power = [_dot(x, x).astype(BF16) for x in power]
        inv = each(lambda x, p: x + _dot(x.astype(BF16), p), inv, power)
    w1 = each(lambda x, y: _dot(x, y).astype(BF16), a_ak, v_s)
    pu = each(lambda x, y, z: _dot(x.astype(BF16), jnp.concatenate([y, z], axis=1)), inv, a_t, w1)
    pu_b = [x.astype(BF16) for x in pu]
    qy = each(_dot, a_rb, pu_b)
    q_m = each(lambda x, y: (x + y[:, :LANES]).astype(BF16), r_f, qy)
    y_v = each(lambda x, y, z: x[:, LANES:] + _dot(y, z), qy, a_rk, v_s)
    s_old = [s_ref[pr] for pr in pairs]
    uy = each(lambda x, y, s: _dot_nt(jnp.concatenate([x[:, :LANES], y], axis=0), s.astype(BF16)), pu_b, q_m, s_old)
    uv_t = each(lambda x, y, z: jnp.concatenate([x[:LANES] + y[:, LANES:], z], axis=0).T.astype(BF16), uy, pu, v_f)
    s_new = each(lambda s, t, x, y: s * jnp.exp(t) + _dot(x, y), s_old, tot, uv_t, bk_end)
    for pr in pairs:
        s_ref[pr] = s_new[pr]
        y_m = uy[pr][LANES:] + y_v[pr]
        y_ref[:, sls[pr]] = y_m[:CHUNK] + y_m[CHUNK:]

    @pl.when(c == pl.num_programs(1) - 1)
    def _():
        s_out_ref[0] = s_ref[...]


def _rwkv_chunk_scan(r, k2, v, kk, a, lw, batch):
    m = r.shape[0]
    nc = m // batch // CHUNK
    blk = pl.BlockSpec((CHUNK, RWKV_DIM), lambda b, c: (b * nc + c, 0))
    y, s_pairs = pl.pallas_call(
        _chunk_kernel,
        grid=(batch, nc),
        in_specs=[blk] * 6,
        out_specs=[blk, pl.BlockSpec((1, PAIRS, LANES, LANES), lambda b, c: (b, 0, 0, 0))],
        out_shape=[jax.ShapeDtypeStruct((m, RWKV_DIM), F32),
                   jax.ShapeDtypeStruct((batch, PAIRS, LANES, LANES), F32)],
        scratch_shapes=[pltpu.VMEM((PAIRS, LANES, LANES), F32)],
        compiler_params=_cparams(("parallel", "arbitrary")),
        name="rwkv_chunk",
    )(r, k2, v, kk, a, lw)
    s_fin = jnp.stack([s_pairs[:, :, :HEAD, :HEAD], s_pairs[:, :, HEAD:, HEAD:]], axis=2)
    return y, s_fin.reshape(batch, RWKV_HEADS, HEAD, HEAD)


def _step_kernel(s_ref, r_ref, k_ref, kk_ref, a_ref, lw_ref, v_ref, s_o, y_o):
    s = s_ref[0]
    kk = kk_ref[0]
    sa = jnp.sum(s * (-kk), axis=-1, keepdims=True)
    s_new = s * jnp.exp(lw_ref[0]) + sa * (kk * a_ref[0]) + v_ref[0] * k_ref[0]
    s_o[0] = s_new
    y_o[0] = jnp.sum(s_new * r_ref[0], axis=-1, keepdims=True)


def _rwkv_step(s0, r, k2, v, kk, a, lw):
    b = s0.shape[0]
    rowv = lambda x: x.reshape(b, RWKV_HEADS, 1, HEAD)
    row_spec = pl.BlockSpec((1, RWKV_HEADS, 1, HEAD), lambda i: (i, 0, 0, 0))
    col_spec = pl.BlockSpec((1, RWKV_HEADS, HEAD, 1), lambda i: (i, 0, 0, 0))
    s_spec = pl.BlockSpec((1, RWKV_HEADS, HEAD, HEAD), lambda i: (i, 0, 0, 0))
    s_new, y = pl.pallas_call(
        _step_kernel,
        grid=(b,),
        in_specs=[s_spec] + [row_spec] * 5 + [col_spec],
        out_specs=[s_spec, col_spec],
        out_shape=[jax.ShapeDtypeStruct(s0.shape, F32), jax.ShapeDtypeStruct((b, RWKV_HEADS, HEAD, 1), F32)],
        compiler_params=_cparams(("parallel",)),
        name="rwkv_step",
    )(s0, rowv(r), rowv(k2), rowv(kk), rowv(a), rowv(lw), v.reshape(b, RWKV_HEADS, HEAD, 1))
    return y.reshape(b, RWKV_DIM), s_new


def _post_kernel(y_ref, r_ref, k_ref, v_ref, g_ref, rk_ref, lg_ref, lb_ref, o_ref):
    ones = _head_ones()
    for pr in range(PAIRS):
        sl = slice(pr * LANES, (pr + 1) * LANES)
        y = y_ref[:, sl]
        mean = _split_dot(y, ones, 2) * (1.0 / HEAD)
        d = y - mean
        var = _split_dot(d * d, ones, 2) * (1.0 / HEAD)
        yn = d * lax.rsqrt(var + GN_EPS) * lg_ref[:, sl] + lb_ref[:, sl]
        bonus = _split_dot(r_ref[:, sl] * k_ref[:, sl] * rk_ref[:, sl], ones, 2) * v_ref[:, sl]
        o_ref[:, sl] = ((yn + bonus) * g_ref[:, sl]).astype(o_ref.dtype)


def _rwkv_post(y, r, k2, v, g, lp, tt):
    m = y.shape[0]
    blk = pl.BlockSpec((tt, RWKV_DIM), lambda i: (i, 0))
    par = pl.BlockSpec((1, RWKV_DIM), lambda i: (0, 0))
    return pl.pallas_call(
        _post_kernel,
        grid=(m // tt,),
        in_specs=[blk] * 5 + [par] * 3,
        out_specs=blk,
        out_shape=jax.ShapeDtypeStruct((m, RWKV_DIM), BF16),
        compiler_params=_cparams(("parallel",)),
        name="rwkv_post",
    )(y, r, k2, v, g, lp['r_k'].reshape(1, -1), lp['lnx_g'].reshape(1, -1), lp['lnx_b'].reshape(1, -1))


def _lambda(lq1_ref, lk1_ref, lq2_ref, lk2_ref, lam_init):
    s1 = jnp.sum(lq1_ref[...] * lk1_ref[...], axis=-1, keepdims=True)
    s2 = jnp.sum(lq2_ref[...] * lk2_ref[...], axis=-1, keepdims=True)
    return jnp.exp(s1) - jnp.exp(s2) + lam_init


def _dattn_kernel(lq1_ref, lk1_ref, lq2_ref, lk2_ref, sg_ref, q_ref, k_ref, v_ref, o_ref,
                  m_ref, acc_ref, *, tq, lam_init):
    nq = q_ref.shape[0] // tq
    lane = lax.broadcasted_iota(jnp.int32, (1, LANES), 1)
    row = lax.broadcasted_iota(jnp.int32, (tq, tq), 0)
    col = lax.broadcasted_iota(jnp.int32, (tq, tq), 1)
    ones = jnp.ones((tq, LANES), BF16)
    lam = _lambda(lq1_ref, lk1_ref, lq2_ref, lk2_ref, lam_init)
    rows = lambda ref, i: ref[i * tq:(i + 1) * tq, :]

    def scores(qi, j):
        q = rows(q_ref, qi)
        zero = jnp.zeros_like(q)
        ks = rows(k_ref, j)
        return [_dot_nt(jnp.where(lane < DA_QK, q, zero), ks), _dot_nt(jnp.where(lane >= DA_QK, q, zero), ks)]

    blocks = [(qi, j) for qi in range(nq) for j in range(qi + 1)]
    s_next = scores(*blocks[0])
    for n, (qi, j) in enumerate(blocks):
        s = s_next
        if n + 1 < len(blocks):
            s_next = scores(*blocks[n + 1])
        first, last = j == 0, j == qi
        if last:
            s = [jnp.where(col <= row, x, NEG_INF) for x in s]
        v_aug = jnp.concatenate([rows(v_ref, j), ones], axis=1)
        row_max = [jnp.max(x, axis=-1, keepdims=True) for x in s]
        m_old = None if first else [m_ref[c] for c in range(2)]
        m_new = row_max if first else [jnp.maximum(mo, mx) for mo, mx in zip(m_old, row_max)]
        p = [jnp.exp2(x - mn).astype(BF16) for x, mn in zip(s, m_new)]
        acc = [_dot(x, v_aug) for x in p]
        if not first:
            acc = [jnp.exp2(m_old[c] - m_new[c]) * acc_ref[c] + acc[c] for c in range(2)]
        if last:
            a0, a1 = acc
            o = a0[:, :LANES] / a0[:, LANES:] - lam * (a1[:, :LANES] / a1[:, LANES:])
            o_ref[qi * tq:(qi + 1) * tq, :] = (_rms_rows(o, sg_ref[...]) * (1.0 - lam_init)).astype(o_ref.dtype)
        else:
            for c in range(2):
                acc_ref[c] = acc[c]
                m_ref[c] = m_new[c]


def _diff_attn_prompt(qb, kb, vb, lp, lam_init, batch, tq=512):
    m = qb.shape[0]
    t = m // batch
    tq = min(tq, t)
    vec = pl.BlockSpec((1, DA_QK), lambda b, h: (0, 0))
    seq_spec = pl.BlockSpec((t, LANES), lambda b, h: (b, h))
    return pl.pallas_call(
        functools.partial(_dattn_kernel, tq=tq, lam_init=lam_init),
        grid=(batch, DA_HEADS),
        in_specs=[vec] * 4 + [pl.BlockSpec((1, DA_V), lambda b, h: (0, 0)), seq_spec, seq_spec, seq_spec],
        out_specs=seq_spec,
        out_shape=jax.ShapeDtypeStruct((m, DA_DIM), BF16),
        scratch_shapes=[pltpu.VMEM((2, tq, 1), F32), pltpu.VMEM((2, tq, 2 * LANES), F32)],
        compiler_params=_cparams(("parallel", "parallel")),
        name="diff_attn_prompt",
    )(*[lp[n].reshape(1, -1) for n in ('lq1', 'lk1', 'lq2', 'lk2')], lp['subln_g'].reshape(1, -1), qb, kb, vb)


def _paged_kernel(pt_ref, lq1_ref, lk1_ref, lq2_ref, lk2_ref, sg_ref, q_ref, kn_ref, vn_ref, *rest, pps, lam_init):
    del pt_ref
    k_refs, v_refs = rest[:pps], rest[pps:2 * pps]
    o_ref, m_ref, l_ref, acc_ref = rest[2 * pps:]
    pg = pl.program_id(1)

    @pl.when(pg == 0)
    def _():
        m_ref[...] = jnp.full_like(m_ref, NEG_INF)
        l_ref[...] = jnp.zeros_like(l_ref)
        acc_ref[...] = jnp.zeros_like(acc_ref)

    q = q_ref[0]
    lane_in = lax.broadcasted_iota(jnp.int32, (LANES, LANES), 0)
    pick = [jnp.where(lane_in == c * DA_QK, 1.0, 0.0).astype(BF16) for c in range(2)]
    half_sum = _head_ones()

    def map_lane(x, c):
        return x[:, c * DA_QK:c * DA_QK + 1]

    def update(k_list, v_list):
        n = k_list[0].shape[0]
        flat = lambda x: x.reshape(n * DA_HEADS, LANES).astype(BF16)
        tile = lambda x: x.reshape(n, DA_HEADS, LANES)
        s = [tile(_dot(flat(k3 * q), half_sum)) for k3 in k_list]
        m_old = m_ref[...]
        m_new = m_old
        for x in s:
            m_new = jnp.maximum(m_new, jnp.max(x, axis=0))
        alpha = jnp.exp2(m_old - m_new)
        p = [jnp.exp2(x - m_new) for x in s]
        l_ref[...] = alpha * l_ref[...] + sum(jnp.sum(x, axis=0) for x in p)
        m_ref[...] = m_new
        p_b = [flat(x) for x in p]
        for c in range(2):
            p_c = [tile(_dot(x, pick[c])) for x in p_b]
            acc_ref[c] = map_lane(alpha, c) * acc_ref[c] + sum(jnp.sum(x * v3, axis=0) for x, v3 in zip(p_c, v_list))

    update([r[0] for r in k_refs], [r[0] for r in v_refs])

    @pl.when(pg == pl.num_programs(1) - 1)
    def _():
        update([kn_ref[...]], [vn_ref[...]])
        lam = _lambda(lq1_ref, lk1_ref, lq2_ref, lk2_ref, lam_init)
        l = l_ref[...]
        o = acc_ref[0] / map_lane(l, 0) - lam * (acc_ref[1] / map_lane(l, 1))
        o_ref[0] = (_rms_rows(o, sg_ref[...]) * (1.0 - lam_init)).astype(o_ref.dtype)


def _diff_attn_paged(q, k_new, v_new, cache_k, cache_v, page_table, lp, lam_init):
    b, n_pages = page_table.shape
    page = cache_k.shape[1]
    pps = next(c for c in (8, 4, 2, 1) if n_pages % c == 0)
    vec = pl.BlockSpec((1, DA_QK), lambda i, p, pt: (0, 0))
    rowb = pl.BlockSpec((1, DA_HEADS, LANES), lambda i, p, pt: (i, 0, 0))

    def page_spec(j):
        return pl.BlockSpec((1, page, DA_HEADS, LANES),
                            lambda i, p, pt: (pt[i * n_pages + p * pps + j], 0, 0, 0))

    grid_spec = pltpu.PrefetchScalarGridSpec(
        num_scalar_prefetch=1,
        grid=(b, n_pages // pps),
        in_specs=([vec] * 4 + [pl.BlockSpec((1, DA_V), lambda i, p, pt: (0, 0)), rowb, rowb, rowb]
                  + [page_spec(j) for j in range(pps)] * 2),
        out_specs=rowb,
        scratch_shapes=[pltpu.VMEM((DA_HEADS, LANES), F32)] * 2 + [pltpu.VMEM((2, DA_HEADS, LANES), F32)],
    )
    r3 = lambda x: x.reshape(b, DA_HEADS, LANES)
    out = pl.pallas_call(
        functools.partial(_paged_kernel, pps=pps, lam_init=lam_init),
        grid_spec=grid_spec,
        out_shape=jax.ShapeDtypeStruct((b, DA_HEADS, LANES), BF16),
        compiler_params=_cparams(("parallel", "arbitrary")),
        name="diff_attn_paged",
    )(page_table.reshape(-1), *[lp[n].reshape(1, -1) for n in ('lq1', 'lk1', 'lq2', 'lk2')],
      lp['subln_g'].reshape(1, -1), r3(q), r3(k_new), r3(v_new), *([cache_k] * pps), *([cache_v] * pps))
    return out.reshape(b, DA_DIM)


def _cross_kernel(q_ref, mk_ref, mv_ref, o_ref):
    for h in range(CROSS_HEADS):
        sl = slice(h * CROSS_HD, (h + 1) * CROSS_HD)
        s = _dot_nt(q_ref[0, :, sl], mk_ref[0, :, sl].astype(BF16))
        m = jnp.max(s, axis=-1, keepdims=True)
        p = jnp.exp(s - m)
        l = jnp.sum(p, axis=-1, keepdims=True)
        o_ref[0, :, sl] = (_dot(p.astype(BF16), mv_ref[0, :, sl].astype(BF16)) / l).astype(o_ref.dtype)


def _cross_attn(q, mk, mv, tq=512):
    b, t, _ = q.shape
    mt = mk.shape[1]
    tq = min(tq, t)
    q_spec = pl.BlockSpec((1, tq, D_MODEL), lambda i, j: (i, j, 0))
    m_spec = pl.BlockSpec((1, mt, D_MODEL), lambda i, j: (i, 0, 0))
    return pl.pallas_call(
        _cross_kernel,
        grid=(b, t // tq),
        in_specs=[q_spec, m_spec, m_spec],
        out_specs=q_spec,
        out_shape=jax.ShapeDtypeStruct(q.shape, BF16),
        compiler_params=_cparams(("parallel", "parallel")),
        name="cross_attn",
    )(q, mk, mv)


def _layer(x, batch, lp, wts, lam_init, mk, mv, rwkv_fn, attn_fn):
    m = x.shape[0]
    t = m // batch
    pr = _mm([x], [wts['in_r']], name="in_proj_rwkv", norm_g=lp['norm_mix_g'], tm=512)
    qb, kf, vf, kb, vb = _da_proj(x, lp['norm_mix_g'], wts['in_da'])
    y_r, s_new = rwkv_fn(pr)
    y_d = attn_fn(qb, kf, vf, kb, vb)
    x = _mm([y_r, y_d], [wts['o_r'], wts['o_d']], name="out_proj", res=x, tm=512)
    qc = _mm([x], [wts['cq']], name="cross_q", norm_g=lp['norm_cross_g'], scale=CROSS_HD ** -0.5, out_dtype=BF16,
             tm=1024)
    tq = t
    qc3 = qc.reshape(batch, t, D_MODEL)
    if t < 8:
        qc3 = jnp.pad(qc3, ((0, 0), (0, 8 - t), (0, 0)))
    oc = _cross_attn(qc3, mk, mv)[:, :tq].reshape(m, D_MODEL)
    x = _mm([oc], [wts['co']], name="cross_out", res=x, tm=512)
    hmid = _mm([x], [wts['gate'], wts['up']], name="ffn_swiglu", norm_g=lp['norm_ffn_g'], swiglu=True,
               out_dtype=BF16, tm=1024, tn=512)
    x = _mm_down(hmid, wts['down'], x, wts['final_g'], wts['is_last'], tm=1024, tk=512)
    last_row = pr.reshape(batch, t, RWKV_PROJ)[:, -1]
    return x, kf, vf, s_new, last_row


def kernel(x_prompt, x_sample, cache_k, cache_v, cache_mem_k, cache_mem_v, state_wkv, state_shift, page_table, mem_prompt, norm_mix_g, w_in, tok_shift_mu, rwkv_w0, rwkv_w2, rwkv_a0, rwkv_a2, rwkv_g2, rwkv_k_k, rwkv_k_a, rwkv_r_k, rwkv_lnx_g, rwkv_lnx_b, diff_lq1, diff_lk1, diff_lq2, diff_lk2, diff_subln_g, w_o, norm_cross_g, norm_mem_g, w_cq, w_ck, w_cv, w_co, norm_ffn_g, w_gate, w_up, w_down, final_norm_g):
    b, t, _ = x_prompt.shape
    db, dt, _ = x_sample.shape
    depth = w_in.shape[0]
    n_mem = mem_prompt.shape[1]
    n_pool, page = cache_k.shape[1], cache_k.shape[2]
    xp = x_prompt.reshape(b * t, D_MODEL)
    xs = x_sample.reshape(db * dt, D_MODEL)
    outs = {n: [] for n in ('k_p', 'v_p', 'k_s', 'v_s', 'S_p', 'S_s', 'sh_p', 'sh_s', 'mk_p', 'mv_p')}
    for l in range(depth):
        lam_init = 0.8 - 0.6 * math.exp(-0.3 * l)
        lp = dict(norm_mix_g=norm_mix_g[l], mu=tok_shift_mu[l], w0=rwkv_w0[l], w2=rwkv_w2[l], a0=rwkv_a0[l],
                  a2=rwkv_a2[l], g2=rwkv_g2[l], k_k=rwkv_k_k[l], k_a=rwkv_k_a[l], r_k=rwkv_r_k[l],
                  lnx_g=rwkv_lnx_g[l], lnx_b=rwkv_lnx_b[l], lq1=diff_lq1[l], lk1=diff_lk1[l], lq2=diff_lq2[l],
                  lk2=diff_lk2[l], subln_g=diff_subln_g[l], norm_cross_g=norm_cross_g[l], norm_ffn_g=norm_ffn_g[l])
        wl = w_in[l].astype(BF16)
        wts = dict(in_r=wl[:, :RWKV_PROJ], in_da=wl[:, RWKV_PROJ:], o_r=w_o[l][:RWKV_DIM].astype(BF16),
                   o_d=w_o[l][RWKV_DIM:].astype(BF16), cq=w_cq[l].astype(BF16), co=w_co[l].astype(BF16),
                   gate=w_gate[l].astype(BF16), up=w_up[l].astype(BF16), down=w_down[l].astype(BF16),
                   final_g=final_norm_g, is_last=l == depth - 1)

        mem2 = mem_prompt.reshape(b * n_mem, D_MODEL)
        mk = _mm([mem2], [w_ck[l].astype(BF16)], name="mem_k", norm_g=norm_mem_g[l], tm=1024)
        mv = _mm([mem2], [w_cv[l].astype(BF16)], name="mem_v", norm_g=norm_mem_g[l], tm=1024)
        tt = min(256, t)

        def rwkv_prompt(pr):
            tiles = pr.reshape(b, t // tt, tt, RWKV_PROJ)
            prev = jnp.concatenate([jnp.zeros((b, 1, RWKV_PROJ), F32), tiles[:, :-1, -1]], axis=1)
            r, k2, v, kk, a, lw, g = _rwkv_prep(pr, prev.reshape(-1, 1, RWKV_PROJ), lp, rolled=True, tt=tt)
            y, s_fin = _rwkv_chunk_scan(r, k2, v, kk, a, lw, b)
            return _rwkv_post(y, r, k2, v, g, lp, tt), s_fin

        def attn_prompt(qb, kf, vf, kb, vb):
            return _diff_attn_prompt(qb, kb, vb, lp, lam_init, b)

        xp, kf, vf, s_new, last = _layer(xp, b, lp, wts, lam_init, mk.reshape(b, n_mem, D_MODEL),
                                         mv.reshape(b, n_mem, D_MODEL), rwkv_prompt, attn_prompt)
        outs['k_p'].append(kf.reshape(b, t, DA_HEADS, 2 * DA_QK))
        outs['v_p'].append(vf.reshape(b, t, DA_HEADS, DA_V))
        outs['S_p'].append(s_new)
        outs['sh_p'].append(last)
        outs['mk_p'].append(mk.reshape(b, n_mem, CROSS_HEADS, CROSS_HD))
        outs['mv_p'].append(mv.reshape(b, n_mem, CROSS_HEADS, CROSS_HD))

        assert dt == 1

        def rwkv_sample(pr):
            r, k2, v, kk, a, lw, g = _rwkv_prep(pr, state_shift[l], lp, rolled=False, tt=db)
            y, s_fin = _rwkv_step(state_wkv[l], r, k2, v, kk, a, lw)
            return _rwkv_post(y, r, k2, v, g, lp, db), s_fin

        def attn_sample(qb, kf, vf, kb, vb):
            return _diff_attn_paged(qb.astype(F32), kf, vf, cache_k[l], cache_v[l], page_table, lp, lam_init)

        mem_ks = cache_mem_k[l].astype(BF16).reshape(db, n_mem, D_MODEL)
        mem_vs = cache_mem_v[l].astype(BF16).reshape(db, n_mem, D_MODEL)
        xs, kf, vf, s_new, last = _layer(xs, db, lp, wts, lam_init, mem_ks, mem_vs, rwkv_sample, attn_sample)
        outs['k_s'].append(kf.reshape(db, dt, DA_HEADS, 2 * DA_QK))
        outs['v_s'].append(vf.reshape(db, dt, DA_HEADS, DA_V))
        outs['S_s'].append(s_new)
        outs['sh_s'].append(last)
    y_prompt = xp.reshape(b, t, D_MODEL)
    y_sample = xs.reshape(db, dt, D_MODEL)
    st = lambda n: jnp.stack(outs[n])
    return (y_prompt, y_sample, st('k_p'), st('v_p'), st('k_s'), st('v_s'), st('S_p'), st('S_s'),
            st('sh_p'), st('sh_s'), st('mk_p'), st('mv_p'))
```

```python
import functools
import math

import jax
import jax.numpy as jnp
from jax import lax
from jax.experimental import pallas as pl
from jax.experimental.pallas import tpu as pltpu

F32 = jnp.float32
BF16 = jnp.bfloat16

D_MODEL = 2048
RWKV_DIM = 1024
HEAD = 64
RWKV_HEADS = 16
DECAY_LORA = 64
AAA_LORA = 64
GATE_LORA = 128
RWKV_PROJ = 3 * RWKV_DIM + DECAY_LORA + AAA_LORA + GATE_LORA
DA_DIM = 1024
DA_QK = 64
DA_V = 128
DA_HEADS = 8
CROSS_HEADS = 4
CROSS_HD = D_MODEL // CROSS_HEADS
RMS_EPS = 1e-6
GN_EPS = 64e-5
NEG_INF = -1e30
LOG2_E = 1.4426950408889634

LANES = 128
CHUNK = 64
PAIRS = RWKV_HEADS // 2
VMEM_LIMIT = 56 * 1024 * 1024

_NT = (((1,), (1,)), ((), ()))


def _cparams(sem):
    return pltpu.CompilerParams(dimension_semantics=sem, vmem_limit_bytes=VMEM_LIMIT)


def _dot(a, b):
    return jnp.dot(a, b, preferred_element_type=F32)


def _dot_nt(a, b):
    return lax.dot_general(a, b, _NT, preferred_element_type=F32)


def _rms_rows(x, g):
    x = x.astype(F32)
    ms = jnp.mean(x * x, axis=-1, keepdims=True)
    return x * lax.rsqrt(ms + RMS_EPS) * g


def _sigmoid(x):
    return 1.0 / (1.0 + jnp.exp(-x))


def _split_dot(x, m_bf16, terms, left=False):
    acc = None
    rem = x
    for t in range(terms):
        piece = rem.astype(BF16)
        d = _dot(m_bf16, piece) if left else _dot(piece, m_bf16)
        acc = d if acc is None else acc + d
        if t + 1 < terms:
            rem = rem - piece.astype(F32)
    return acc


def _head_ones():
    i = lax.broadcasted_iota(jnp.int32, (LANES, LANES), 0) // HEAD
    j = lax.broadcasted_iota(jnp.int32, (LANES, LANES), 1) // HEAD
    return jnp.where(i == j, 1.0, 0.0).astype(BF16)


def _mm_kernel(*refs, n_a, has_norm, swiglu, has_res, scale):
    refs = list(refs)
    a_refs = [refs.pop(0) for _ in range(n_a)]
    g_ref = refs.pop(0) if has_norm else None
    w_refs = [refs.pop(0) for _ in range(2 if swiglu else n_a)]
    res_ref = refs.pop(0) if has_res else None
    o_ref = refs.pop(0)
    if has_norm:
        xn_ref = refs.pop(0)

        @pl.when(pl.program_id(1) == 0)
        def _():
            xn_ref[...] = _rms_rows(a_refs[0][...], g_ref[...]).astype(BF16)

        lhs = [xn_ref[...]]
    else:
        lhs = [r[...] for r in a_refs]
    if swiglu:
        gate = _dot(lhs[0], w_refs[0][...])
        up = _dot(lhs[0], w_refs[1][...])
        acc = gate * _sigmoid(gate) * up
    else:
        acc = _dot(lhs[0], w_refs[0][...])
        for a, w in zip(lhs[1:], w_refs[1:]):
            acc = acc + _dot(a, w[...])
    if scale != 1.0:
        acc = acc * scale
    if has_res:
        acc = acc + res_ref[...]
    o_ref[...] = acc.astype(o_ref.dtype)


def _mm(a_list, w_list, *, name, norm_g=None, swiglu=False, res=None, scale=1.0, out_dtype=F32, tm=512, tn=None):
    m = a_list[0].shape[0]
    n = w_list[0].shape[1]
    tm = min(tm, m)
    resident = tn is None
    tn = n if resident else min(tn, n)
    assert m % tm == 0 and n % tn == 0
    has_norm = norm_g is not None
    in_specs = [pl.BlockSpec((tm, a.shape[1]), lambda i, j: (i, 0)) for a in a_list]
    args = list(a_list)
    if has_norm:
        in_specs.append(pl.BlockSpec((1, a_list[0].shape[1]), lambda i, j: (0, 0)))
        args.append(norm_g.reshape(1, -1))
    for w in w_list:
        if resident:
            in_specs.append(pl.BlockSpec((w.shape[0], n), lambda i, j: (0, 0), pipeline_mode=pl.Buffered(1)))
        else:
            in_specs.append(pl.BlockSpec((w.shape[0], tn), lambda i, j: (0, j)))
        args.append(w)
    if res is not None:
        in_specs.append(pl.BlockSpec((tm, tn), lambda i, j: (i, j)))
        args.append(res)
    scratch = [pltpu.VMEM((tm, a_list[0].shape[1]), BF16)] if has_norm else []
    return pl.pallas_call(
        functools.partial(_mm_kernel, n_a=len(a_list), has_norm=has_norm, swiglu=swiglu,
                          has_res=res is not None, scale=scale),
        grid=(m // tm, n // tn),
        in_specs=in_specs,
        out_specs=pl.BlockSpec((tm, tn), lambda i, j: (i, j)),
        out_shape=jax.ShapeDtypeStruct((m, n), out_dtype),
        scratch_shapes=scratch,
        compiler_params=_cparams(("parallel", "arbitrary")),
        name=name,
    )(*args)


def _da_proj_kernel(x_ref, g_ref, w_ref, q_ref, kf_ref, vf_ref, kb_ref, vb_ref):
    xn = _rms_rows(x_ref[...], g_ref[...]).astype(BF16)
    q_ref[...] = (_dot(xn, w_ref[:, :DA_DIM]) * (DA_QK ** -0.5 * LOG2_E)).astype(BF16)
    k = _dot(xn, w_ref[:, DA_DIM:2 * DA_DIM])
    kf_ref[...] = k
    kb_ref[...] = k.astype(BF16)
    v = _dot(xn, w_ref[:, 2 * DA_DIM:])
    vf_ref[...] = v
    vb_ref[...] = v.astype(BF16)


def _da_proj(x, norm_g, w_da, tm=512):
    m, k = x.shape
    tm = min(tm, m)
    blk = pl.BlockSpec((tm, DA_DIM), lambda i: (i, 0))
    return pl.pallas_call(
        _da_proj_kernel,
        grid=(m // tm,),
        in_specs=[pl.BlockSpec((tm, k), lambda i: (i, 0)),
                  pl.BlockSpec((1, k), lambda i: (0, 0)),
                  pl.BlockSpec((k, 3 * DA_DIM), lambda i: (0, 0), pipeline_mode=pl.Buffered(1))],
        out_specs=[blk] * 5,
        out_shape=[jax.ShapeDtypeStruct((m, DA_DIM), dt) for dt in (BF16, F32, F32, BF16, BF16)],
        compiler_params=_cparams(("parallel",)),
        name="da_proj",
    )(x, norm_g.reshape(1, -1), w_da)


def _mm_down_kernel(a_ref, w_ref, res_ref, g_ref, o_ref, *, final_norm):
    kk = pl.program_id(1)

    @pl.when(kk == 0)
    def _():
        o_ref[...] = res_ref[...]

    o_ref[...] += _dot(a_ref[...], w_ref[...])

    if final_norm:
        @pl.when(kk == pl.num_programs(1) - 1)
        def _():
            o_ref[...] = _rms_rows(o_ref[...], g_ref[...])


def _mm_down(a, w, res, final_g, final_norm, tm=512, tk=512):
    m, k = a.shape
    n = w.shape[1]
    tm = min(tm, m)
    assert m % tm == 0 and k % tk == 0
    return pl.pallas_call(
        functools.partial(_mm_down_kernel, final_norm=final_norm),
        grid=(m // tm, k // tk),
        in_specs=[pl.BlockSpec((tm, tk), lambda i, kk: (i, kk)),
                  pl.BlockSpec((tk, n), lambda i, kk: (kk, 0)),
                  pl.BlockSpec((tm, n), lambda i, kk: (i, 0)),
                  pl.BlockSpec((1, n), lambda i, kk: (0, 0))],
        out_specs=pl.BlockSpec((tm, n), lambda i, kk: (i, 0)),
        out_shape=jax.ShapeDtypeStruct((m, n), F32),
        compiler_params=_cparams(("parallel", "arbitrary")),
        name="ffn_down",
    )(a, w, res, final_g.reshape(1, -1))


def _head_sum(x):
    lo = lax.broadcasted_iota(jnp.int32, (1, LANES), 1) < HEAD
    s_lo = jnp.sum(jnp.where(lo, x, 0.0), axis=-1, keepdims=True)
    s_all = jnp.sum(x, axis=-1, keepdims=True)
    return jnp.where(lo, s_lo, s_all - s_lo)


def _token_mix(p, p_prev, mu, w0, w2, a0, a2, g2, k_k, k_a):
    ps = p + (p_prev - p) * mu
    o1, o2, o3 = RWKV_DIM, 2 * RWKV_DIM, 3 * RWKV_DIM
    o4 = o3 + DECAY_LORA
    o5 = o4 + AAA_LORA
    r, k, v = ps[:, :o1], ps[:, o1:o2], ps[:, o2:o3]
    wd, ad, gd = ps[:, o3:o4], ps[:, o4:o5], ps[:, o5:]
    z = w0 + _dot(jnp.tanh(wd).astype(BF16), w2)
    w_log = -(jnp.maximum(-z, 0.0) + jnp.log(1.0 + jnp.exp(-jnp.abs(z)))) - 0.5
    lw = -jnp.exp(w_log)
    a = _sigmoid(a0 + _dot(ad.astype(BF16), a2))
    g = _dot(_sigmoid(gd).astype(BF16), g2)
    kraw = k * k_k
    kk = []
    for pr in range(PAIRS):
        kp = kraw[:, pr * LANES:(pr + 1) * LANES]
        kk.append(kp * lax.rsqrt(jnp.maximum(_head_sum(kp * kp), 1e-24)))
    return r, k * (1.0 + (a - 1.0) * k_a), v, kk, a, lw, g


def _mix_out(y, r, k2, v, g, r_k, ln_g, ln_b):
    mean = _head_sum(y) * (1.0 / HEAD)
    d = y - mean
    var = _head_sum(d * d) * (1.0 / HEAD)
    yn = d * lax.rsqrt(var + GN_EPS) * ln_g + ln_b
    return (yn + _head_sum(r * k2 * r_k) * v) * g


def _prep_kernel(p_ref, prev_ref, mu_ref, w0_ref, w2_ref, a0_ref, a2_ref, g2_ref, kk_ref, ka_ref,
                 r_o, k_o, v_o, kk_o, a_o, lw_o, g_o):
    r, k2, v, kk, a, lw, g = _token_mix(p_ref[...], prev_ref[...], mu_ref[...], w0_ref[...], w2_ref[...],
                                        a0_ref[...], a2_ref[...], g2_ref[...], kk_ref[...], ka_ref[...])
    for pr in range(PAIRS):
        kk_o[:, pr * LANES:(pr + 1) * LANES] = kk[pr]
    r_o[...] = r
    k_o[...] = k2
    v_o[...] = v
    a_o[...] = a
    lw_o[...] = lw
    g_o[...] = g


def _mix_params(lp):
    row = lambda x: x.reshape(1, -1)
    return (row(lp['mu']), row(lp['w0']), lp['w2'].astype(BF16), row(lp['a0']), lp['a2'].astype(BF16),
            lp['g2'].astype(BF16), row(lp['k_k']), row(lp['k_a']))


def _mix_param_specs(index_map):
    shapes = [(1, RWKV_PROJ), (1, RWKV_DIM), (DECAY_LORA, RWKV_DIM), (1, RWKV_DIM), (AAA_LORA, RWKV_DIM),
              (GATE_LORA, RWKV_DIM), (1, RWKV_DIM), (1, RWKV_DIM)]
    return [pl.BlockSpec(s, index_map) for s in shapes]


def _rwkv_prep(p, prev, lp):
    m = p.shape[0]
    blk_in = pl.BlockSpec((m, RWKV_PROJ), lambda i: (0, 0))
    out_blk = pl.BlockSpec((m, RWKV_DIM), lambda i: (0, 0))
    return pl.pallas_call(
        _prep_kernel,
        grid=(1,),
        in_specs=[blk_in, blk_in] + _mix_param_specs(lambda i: (0, 0)),
        out_specs=[out_blk] * 7,
        out_shape=[jax.ShapeDtypeStruct((m, RWKV_DIM), F32)] * 7,
        compiler_params=_cparams(("arbitrary",)),
        name="rwkv_prep",
    )(p, prev, *_mix_params(lp))


def _chunk_kernel(p_ref, mu_ref, w0_ref, w2_ref, a0_ref, a2_ref, g2_ref, kk_ref, ka_ref, rk_ref, lg_ref, lb_ref,
                  y_ref, s_out_ref, s_ref, prev_ref):
    c = pl.program_id(1)

    @pl.when(c == 0)
    def _():
        s_ref[...] = jnp.zeros_like(s_ref)
        prev_ref[...] = jnp.zeros_like(prev_ref)

    p = p_ref[...]
    first_row = lax.broadcasted_iota(jnp.int32, p.shape, 0) == 0
    p_prev = jnp.where(first_row, prev_ref[...], pltpu.roll(p, 1, 0))
    prev_ref[...] = p[CHUNK - 1:CHUNK, :]
    r_all, k_all, v_all, kk, a_all, lw_all, g_all = _token_mix(
        p, p_prev, mu_ref[...], w0_ref[...], w2_ref[...], a0_ref[...], a2_ref[...], g2_ref[...], kk_ref[...],
        ka_ref[...])

    lane = lax.broadcasted_iota(jnp.int32, (1, LANES), 1)
    m0 = jnp.where(lane < HEAD, 1.0, 0.0)
    m1 = 1.0 - m0
    ii = lax.broadcasted_iota(jnp.int32, (LANES, LANES), 0)
    jj = lax.broadcasted_iota(jnp.int32, (LANES, LANES), 1)
    strict = ii > jj
    incl = ii >= jj
    eye = jnp.where(ii == jj, 1.0, 0.0)
    ti = lax.broadcasted_iota(jnp.int32, (CHUNK, CHUNK), 0)
    tj = lax.broadcasted_iota(jnp.int32, (CHUNK, CHUNK), 1)
    tril = jnp.where(ti >= tj, 1.0, 0.0).astype(BF16)

    def stack(x):
        return jnp.concatenate([x * m0, x * m1], axis=0)

    pairs = range(PAIRS)
    sls = [slice(pr * LANES, (pr + 1) * LANES) for pr in pairs]
    each = lambda fn, *cols: [fn(*xs) for xs in zip(*cols)]
    lw = [lw_all[:, sl] for sl in sls]
    bb = [x * a_all[:, sl] for x, sl in zip(kk, sls)]
    k2 = [k_all[:, sl] for sl in sls]
    r = [r_all[:, sl] for sl in sls]
    v = [v_all[:, sl] for sl in sls]
    cum = [_split_dot(x, tril, 3, left=True) for x in lw]
    tot = [x[CHUNK - 1:CHUNK, :] for x in cum]
    inv_g = [jnp.exp(-x) for x in cum]
    to_end = each(lambda t, x: jnp.exp(t - x), tot, cum)
    a_t = each(lambda x, cm, l: stack(-x * jnp.exp(cm - l)).astype(BF16), kk, cum, lw)
    b_t = each(lambda x, g: stack(x * g).astype(BF16), bb, inv_g)
    k_t = each(lambda x, g: stack(x * g).astype(BF16), k2, inv_g)
    r_f = each(lambda x, cm: stack(x * jnp.exp(cm)), r, cum)
    r_t = [x.astype(BF16) for x in r_f]
    v_f = [stack(x) for x in v]
    v_s = [x.astype(BF16) for x in v_f]
    bk_end = each(lambda x, y, e: jnp.concatenate([stack(x * e), stack(y * e)], axis=0).astype(BF16), bb, k2, to_end)

    a_ab = each(lambda x, y: jnp.where(strict, _dot_nt(x, y), 0.0), a_t, b_t)
    a_ak = each(lambda x, y: jnp.where(strict, _dot_nt(x, y), 0.0).astype(BF16), a_t, k_t)
    a_rb = each(lambda x, y: jnp.where(incl, _dot_nt(x, y), 0.0).astype(BF16), r_t, b_t)
    a_rk = each(lambda x, y: jnp.where(incl, _dot_nt(x, y), 0.0).astype(BF16), r_t, k_t)
    inv = [eye + x for x in a_ab]
    power = [x.astype(BF16) for x in a_ab]
    for _ in range(5):
        power = [_dot(x, x).astype(BF16) for x in power]
        inv = each(lambda x, p: x + _dot(x.astype(BF16), p), inv, power)
    w1 = each(lambda x, y: _dot(x, y).astype(BF16), a_ak, v_s)
    pu = each(lambda x, y, z: _dot(x.astype(BF16), jnp.concatenate([y, z], axis=1)), inv, a_t, w1)
    pu_b = [x.astype(BF16) for x in pu]
    qy = each(_dot, a_rb, pu_b)
    q_m = each(lambda x, y: (x + y[:, :LANES]).astype(BF16), r_f, qy)
    y_v = each(lambda x, y, z: x[:, LANES:] + _dot(y, z), qy, a_rk, v_s)
    s_old = [s_ref[pr] for pr in pairs]
    uy = each(lambda x, y, s: _dot_nt(jnp.concatenate([x[:, :LANES], y], axis=0), s.astype(BF16)), pu_b, q_m, s_old)
    uv_t = each(lambda x, y, z: jnp.concatenate([x[:LANES] + y[:, LANES:], z], axis=0).T.astype(BF16), uy, pu, v_f)
    s_new = each(lambda s, t, x, y: s * jnp.exp(t) + _dot(x, y), s_old, tot, uv_t, bk_end)
    for pr in pairs:
        s_ref[pr] = s_new[pr]
        y_m = uy[pr][LANES:] + y_v[pr]
        sl = sls[pr]
        out = _mix_out(y_m[:CHUNK] + y_m[CHUNK:], r[pr], k2[pr], v[pr], g_all[:, sl], rk_ref[:, sl], lg_ref[:, sl],
                       lb_ref[:, sl])
        y_ref[:, sl] = out.astype(y_ref.dtype)

    @pl.when(c == pl.num_programs(1) - 1)
    def _():
        s_out_ref[0] = s_ref[...]


def _rwkv_chunk_scan(p, lp, batch):
    m = p.shape[0]
    nc = m // batch // CHUNK
    row = lambda x: x.reshape(1, -1)
    const = lambda b, c: (0, 0)
    vec = pl.BlockSpec((1, RWKV_DIM), const)
    y, s_pairs = pl.pallas_call(
        _chunk_kernel,
        grid=(batch, nc),
        in_specs=([pl.BlockSpec((CHUNK, RWKV_PROJ), lambda b, c: (b * nc + c, 0))] + _mix_param_specs(const)
                  + [vec] * 3),
        out_specs=[pl.BlockSpec((CHUNK, RWKV_DIM), lambda b, c: (b * nc + c, 0)),
                   pl.BlockSpec((1, PAIRS, LANES, LANES), lambda b, c: (b, 0, 0, 0))],
        out_shape=[jax.ShapeDtypeStruct((m, RWKV_DIM), BF16),
                   jax.ShapeDtypeStruct((batch, PAIRS, LANES, LANES), F32)],
        scratch_shapes=[pltpu.VMEM((PAIRS, LANES, LANES), F32), pltpu.VMEM((1, RWKV_PROJ), F32)],
        compiler_params=_cparams(("parallel", "arbitrary")),
        name="rwkv_chunk",
    )(p, *_mix_params(lp), row(lp['r_k']), row(lp['lnx_g']), row(lp['lnx_b']))
    s_fin = jnp.stack([s_pairs[:, :, :HEAD, :HEAD], s_pairs[:, :, HEAD:, HEAD:]], axis=2)
    return y, s_fin.reshape(batch, RWKV_HEADS, HEAD, HEAD)


def _step_kernel(s_ref, r_ref, k_ref, kk_ref, a_ref, lw_ref, v_ref, s_o, y_o):
    s = s_ref[0]
    kk = kk_ref[0]
    sa = jnp.sum(s * (-kk), axis=-1, keepdims=True)
    s_new = s * jnp.exp(lw_ref[0]) + sa * (kk * a_ref[0]) + v_ref[0] * k_ref[0]
    s_o[0] = s_new
    y_o[0] = jnp.sum(s_new * r_ref[0], axis=-1, keepdims=True)


def _rwkv_step(s0, r, k2, v, kk, a, lw):
    b = s0.shape[0]
    rowv = lambda x: x.reshape(b, RWKV_HEADS, 1, HEAD)
    row_spec = pl.BlockSpec((1, RWKV_HEADS, 1, HEAD), lambda i: (i, 0, 0, 0))
    col_spec = pl.BlockSpec((1, RWKV_HEADS, HEAD, 1), lambda i: (i, 0, 0, 0))
    s_spec = pl.BlockSpec((1, RWKV_HEADS, HEAD, HEAD), lambda i: (i, 0, 0, 0))
    s_new, y = pl.pallas_call(
        _step_kernel,
        grid=(b,),
        in_specs=[s_spec] + [row_spec] * 5 + [col_spec],
        out_specs=[s_spec, col_spec],
        out_shape=[jax.ShapeDtypeStruct(s0.shape, F32), jax.ShapeDtypeStruct((b, RWKV_HEADS, HEAD, 1), F32)],
        compiler_params=_cparams(("parallel",)),
        name="rwkv_step",
    )(s0, rowv(r), rowv(k2), rowv(kk), rowv(a), rowv(lw), v.reshape(b, RWKV_HEADS, HEAD, 1))
    return y.reshape(b, RWKV_DIM), s_new


def _post_kernel(y_ref, r_ref, k_ref, v_ref, g_ref, rk_ref, lg_ref, lb_ref, o_ref):
    for pr in range(PAIRS):
        sl = slice(pr * LANES, (pr + 1) * LANES)
        out = _mix_out(y_ref[:, sl], r_ref[:, sl], k_ref[:, sl], v_ref[:, sl], g_ref[:, sl], rk_ref[:, sl],
                       lg_ref[:, sl], lb_ref[:, sl])
        o_ref[:, sl] = out.astype(o_ref.dtype)


def _rwkv_post(y, r, k2, v, g, lp):
    m = y.shape[0]
    blk = pl.BlockSpec((m, RWKV_DIM), lambda i: (0, 0))
    par = pl.BlockSpec((1, RWKV_DIM), lambda i: (0, 0))
    return pl.pallas_call(
        _post_kernel,
        grid=(1,),
        in_specs=[blk] * 5 + [par] * 3,
        out_specs=blk,
        out_shape=jax.ShapeDtypeStruct((m, RWKV_DIM), BF16),
        compiler_params=_cparams(("arbitrary",)),
        name="rwkv_post",
    )(y, r, k2, v, g, lp['r_k'].reshape(1, -1), lp['lnx_g'].reshape(1, -1), lp['lnx_b'].reshape(1, -1))


def _lambda(lq1_ref, lk1_ref, lq2_ref, lk2_ref, lam_init):
    s1 = jnp.sum(lq1_ref[...] * lk1_ref[...], axis=-1, keepdims=True)
    s2 = jnp.sum(lq2_ref[...] * lk2_ref[...], axis=-1, keepdims=True)
    return jnp.exp(s1) - jnp.exp(s2) + lam_init


def _dattn_kernel(lq1_ref, lk1_ref, lq2_ref, lk2_ref, sg_ref, q_ref, k_ref, v_ref, o_ref,
                  m_ref, acc_ref, *, tq, lam_init):
    nq = q_ref.shape[0] // tq
    lane = lax.broadcasted_iota(jnp.int32, (1, LANES), 1)
    row = lax.broadcasted_iota(jnp.int32, (tq, tq), 0)
    col = lax.broadcasted_iota(jnp.int32, (tq, tq), 1)
    ones = jnp.ones((tq, LANES), BF16)
    lam = _lambda(lq1_ref, lk1_ref, lq2_ref, lk2_ref, lam_init)
    rows = lambda ref, i: ref[i * tq:(i + 1) * tq, :]

    def scores(qi, j):
        q = rows(q_ref, qi)
        zero = jnp.zeros_like(q)
        ks = rows(k_ref, j)
        return [_dot_nt(jnp.where(lane < DA_QK, q, zero), ks), _dot_nt(jnp.where(lane >= DA_QK, q, zero), ks)]

    blocks = [(qi, j) for qi in range(nq) for j in range(qi + 1)]
    s_next = scores(*blocks[0])
    for n, (qi, j) in enumerate(blocks):
        s = s_next
        if n + 1 < len(blocks):
            s_next = scores(*blocks[n + 1])
        first, last = j == 0, j == qi
        if last:
            s = [jnp.where(col <= row, x, NEG_INF) for x in s]
        v_aug = jnp.concatenate([rows(v_ref, j), ones], axis=1)
        row_max = [jnp.max(x, axis=-1, keepdims=True) for x in s]
        m_old = None if first else [m_ref[c] for c in range(2)]
        m_new = row_max if first else [jnp.maximum(mo, mx) for mo, mx in zip(m_old, row_max)]
        p = [jnp.exp2(x - mn).astype(BF16) for x, mn in zip(s, m_new)]
        acc = [_dot(x, v_aug) for x in p]
        if not first:
            acc = [jnp.exp2(m_old[c] - m_new[c]) * acc_ref[c] + acc[c] for c in range(2)]
        if last:
            a0, a1 = acc
            o = a0[:, :LANES] / a0[:, LANES:] - lam * (a1[:, :LANES] / a1[:, LANES:])
            o_ref[qi * tq:(qi + 1) * tq, :] = (_rms_rows(o, sg_ref[...]) * (1.0 - lam_init)).astype(o_ref.dtype)
        else:
            for c in range(2):
                acc_ref[c] = acc[c]
                m_ref[c] = m_new[c]


def _diff_attn_prompt(qb, kb, vb, lp, lam_init, batch, tq=512):
    m = qb.shape[0]
    t = m // batch
    tq = min(tq, t)
    vec = pl.BlockSpec((1, DA_QK), lambda b, h: (0, 0))
    seq_spec = pl.BlockSpec((t, LANES), lambda b, h: (b, h))
    return pl.pallas_call(
        functools.partial(_dattn_kernel, tq=tq, lam_init=lam_init),
        grid=(batch, DA_HEADS),
        in_specs=[vec] * 4 + [pl.BlockSpec((1, DA_V), lambda b, h: (0, 0)), seq_spec, seq_spec, seq_spec],
        out_specs=seq_spec,
        out_shape=jax.ShapeDtypeStruct((m, DA_DIM), BF16),
        scratch_shapes=[pltpu.VMEM((2, tq, 1), F32), pltpu.VMEM((2, tq, 2 * LANES), F32)],
        compiler_params=_cparams(("parallel", "parallel")),
        name="diff_attn_prompt",
    )(*[lp[n].reshape(1, -1) for n in ('lq1', 'lk1', 'lq2', 'lk2')], lp['subln_g'].reshape(1, -1), qb, kb, vb)


def _paged_kernel(pt_ref, lq1_ref, lk1_ref, lq2_ref, lk2_ref, sg_ref, q_ref, kn_ref, vn_ref, *rest, pps, lam_init):
    del pt_ref
    k_refs, v_refs = rest[:pps], rest[pps:2 * pps]
    o_ref, m_ref, l_ref, acc_ref = rest[2 * pps:]
    pg = pl.program_id(1)

    @pl.when(pg == 0)
    def _():
        m_ref[...] = jnp.full_like(m_ref, NEG_INF)
        l_ref[...] = jnp.zeros_like(l_ref)
        acc_ref[...] = jnp.zeros_like(acc_ref)

    q = q_ref[0]
    lane_in = lax.broadcasted_iota(jnp.int32, (LANES, LANES), 0)
    pick = [jnp.where(lane_in == c * DA_QK, 1.0, 0.0).astype(BF16) for c in range(2)]
    half_sum = _head_ones()

    def map_lane(x, c):
        return x[:, c * DA_QK:c * DA_QK + 1]

    def update(k_list, v_list):
        n = k_list[0].shape[0]
        flat = lambda x: x.reshape(n * DA_HEADS, LANES).astype(BF16)
        tile = lambda x: x.reshape(n, DA_HEADS, LANES)
        s = [tile(_dot(flat(k3 * q), half_sum)) for k3 in k_list]
        m_old = m_ref[...]
        m_new = m_old
        for x in s:
            m_new = jnp.maximum(m_new, jnp.max(x, axis=0))
        alpha = jnp.exp2(m_old - m_new)
        p = [jnp.exp2(x - m_new) for x in s]
        l_ref[...] = alpha * l_ref[...] + sum(jnp.sum(x, axis=0) for x in p)
        m_ref[...] = m_new
        p_b = [flat(x) for x in p]
        for c in range(2):
            p_c = [tile(_dot(x, pick[c])) for x in p_b]
            acc_ref[c] = map_lane(alpha, c) * acc_ref[c] + sum(jnp.sum(x * v3, axis=0) for x, v3 in zip(p_c, v_list))

    update([r[0] for r in k_refs], [r[0] for r in v_refs])

    @pl.when(pg == pl.num_programs(1) - 1)
    def _():
        update([kn_ref[...]], [vn_ref[...]])
        lam = _lambda(lq1_ref, lk1_ref, lq2_ref, lk2_ref, lam_init)
        l = l_ref[...]
        o = acc_ref[0] / map_lane(l, 0) - lam * (acc_ref[1] / map_lane(l, 1))
        o_ref[0] = (_rms_rows(o, sg_ref[...]) * (1.0 - lam_init)).astype(o_ref.dtype)


def _diff_attn_paged(q, k_new, v_new, cache_k, cache_v, page_table, lp, lam_init):
    b, n_pages = page_table.shape
    page = cache_k.shape[1]
    pps = next(c for c in (8, 4, 2, 1) if n_pages % c == 0)
    vec = pl.BlockSpec((1, DA_QK), lambda i, p, pt: (0, 0))
    rowb = pl.BlockSpec((1, DA_HEADS, LANES), lambda i, p, pt: (i, 0, 0))

    def page_spec(j):
        return pl.BlockSpec((1, page, DA_HEADS, LANES),
                            lambda i, p, pt: (pt[i * n_pages + p * pps + j], 0, 0, 0))

    grid_spec = pltpu.PrefetchScalarGridSpec(
        num_scalar_prefetch=1,
        grid=(b, n_pages // pps),
        in_specs=([vec] * 4 + [pl.BlockSpec((1, DA_V), lambda i, p, pt: (0, 0)), rowb, rowb, rowb]
                  + [page_spec(j) for j in range(pps)] * 2),
        out_specs=rowb,
        scratch_shapes=[pltpu.VMEM((DA_HEADS, LANES), F32)] * 2 + [pltpu.VMEM((2, DA_HEADS, LANES), F32)],
    )
    r3 = lambda x: x.reshape(b, DA_HEADS, LANES)
    out = pl.pallas_call(
        functools.partial(_paged_kernel, pps=pps, lam_init=lam_init),
        grid_spec=grid_spec,
        out_shape=jax.ShapeDtypeStruct((b, DA_HEADS, LANES), BF16),
        compiler_params=_cparams(("parallel", "arbitrary")),
        name="diff_attn_paged",
    )(page_table.reshape(-1), *[lp[n].reshape(1, -1) for n in ('lq1', 'lk1', 'lq2', 'lk2')],
      lp['subln_g'].reshape(1, -1), r3(q), r3(k_new), r3(v_new), *([cache_k] * pps), *([cache_v] * pps))
    return out.reshape(b, DA_DIM)


def _cross_kernel(q_ref, mk_ref, mv_ref, o_ref):
    for h in range(CROSS_HEADS):
        sl = slice(h * CROSS_HD, (h + 1) * CROSS_HD)
        s = _dot_nt(q_ref[0, :, sl], mk_ref[0, :, sl].astype(BF16))
        m = jnp.max(s, axis=-1, keepdims=True)
        p = jnp.exp(s - m)
        l = jnp.sum(p, axis=-1, keepdims=True)
        o_ref[0, :, sl] = (_dot(p.astype(BF16), mv_ref[0, :, sl].astype(BF16)) / l).astype(o_ref.dtype)


def _cross_attn(q, mk, mv, tq=512):
    b, t, _ = q.shape
    mt = mk.shape[1]
    tq = min(tq, t)
    q_spec = pl.BlockSpec((1, tq, D_MODEL), lambda i, j: (i, j, 0))
    m_spec = pl.BlockSpec((1, mt, D_MODEL), lambda i, j: (i, 0, 0))
    return pl.pallas_call(
        _cross_kernel,
        grid=(b, t // tq),
        in_specs=[q_spec, m_spec, m_spec],
        out_specs=q_spec,
        out_shape=jax.ShapeDtypeStruct(q.shape, BF16),
        compiler_params=_cparams(("parallel", "parallel")),
        name="cross_attn",
    )(q, mk, mv)


def _layer(x, batch, lp, wts, lam_init, mk, mv, rwkv_fn, attn_fn):
    m = x.shape[0]
    t = m // batch
    pr = _mm([x], [wts['in_r']], name="in_proj_rwkv", norm_g=lp['norm_mix_g'], tm=512)
    qb, kf, vf, kb, vb = _da_proj(x, lp['norm_mix_g'], wts['in_da'])
    y_r, s_new = rwkv_fn(pr)
    y_d = attn_fn(qb, kf, vf, kb, vb)
    x = _mm([y_r, y_d], [wts['o_r'], wts['o_d']], name="out_proj", res=x, tm=512)
    qc = _mm([x], [wts['cq']], name="cross_q", norm_g=lp['norm_cross_g'], scale=CROSS_HD ** -0.5, out_dtype=BF16,
             tm=1024)
    tq = t
    qc3 = qc.reshape(batch, t, D_MODEL)
    if t < 8:
        qc3 = jnp.pad(qc3, ((0, 0), (0, 8 - t), (0, 0)))
    oc = _cross_attn(qc3, mk, mv)[:, :tq].reshape(m, D_MODEL)
    x = _mm([oc], [wts['co']], name="cross_out", res=x, tm=512)
    hmid = _mm([x], [wts['gate'], wts['up']], name="ffn_swiglu", norm_g=lp['norm_ffn_g'], swiglu=True,
               out_dtype=BF16, tm=1024, tn=512)
    x = _mm_down(hmid, wts['down'], x, wts['final_g'], wts['is_last'], tm=1024, tk=hmid.shape[1] // 4)
    last_row = pr.reshape(batch, t, RWKV_PROJ)[:, -1]
    return x, kf, vf, s_new, last_row


def kernel(x_prompt, x_sample, cache_k, cache_v, cache_mem_k, cache_mem_v, state_wkv, state_shift, page_table, mem_prompt, norm_mix_g, w_in, tok_shift_mu, rwkv_w0, rwkv_w2, rwkv_a0, rwkv_a2, rwkv_g2, rwkv_k_k, rwkv_k_a, rwkv_r_k, rwkv_lnx_g, rwkv_lnx_b, diff_lq1, diff_lk1, diff_lq2, diff_lk2, diff_subln_g, w_o, norm_cross_g, norm_mem_g, w_cq, w_ck, w_cv, w_co, norm_ffn_g, w_gate, w_up, w_down, final_norm_g):
    b, t, _ = x_prompt.shape
    db, dt, _ = x_sample.shape
    depth = w_in.shape[0]
    n_mem = mem_prompt.shape[1]
    n_pool, page = cache_k.shape[1], cache_k.shape[2]
    xp = x_prompt.reshape(b * t, D_MODEL)
    xs = x_sample.reshape(db * dt, D_MODEL)
    outs = {n: [] for n in ('k_p', 'v_p', 'k_s', 'v_s', 'S_p', 'S_s', 'sh_p', 'sh_s', 'mk_p', 'mv_p')}
    for l in range(depth):
        lam_init = 0.8 - 0.6 * math.exp(-0.3 * l)
        lp = dict(norm_mix_g=norm_mix_g[l], mu=tok_shift_mu[l], w0=rwkv_w0[l], w2=rwkv_w2[l], a0=rwkv_a0[l],
                  a2=rwkv_a2[l], g2=rwkv_g2[l], k_k=rwkv_k_k[l], k_a=rwkv_k_a[l], r_k=rwkv_r_k[l],
                  lnx_g=rwkv_lnx_g[l], lnx_b=rwkv_lnx_b[l], lq1=diff_lq1[l], lk1=diff_lk1[l], lq2=diff_lq2[l],
                  lk2=diff_lk2[l], subln_g=diff_subln_g[l], norm_cross_g=norm_cross_g[l], norm_ffn_g=norm_ffn_g[l])
        wl = w_in[l].astype(BF16)
        wts = dict(in_r=wl[:, :RWKV_PROJ], in_da=wl[:, RWKV_PROJ:], o_r=w_o[l][:RWKV_DIM].astype(BF16),
                   o_d=w_o[l][RWKV_DIM:].astype(BF16), cq=w_cq[l].astype(BF16), co=w_co[l].astype(BF16),
                   gate=w_gate[l].astype(BF16), up=w_up[l].astype(BF16), down=w_down[l].astype(BF16),
                   final_g=final_norm_g, is_last=l == depth - 1)

        mem2 = mem_prompt.reshape(b * n_mem, D_MODEL)
        mk = _mm([mem2], [w_ck[l].astype(BF16)], name="mem_k", norm_g=norm_mem_g[l], tm=1024)
        mv = _mm([mem2], [w_cv[l].astype(BF16)], name="mem_v", norm_g=norm_mem_g[l], tm=1024)

        def rwkv_prompt(pr):
            return _rwkv_chunk_scan(pr, lp, b)

        def attn_prompt(qb, kf, vf, kb, vb):
            return _diff_attn_prompt(qb, kb, vb, lp, lam_init, b)

        xp, kf, vf, s_new, last = _layer(xp, b, lp, wts, lam_init, mk.reshape(b, n_mem, D_MODEL),
                                         mv.reshape(b, n_mem, D_MODEL), rwkv_prompt, attn_prompt)
        outs['k_p'].append(kf.reshape(b, t, DA_HEADS, 2 * DA_QK))
        outs['v_p'].append(vf.reshape(b, t, DA_HEADS, DA_V))
        outs['S_p'].append(s_new)
        outs['sh_p'].append(last)
        outs['mk_p'].append(mk.reshape(b, n_mem, CROSS_HEADS, CROSS_HD))
        outs['mv_p'].append(mv.reshape(b, n_mem, CROSS_HEADS, CROSS_HD))

        assert dt == 1

        def rwkv_sample(pr):
            r, k2, v, kk, a, lw, g = _rwkv_prep(pr, state_shift[l], lp)
            y, s_fin = _rwkv_step(state_wkv[l], r, k2, v, kk, a, lw)
            return _rwkv_post(y, r, k2, v, g, lp), s_fin

        def attn_sample(qb, kf, vf, kb, vb):
            return _diff_attn_paged(qb.astype(F32), kf, vf, cache_k[l], cache_v[l], page_table, lp, lam_init)

        mem_ks = cache_mem_k[l].astype(BF16).reshape(db, n_mem, D_MODEL)
        mem_vs = cache_mem_v[l].astype(BF16).reshape(db, n_mem, D_MODEL)
        xs, kf, vf, s_new, last = _layer(xs, db, lp, wts, lam_init, mem_ks, mem_vs, rwkv_sample, attn_sample)
        outs['k_s'].append(kf.reshape(db, dt, DA_HEADS, 2 * DA_QK))
        outs['v_s'].append(vf.reshape(db, dt, DA_HEADS, DA_V))
        outs['S_s'].append(s_new)
        outs['sh_s'].append(last)
    y_prompt = xp.reshape(b, t, D_MODEL)
    y_sample = xs.reshape(db, dt, D_MODEL)
    st = lambda n: jnp.stack(outs[n])
    return (y_prompt, y_sample, st('k_p'), st('v_p'), st('k_s'), st('v_s'), st('S_p'), st('S_s'),
            st('sh_p'), st('sh_s'), st('mk_p'), st('mv_p'))
```

```python
import functools
import math

import jax
import jax.numpy as jnp
from jax import lax
from jax.experimental import pallas as pl
from jax.experimental.pallas import tpu as pltpu

F32 = jnp.float32
BF16 = jnp.bfloat16

D_MODEL = 2048
RWKV_DIM = 1024
HEAD = 64
RWKV_HEADS = 16
DECAY_LORA = 64
AAA_LORA = 64
GATE_LORA = 128
RWKV_PROJ = 3 * RWKV_DIM + DECAY_LORA + AAA_LORA + GATE_LORA
DA_DIM = 1024
DA_QK = 64
DA_V = 128
DA_HEADS = 8
CROSS_HEADS = 4
CROSS_HD = D_MODEL // CROSS_HEADS
RMS_EPS = 1e-6
GN_EPS = 64e-5
NEG_INF = -1e30
LOG2_E = 1.4426950408889634

LANES = 128
CHUNK = 64
PAIRS = RWKV_HEADS // 2
VMEM_LIMIT = 56 * 1024 * 1024

_NT = (((1,), (1,)), ((), ()))


def _cparams(sem):
    return pltpu.CompilerParams(dimension_semantics=sem, vmem_limit_bytes=VMEM_LIMIT)


def _dot(a, b):
    return jnp.dot(a, b, preferred_element_type=F32)


def _dot_nt(a, b):
    return lax.dot_general(a, b, _NT, preferred_element_type=F32)


def _rms_rows(x, g):
    x = x.astype(F32)
    ms = jnp.mean(x * x, axis=-1, keepdims=True)
    return x * lax.rsqrt(ms + RMS_EPS) * g


def _sigmoid(x):
    return 1.0 / (1.0 + jnp.exp(-x))


def _split_dot(x, m_bf16, terms, left=False):
    acc = None
    rem = x
    for t in range(terms):
        piece = rem.astype(BF16)
        d = _dot(m_bf16, piece) if left else _dot(piece, m_bf16)
        acc = d if acc is None else acc + d
        if t + 1 < terms:
            rem = rem - piece.astype(F32)
    return acc


def _head_ones():
    i = lax.broadcasted_iota(jnp.int32, (LANES, LANES), 0) // HEAD
    j = lax.broadcasted_iota(jnp.int32, (LANES, LANES), 1) // HEAD
    return jnp.where(i == j, 1.0, 0.0).astype(BF16)


def _mm_kernel(*refs, n_a, has_norm, swiglu, has_res, scale):
    refs = list(refs)
    a_refs = [refs.pop(0) for _ in range(n_a)]
    g_ref = refs.pop(0) if has_norm else None
    w_refs = [refs.pop(0) for _ in range(2 if swiglu else n_a)]
    res_ref = refs.pop(0) if has_res else None
    o_ref = refs.pop(0)
    if has_norm:
        xn_ref = refs.pop(0)

        @pl.when(pl.program_id(1) == 0)
        def _():
            xn_ref[...] = _rms_rows(a_refs[0][...], g_ref[...]).astype(BF16)

        lhs = [xn_ref[...]]
    else:
        lhs = [r[...] for r in a_refs]
    if swiglu:
        gate = _dot(lhs[0], w_refs[0][...])
        up = _dot(lhs[0], w_refs[1][...])
        acc = gate * _sigmoid(gate) * up
    else:
        acc = _dot(lhs[0], w_refs[0][...])
        for a, w in zip(lhs[1:], w_refs[1:]):
            acc = acc + _dot(a, w[...])
    if scale != 1.0:
        acc = acc * scale
    if has_res:
        acc = acc + res_ref[...]
    o_ref[...] = acc.astype(o_ref.dtype)


def _mm(a_list, w_list, *, name, norm_g=None, swiglu=False, res=None, scale=1.0, out_dtype=F32, tm=512, tn=None):
    m = a_list[0].shape[0]
    n = w_list[0].shape[1]
    tm = min(tm, m)
    resident = tn is None
    tn = n if resident else min(tn, n)
    assert m % tm == 0 and n % tn == 0
    has_norm = norm_g is not None
    in_specs = [pl.BlockSpec((tm, a.shape[1]), lambda i, j: (i, 0)) for a in a_list]
    args = list(a_list)
    if has_norm:
        in_specs.append(pl.BlockSpec((1, a_list[0].shape[1]), lambda i, j: (0, 0)))
        args.append(norm_g.reshape(1, -1))
    for w in w_list:
        if resident:
            in_specs.append(pl.BlockSpec((w.shape[0], n), lambda i, j: (0, 0), pipeline_mode=pl.Buffered(1)))
        else:
            in_specs.append(pl.BlockSpec((w.shape[0], tn), lambda i, j: (0, j)))
        args.append(w)
    if res is not None:
        in_specs.append(pl.BlockSpec((tm, tn), lambda i, j: (i, j)))
        args.append(res)
    scratch = [pltpu.VMEM((tm, a_list[0].shape[1]), BF16)] if has_norm else []
    return pl.pallas_call(
        functools.partial(_mm_kernel, n_a=len(a_list), has_norm=has_norm, swiglu=swiglu,
                          has_res=res is not None, scale=scale),
        grid=(m // tm, n // tn),
        in_specs=in_specs,
        out_specs=pl.BlockSpec((tm, tn), lambda i, j: (i, j)),
        out_shape=jax.ShapeDtypeStruct((m, n), out_dtype),
        scratch_shapes=scratch,
        compiler_params=_cparams(("parallel", "arbitrary")),
        name=name,
    )(*args)


def _da_proj_kernel(x_ref, g_ref, w_ref, q_ref, kf_ref, vf_ref, kb_ref, vb_ref):
    xn = _rms_rows(x_ref[...], g_ref[...]).astype(BF16)
    q_ref[...] = (_dot(xn, w_ref[:, :DA_DIM]) * (DA_QK ** -0.5 * LOG2_E)).astype(BF16)
    k = _dot(xn, w_ref[:, DA_DIM:2 * DA_DIM])
    kf_ref[...] = k
    kb_ref[...] = k.astype(BF16)
    v = _dot(xn, w_ref[:, 2 * DA_DIM:])
    vf_ref[...] = v
    vb_ref[...] = v.astype(BF16)


def _da_proj(x, norm_g, w_da, tm=512):
    m, k = x.shape
    tm = min(tm, m)
    blk = pl.BlockSpec((tm, DA_DIM), lambda i: (i, 0))
    return pl.pallas_call(
        _da_proj_kernel,
        grid=(m // tm,),
        in_specs=[pl.BlockSpec((tm, k), lambda i: (i, 0)),
                  pl.BlockSpec((1, k), lambda i: (0, 0)),
                  pl.BlockSpec((k, 3 * DA_DIM), lambda i: (0, 0), pipeline_mode=pl.Buffered(1))],
        out_specs=[blk] * 5,
        out_shape=[jax.ShapeDtypeStruct((m, DA_DIM), dt) for dt in (BF16, F32, F32, BF16, BF16)],
        compiler_params=_cparams(("parallel",)),
        name="da_proj",
    )(x, norm_g.reshape(1, -1), w_da)


def _mm_down_kernel(a_ref, w_ref, res_ref, g_ref, o_ref, *, final_norm):
    kk = pl.program_id(1)

    @pl.when(kk == 0)
    def _():
        o_ref[...] = res_ref[...]

    o_ref[...] += _dot(a_ref[...], w_ref[...])

    if final_norm:
        @pl.when(kk == pl.num_programs(1) - 1)
        def _():
            o_ref[...] = _rms_rows(o_ref[...], g_ref[...])


def _mm_down(a, w, res, final_g, final_norm, tm=512, tk=512):
    m, k = a.shape
    n = w.shape[1]
    tm = min(tm, m)
    assert m % tm == 0 and k % tk == 0
    return pl.pallas_call(
        functools.partial(_mm_down_kernel, final_norm=final_norm),
        grid=(m // tm, k // tk),
        in_specs=[pl.BlockSpec((tm, tk), lambda i, kk: (i, kk)),
                  pl.BlockSpec((tk, n), lambda i, kk: (kk, 0)),
                  pl.BlockSpec((tm, n), lambda i, kk: (i, 0)),
                  pl.BlockSpec((1, n), lambda i, kk: (0, 0))],
        out_specs=pl.BlockSpec((tm, n), lambda i, kk: (i, 0)),
        out_shape=jax.ShapeDtypeStruct((m, n), F32),
        compiler_params=_cparams(("parallel", "arbitrary")),
        name="ffn_down",
    )(a, w, res, final_g.reshape(1, -1))


def _head_sum(x):
    lo = lax.broadcasted_iota(jnp.int32, (1, LANES), 1) < HEAD
    s_lo = jnp.sum(jnp.where(lo, x, 0.0), axis=-1, keepdims=True)
    s_all = jnp.sum(x, axis=-1, keepdims=True)
    return jnp.where(lo, s_lo, s_all - s_lo)


def _token_mix(p, p_prev, mu, w0, w2, a0, a2, g2, k_k, k_a):
    ps = p + (p_prev - p) * mu
    o1, o2, o3 = RWKV_DIM, 2 * RWKV_DIM, 3 * RWKV_DIM
    o4 = o3 + DECAY_LORA
    o5 = o4 + AAA_LORA
    r, k, v = ps[:, :o1], ps[:, o1:o2], ps[:, o2:o3]
    wd, ad, gd = ps[:, o3:o4], ps[:, o4:o5], ps[:, o5:]
    z = w0 + _dot(jnp.tanh(wd).astype(BF16), w2)
    w_log = -(jnp.maximum(-z, 0.0) + jnp.log(1.0 + jnp.exp(-jnp.abs(z)))) - 0.5
    lw = -jnp.exp(w_log)
    a = _sigmoid(a0 + _dot(ad.astype(BF16), a2))
    g = _dot(_sigmoid(gd).astype(BF16), g2)
    kraw = k * k_k
    kk = []
    for pr in range(PAIRS):
        kp = kraw[:, pr * LANES:(pr + 1) * LANES]
        kk.append(kp * lax.rsqrt(jnp.maximum(_head_sum(kp * kp), 1e-24)))
    return r, k * (1.0 + (a - 1.0) * k_a), v, kk, a, lw, g


def _mix_out(y, r, k2, v, g, r_k, ln_g, ln_b):
    mean = _head_sum(y) * (1.0 / HEAD)
    d = y - mean
    var = _head_sum(d * d) * (1.0 / HEAD)
    yn = d * lax.rsqrt(var + GN_EPS) * ln_g + ln_b
    return (yn + _head_sum(r * k2 * r_k) * v) * g


def _prep_kernel(p_ref, prev_ref, mu_ref, w0_ref, w2_ref, a0_ref, a2_ref, g2_ref, kk_ref, ka_ref,
                 r_o, k_o, v_o, kk_o, a_o, lw_o, g_o):
    r, k2, v, kk, a, lw, g = _token_mix(p_ref[...], prev_ref[...], mu_ref[...], w0_ref[...], w2_ref[...],
                                        a0_ref[...], a2_ref[...], g2_ref[...], kk_ref[...], ka_ref[...])
    for pr in range(PAIRS):
        kk_o[:, pr * LANES:(pr + 1) * LANES] = kk[pr]
    r_o[...] = r
    k_o[...] = k2
    v_o[...] = v
    a_o[...] = a
    lw_o[...] = lw
    g_o[...] = g


def _mix_params(lp):
    row = lambda x: x.reshape(1, -1)
    return (row(lp['mu']), row(lp['w0']), lp['w2'].astype(BF16), row(lp['a0']), lp['a2'].astype(BF16),
            lp['g2'].astype(BF16), row(lp['k_k']), row(lp['k_a']))


def _mix_param_specs(index_map):
    shapes = [(1, RWKV_PROJ), (1, RWKV_DIM), (DECAY_LORA, RWKV_DIM), (1, RWKV_DIM), (AAA_LORA, RWKV_DIM),
              (GATE_LORA, RWKV_DIM), (1, RWKV_DIM), (1, RWKV_DIM)]
    return [pl.BlockSpec(s, index_map) for s in shapes]


def _rwkv_prep(p, prev, lp):
    m = p.shape[0]
    blk_in = pl.BlockSpec((m, RWKV_PROJ), lambda i: (0, 0))
    out_blk = pl.BlockSpec((m, RWKV_DIM), lambda i: (0, 0))
    return pl.pallas_call(
        _prep_kernel,
        grid=(1,),
        in_specs=[blk_in, blk_in] + _mix_param_specs(lambda i: (0, 0)),
        out_specs=[out_blk] * 7,
        out_shape=[jax.ShapeDtypeStruct((m, RWKV_DIM), F32)] * 7,
        compiler_params=_cparams(("arbitrary",)),
        name="rwkv_prep",
    )(p, prev, *_mix_params(lp))


def _chunk_kernel(p_ref, mu_ref, w0_ref, w2_ref, a0_ref, a2_ref, g2_ref, kk_ref, ka_ref, rk_ref, lg_ref, lb_ref,
                  y_ref, s_out_ref, s_ref, prev_ref):
    c = pl.program_id(1)

    @pl.when(c == 0)
    def _():
        s_ref[...] = jnp.zeros_like(s_ref)
        prev_ref[...] = jnp.zeros_like(prev_ref)

    p = p_ref[...]
    rows = p.shape[0]
    n_chunks = rows // CHUNK
    first_row = lax.broadcasted_iota(jnp.int32, p.shape, 0) == 0
    p_prev = jnp.where(first_row, prev_ref[...], pltpu.roll(p, 1, 0))
    prev_ref[...] = p[rows - 1:rows, :]
    r_all, k_all, v_all, kk_all, a_all, lw_all, g_all = _token_mix(
        p, p_prev, mu_ref[...], w0_ref[...], w2_ref[...], a0_ref[...], a2_ref[...], g2_ref[...], kk_ref[...],
        ka_ref[...])

    lane = lax.broadcasted_iota(jnp.int32, (1, LANES), 1)
    m0 = jnp.where(lane < HEAD, 1.0, 0.0)
    m1 = 1.0 - m0
    ii = lax.broadcasted_iota(jnp.int32, (LANES, LANES), 0)
    jj = lax.broadcasted_iota(jnp.int32, (LANES, LANES), 1)
    strict = ii > jj
    incl = ii >= jj
    eye = jnp.where(ii == jj, 1.0, 0.0)
    ti = lax.broadcasted_iota(jnp.int32, (CHUNK, CHUNK), 0)
    tj = lax.broadcasted_iota(jnp.int32, (CHUNK, CHUNK), 1)
    tril = jnp.where(ti >= tj, 1.0, 0.0).astype(BF16)

    def stack(x):
        return jnp.concatenate([x * m0, x * m1], axis=0)

    pairs = range(PAIRS)
    items = [(ci, pr) for ci in range(n_chunks) for pr in pairs]
    tok = [slice(ci * CHUNK, (ci + 1) * CHUNK) for ci, _ in items]
    sls = [slice(pr * LANES, (pr + 1) * LANES) for _, pr in items]
    each = lambda fn, *cols: [fn(*xs) for xs in zip(*cols)]
    cut = lambda x: [x[t, sl] for t, sl in zip(tok, sls)]
    lw, k2, r, v = cut(lw_all), cut(k_all), cut(r_all), cut(v_all)
    kk = [kk_all[pr][t, :] for (_, pr), t in zip(items, tok)]
    bb = each(lambda x, y: x * y, kk, cut(a_all))
    cum_all = [_split_dot(lw_all[ci * CHUNK:(ci + 1) * CHUNK, :], tril, 3, left=True) for ci in range(n_chunks)]
    cum = [cum_all[ci][:, sl] for (ci, _), sl in zip(items, sls)]
    tot = [x[CHUNK - 1:CHUNK, :] for x in cum]
    inv_g = [jnp.exp(-x) for x in cum]
    to_end = each(lambda t, x: jnp.exp(t - x), tot, cum)
    a_t = each(lambda x, cm, l: stack(-x * jnp.exp(cm - l)).astype(BF16), kk, cum, lw)
    b_t = each(lambda x, g: stack(x * g).astype(BF16), bb, inv_g)
    k_t = each(lambda x, g: stack(x * g).astype(BF16), k2, inv_g)
    r_f = each(lambda x, cm: stack(x * jnp.exp(cm)), r, cum)
    r_t = [x.astype(BF16) for x in r_f]
    v_f = [stack(x) for x in v]
    v_s = [x.astype(BF16) for x in v_f]
    bk_end = each(lambda x, y, e: jnp.concatenate([stack(x * e), stack(y * e)], axis=0).astype(BF16), bb, k2, to_end)

    a_ab = each(lambda x, y: jnp.where(strict, _dot_nt(x, y), 0.0), a_t, b_t)
    a_ak = each(lambda x, y: jnp.where(strict, _dot_nt(x, y), 0.0).astype(BF16), a_t, k_t)
    a_rb = each(lambda x, y: jnp.where(incl, _dot_nt(x, y), 0.0).astype(BF16), r_t, b_t)
    a_rk = each(lambda x, y: jnp.where(incl, _dot_nt(x, y), 0.0).astype(BF16), r_t, k_t)
    inv = [eye + x for x in a_ab]
    power = [x.astype(BF16) for x in a_ab]
    for _ in range(5):
        power = [_dot(x, x).astype(BF16) for x in power]
        inv = each(lambda x, p: x + _dot(x.astype(BF16), p), inv, power)
    w1 = each(lambda x, y: _dot(x, y).astype(BF16), a_ak, v_s)
    pu = each(lambda x, y, z: _dot(x.astype(BF16), jnp.concatenate([y, z], axis=1)), inv, a_t, w1)
    pu_b = [x.astype(BF16) for x in pu]
    qy = each(_dot, a_rb, pu_b)
    q_m = each(lambda x, y: (x + y[:, :LANES]).astype(BF16), r_f, qy)
    y_v = each(lambda x, y, z: x[:, LANES:] + _dot(y, z), qy, a_rk, v_s)
    pq = each(lambda x, y: jnp.concatenate([x[:, :LANES], y], axis=0), pu_b, q_m)
    state = [s_ref[pr] for pr in pairs]
    g_cut = cut(g_all)
    for ci in range(n_chunks):
        grp = slice(ci * PAIRS, (ci + 1) * PAIRS)
        uy = each(lambda x, s: _dot_nt(x, s.astype(BF16)), pq[grp], state)
        uv_t = each(lambda x, y, z: jnp.concatenate([x[:LANES] + y[:, LANES:], z], axis=0).T.astype(BF16),
                    uy, pu[grp], v_f[grp])
        state = each(lambda s, t, x, y: s * jnp.exp(t) + _dot(x, y), state, tot[grp], uv_t, bk_end[grp])
        for pr in pairs:
            i = ci * PAIRS + pr
            y_m = uy[pr][LANES:] + y_v[i]
            out = _mix_out(y_m[:CHUNK] + y_m[CHUNK:], r[i], k2[i], v[i], g_cut[i], rk_ref[:, sls[i]],
                           lg_ref[:, sls[i]], lb_ref[:, sls[i]])
            y_ref[tok[i], sls[i]] = out.astype(y_ref.dtype)
    for pr in pairs:
        s_ref[pr] = state[pr]

    @pl.when(c == pl.num_programs(1) - 1)
    def _():
        s_out_ref[0] = s_ref[...]


def _rwkv_chunk_scan(p, lp, batch):
    m = p.shape[0]
    n_chunks = m // batch // CHUNK
    cps = 2 if n_chunks % 2 == 0 else 1
    nc = n_chunks // cps
    row = lambda x: x.reshape(1, -1)
    const = lambda b, c: (0, 0)
    vec = pl.BlockSpec((1, RWKV_DIM), const)
    y, s_pairs = pl.pallas_call(
        _chunk_kernel,
        grid=(batch, nc),
        in_specs=([pl.BlockSpec((cps * CHUNK, RWKV_PROJ), lambda b, c: (b * nc + c, 0))] + _mix_param_specs(const)
                  + [vec] * 3),
        out_specs=[pl.BlockSpec((cps * CHUNK, RWKV_DIM), lambda b, c: (b * nc + c, 0)),
                   pl.BlockSpec((1, PAIRS, LANES, LANES), lambda b, c: (b, 0, 0, 0))],
        out_shape=[jax.ShapeDtypeStruct((m, RWKV_DIM), BF16),
                   jax.ShapeDtypeStruct((batch, PAIRS, LANES, LANES), F32)],
        scratch_shapes=[pltpu.VMEM((PAIRS, LANES, LANES), F32), pltpu.VMEM((1, RWKV_PROJ), F32)],
        compiler_params=_cparams(("parallel", "arbitrary")),
        name="rwkv_chunk",
    )(p, *_mix_params(lp), row(lp['r_k']), row(lp['lnx_g']), row(lp['lnx_b']))
    s_fin = jnp.stack([s_pairs[:, :, :HEAD, :HEAD], s_pairs[:, :, HEAD:, HEAD:]], axis=2)
    return y, s_fin.reshape(batch, RWKV_HEADS, HEAD, HEAD)


def _step_kernel(s_ref, r_ref, k_ref, kk_ref, a_ref, lw_ref, v_ref, s_o, y_o):
    s = s_ref[0]
    kk = kk_ref[0]
    sa = jnp.sum(s * (-kk), axis=-1, keepdims=True)
    s_new = s * jnp.exp(lw_ref[0]) + sa * (kk * a_ref[0]) + v_ref[0] * k_ref[0]
    s_o[0] = s_new
    y_o[0] = jnp.sum(s_new * r_ref[0], axis=-1, keepdims=True)


def _rwkv_step(s0, r, k2, v, kk, a, lw):
    b = s0.shape[0]
    rowv = lambda x: x.reshape(b, RWKV_HEADS, 1, HEAD)
    row_spec = pl.BlockSpec((1, RWKV_HEADS, 1, HEAD), lambda i: (i, 0, 0, 0))
    col_spec = pl.BlockSpec((1, RWKV_HEADS, HEAD, 1), lambda i: (i, 0, 0, 0))
    s_spec = pl.BlockSpec((1, RWKV_HEADS, HEAD, HEAD), lambda i: (i, 0, 0, 0))
    s_new, y = pl.pallas_call(
        _step_kernel,
        grid=(b,),
        in_specs=[s_spec] + [row_spec] * 5 + [col_spec],
        out_specs=[s_spec, col_spec],
        out_shape=[jax.ShapeDtypeStruct(s0.shape, F32), jax.ShapeDtypeStruct((b, RWKV_HEADS, HEAD, 1), F32)],
        compiler_params=_cparams(("parallel",)),
        name="rwkv_step",
    )(s0, rowv(r), rowv(k2), rowv(kk), rowv(a), rowv(lw), v.reshape(b, RWKV_HEADS, HEAD, 1))
    return y.reshape(b, RWKV_DIM), s_new


def _post_kernel(y_ref, r_ref, k_ref, v_ref, g_ref, rk_ref, lg_ref, lb_ref, o_ref):
    for pr in range(PAIRS):
        sl = slice(pr * LANES, (pr + 1) * LANES)
        out = _mix_out(y_ref[:, sl], r_ref[:, sl], k_ref[:, sl], v_ref[:, sl], g_ref[:, sl], rk_ref[:, sl],
                       lg_ref[:, sl], lb_ref[:, sl])
        o_ref[:, sl] = out.astype(o_ref.dtype)


def _rwkv_post(y, r, k2, v, g, lp):
    m = y.shape[0]
    blk = pl.BlockSpec((m, RWKV_DIM), lambda i: (0, 0))
    par = pl.BlockSpec((1, RWKV_DIM), lambda i: (0, 0))
    return pl.pallas_call(
        _post_kernel,
        grid=(1,),
        in_specs=[blk] * 5 + [par] * 3,
        out_specs=blk,
        out_shape=jax.ShapeDtypeStruct((m, RWKV_DIM), BF16),
        compiler_params=_cparams(("arbitrary",)),
        name="rwkv_post",
    )(y, r, k2, v, g, lp['r_k'].reshape(1, -1), lp['lnx_g'].reshape(1, -1), lp['lnx_b'].reshape(1, -1))


def _lambda(lq1_ref, lk1_ref, lq2_ref, lk2_ref, lam_init):
    s1 = jnp.sum(lq1_ref[...] * lk1_ref[...], axis=-1, keepdims=True)
    s2 = jnp.sum(lq2_ref[...] * lk2_ref[...], axis=-1, keepdims=True)
    return jnp.exp(s1) - jnp.exp(s2) + lam_init


def _dattn_kernel(lq1_ref, lk1_ref, lq2_ref, lk2_ref, sg_ref, q_ref, k_ref, v_ref, o_ref,
                  m_ref, acc_ref, *, tq, lam_init):
    nq = q_ref.shape[0] // tq
    lane = lax.broadcasted_iota(jnp.int32, (1, LANES), 1)
    row = lax.broadcasted_iota(jnp.int32, (tq, tq), 0)
    col = lax.broadcasted_iota(jnp.int32, (tq, tq), 1)
    ones = jnp.ones((tq, LANES), BF16)
    lam = _lambda(lq1_ref, lk1_ref, lq2_ref, lk2_ref, lam_init)
    rows = lambda ref, i: ref[i * tq:(i + 1) * tq, :]

    def scores(qi, j):
        q = rows(q_ref, qi)
        zero = jnp.zeros_like(q)
        ks = rows(k_ref, j)
        return [_dot_nt(jnp.where(lane < DA_QK, q, zero), ks), _dot_nt(jnp.where(lane >= DA_QK, q, zero), ks)]

    blocks = [(qi, j) for qi in range(nq) for j in range(qi + 1)]
    s_next = scores(*blocks[0])
    for n, (qi, j) in enumerate(blocks):
        s = s_next
        if n + 1 < len(blocks):
            s_next = scores(*blocks[n + 1])
        first, last = j == 0, j == qi
        if last:
            s = [jnp.where(col <= row, x, NEG_INF) for x in s]
        v_aug = jnp.concatenate([rows(v_ref, j), ones], axis=1)
        row_max = [jnp.max(x, axis=-1, keepdims=True) for x in s]
        m_old = None if first else [m_ref[c] for c in range(2)]
        m_new = row_max if first else [jnp.maximum(mo, mx) for mo, mx in zip(m_old, row_max)]
        p = [jnp.exp2(x - mn).astype(BF16) for x, mn in zip(s, m_new)]
        acc = [_dot(x, v_aug) for x in p]
        if not first:
            acc = [jnp.exp2(m_old[c] - m_new[c]) * acc_ref[c] + acc[c] for c in range(2)]
        if last:
            a0, a1 = acc
            o = a0[:, :LANES] / a0[:, LANES:] - lam * (a1[:, :LANES] / a1[:, LANES:])
            o_ref[qi * tq:(qi + 1) * tq, :] = (_rms_rows(o, sg_ref[...]) * (1.0 - lam_init)).astype(o_ref.dtype)
        else:
            for c in range(2):
                acc_ref[c] = acc[c]
                m_ref[c] = m_new[c]


def _diff_attn_prompt(qb, kb, vb, lp, lam_init, batch, tq=512):
    m = qb.shape[0]
    t = m // batch
    tq = min(tq, t)
    vec = pl.BlockSpec((1, DA_QK), lambda b, h: (0, 0))
    seq_spec = pl.BlockSpec((t, LANES), lambda b, h: (b, h))
    return pl.pallas_call(
        functools.partial(_dattn_kernel, tq=tq, lam_init=lam_init),
        grid=(batch, DA_HEADS),
        in_specs=[vec] * 4 + [pl.BlockSpec((1, DA_V), lambda b, h: (0, 0)), seq_spec, seq_spec, seq_spec],
        out_specs=seq_spec,
        out_shape=jax.ShapeDtypeStruct((m, DA_DIM), BF16),
        scratch_shapes=[pltpu.VMEM((2, tq, 1), F32), pltpu.VMEM((2, tq, 2 * LANES), F32)],
        compiler_params=_cparams(("parallel", "parallel")),
        name="diff_attn_prompt",
    )(*[lp[n].reshape(1, -1) for n in ('lq1', 'lk1', 'lq2', 'lk2')], lp['subln_g'].reshape(1, -1), qb, kb, vb)


def _paged_kernel(pt_ref, lq1_ref, lk1_ref, lq2_ref, lk2_ref, sg_ref, q_ref, kn_ref, vn_ref, *rest, pps, lam_init):
    del pt_ref
    k_refs, v_refs = rest[:pps], rest[pps:2 * pps]
    o_ref, m_ref, l_ref, acc_ref = rest[2 * pps:]
    pg = pl.program_id(1)

    @pl.when(pg == 0)
    def _():
        m_ref[...] = jnp.full_like(m_ref, NEG_INF)
        l_ref[...] = jnp.zeros_like(l_ref)
        acc_ref[...] = jnp.zeros_like(acc_ref)

    q = q_ref[0]
    half_sum = _head_ones()
    swap = lambda x: pltpu.roll(x, DA_QK, 1)

    def update(k_list, v_list):
        n = k_list[0].shape[0]
        rows = n * DA_HEADS
        s = [_dot((k3 * q).reshape(rows, LANES).astype(BF16), half_sum) for k3 in k_list]
        m_old = m_ref[...]
        m_new = m_old
        for x in s:
            m_new = jnp.maximum(m_new, jnp.max(x.reshape(n, DA_HEADS, LANES), axis=0))
        alpha = jnp.exp2(m_old - m_new)
        l_add = acc_a = acc_b = jnp.zeros((DA_HEADS, LANES), F32)
        for x, v3 in zip(s, v_list):
            p = jnp.exp2(x.reshape(n, DA_HEADS, LANES) - m_new)
            p_swapped = swap(p.reshape(rows, LANES)).reshape(n, DA_HEADS, LANES)
            l_add = l_add + jnp.sum(p, axis=0)
            acc_a = acc_a + jnp.sum(p * v3, axis=0)
            acc_b = acc_b + jnp.sum(p_swapped * v3, axis=0)
        l_ref[...] = alpha * l_ref[...] + l_add
        acc_ref[0] = alpha * acc_ref[0] + acc_a
        acc_ref[1] = swap(alpha) * acc_ref[1] + acc_b
        m_ref[...] = m_new

    update([r[0] for r in k_refs], [r[0] for r in v_refs])

    @pl.when(pg == pl.num_programs(1) - 1)
    def _():
        update([kn_ref[...]], [vn_ref[...]])
        lam = _lambda(lq1_ref, lk1_ref, lq2_ref, lk2_ref, lam_init)
        l = l_ref[...]
        straight = acc_ref[0] / l
        crossed = acc_ref[1] / swap(l)
        low = lax.broadcasted_iota(jnp.int32, (1, LANES), 1) < DA_QK
        o = jnp.where(low, straight, crossed) - lam * jnp.where(low, crossed, straight)
        o_ref[0] = (_rms_rows(o, sg_ref[...]) * (1.0 - lam_init)).astype(o_ref.dtype)


def _diff_attn_paged(q, k_new, v_new, cache_k, cache_v, page_table, lp, lam_init):
    b, n_pages = page_table.shape
    page = cache_k.shape[1]
    pps = next(c for c in (8, 4, 2, 1) if n_pages % c == 0)
    vec = pl.BlockSpec((1, DA_QK), lambda i, p, pt: (0, 0))
    rowb = pl.BlockSpec((1, DA_HEADS, LANES), lambda i, p, pt: (i, 0, 0))

    def page_spec(j):
        return pl.BlockSpec((1, page, DA_HEADS, LANES),
                            lambda i, p, pt: (pt[i * n_pages + p * pps + j], 0, 0, 0))

    grid_spec = pltpu.PrefetchScalarGridSpec(
        num_scalar_prefetch=1,
        grid=(b, n_pages // pps),
        in_specs=([vec] * 4 + [pl.BlockSpec((1, DA_V), lambda i, p, pt: (0, 0)), rowb, rowb, rowb]
                  + [page_spec(j) for j in range(pps)] * 2),
        out_specs=rowb,
        scratch_shapes=[pltpu.VMEM((DA_HEADS, LANES), F32)] * 2 + [pltpu.VMEM((2, DA_HEADS, LANES), F32)],
    )
    r3 = lambda x: x.reshape(b, DA_HEADS, LANES)
    out = pl.pallas_call(
        functools.partial(_paged_kernel, pps=pps, lam_init=lam_init),
        grid_spec=grid_spec,
        out_shape=jax.ShapeDtypeStruct((b, DA_HEADS, LANES), BF16),
        compiler_params=_cparams(("parallel", "arbitrary")),
        name="diff_attn_paged",
    )(page_table.reshape(-1), *[lp[n].reshape(1, -1) for n in ('lq1', 'lk1', 'lq2', 'lk2')],
      lp['subln_g'].reshape(1, -1), r3(q), r3(k_new), r3(v_new), *([cache_k] * pps), *([cache_v] * pps))
    return out.reshape(b, DA_DIM)


def _cross_kernel(q_ref, mk_ref, mv_ref, o_ref):
    for h in range(CROSS_HEADS):
        sl = slice(h * CROSS_HD, (h + 1) * CROSS_HD)
        s = _dot_nt(q_ref[0, :, sl], mk_ref[0, :, sl].astype(BF16))
        m = jnp.max(s, axis=-1, keepdims=True)
        p = jnp.exp(s - m)
        l = jnp.sum(p, axis=-1, keepdims=True)
        o_ref[0, :, sl] = (_dot(p.astype(BF16), mv_ref[0, :, sl].astype(BF16)) / l).astype(o_ref.dtype)


def _cross_decode_kernel(q_ref, mk_ref, mv_ref, o_ref):
    q = q_ref[0].astype(F32)
    s = jnp.sum(mk_ref[0] * q, axis=-1, keepdims=True)
    p = jnp.exp(s - jnp.max(s, axis=0, keepdims=True))
    o = jnp.sum(p * mv_ref[0], axis=0) / jnp.sum(p, axis=0)
    o_ref[0] = o.astype(o_ref.dtype)


def _cross_attn_decode(q, mk, mv):
    b = q.shape[0]
    mt = mk.shape[1]
    q_spec = pl.BlockSpec((1, CROSS_HEADS, CROSS_HD), lambda i: (i, 0, 0))
    m_spec = pl.BlockSpec((1, mt, CROSS_HEADS, CROSS_HD), lambda i: (i, 0, 0, 0))
    out = pl.pallas_call(
        _cross_decode_kernel,
        grid=(b,),
        in_specs=[q_spec, m_spec, m_spec],
        out_specs=q_spec,
        out_shape=jax.ShapeDtypeStruct((b, CROSS_HEADS, CROSS_HD), BF16),
        compiler_params=_cparams(("parallel",)),
        name="cross_attn_decode",
    )(q.reshape(b, CROSS_HEADS, CROSS_HD), mk, mv)
    return out.reshape(b, D_MODEL)


def _cross_attn(q, mk, mv, tq=512):
    b, t, _ = q.shape
    mt = mk.shape[1]
    tq = min(tq, t)
    q_spec = pl.BlockSpec((1, tq, D_MODEL), lambda i, j: (i, j, 0))
    m_spec = pl.BlockSpec((1, mt, D_MODEL), lambda i, j: (i, 0, 0))
    return pl.pallas_call(
        _cross_kernel,
        grid=(b, t // tq),
        in_specs=[q_spec, m_spec, m_spec],
        out_specs=q_spec,
        out_shape=jax.ShapeDtypeStruct(q.shape, BF16),
        compiler_params=_cparams(("parallel", "parallel")),
        name="cross_attn",
    )(q, mk, mv)


def _layer(x, batch, lp, wts, lam_init, mk, mv, rwkv_fn, attn_fn):
    m = x.shape[0]
    t = m // batch
    pr = _mm([x], [wts['in_r']], name="in_proj_rwkv", norm_g=lp['norm_mix_g'], tm=512)
    qb, kf, vf, kb, vb = _da_proj(x, lp['norm_mix_g'], wts['in_da'])
    y_r, s_new = rwkv_fn(pr)
    y_d = attn_fn(qb, kf, vf, kb, vb)
    x = _mm([y_r, y_d], [wts['o_r'], wts['o_d']], name="out_proj", res=x, tm=512)
    qc = _mm([x], [wts['cq']], name="cross_q", norm_g=lp['norm_cross_g'], scale=CROSS_HD ** -0.5, out_dtype=BF16,
             tm=1024)
    if mk.ndim == 4:
        oc = _cross_attn_decode(qc, mk, mv)
    else:
        oc = _cross_attn(qc.reshape(batch, t, D_MODEL), mk, mv).reshape(m, D_MODEL)
    x = _mm([oc], [wts['co']], name="cross_out", res=x, tm=512)
    hmid = _mm([x], [wts['gate'], wts['up']], name="ffn_swiglu", norm_g=lp['norm_ffn_g'], swiglu=True,
               out_dtype=BF16, tm=1024, tn=512)
    x = _mm_down(hmid, wts['down'], x, wts['final_g'], wts['is_last'], tm=1024, tk=hmid.shape[1] // 4)
    last_row = pr.reshape(batch, t, RWKV_PROJ)[:, -1]
    return x, kf, vf, s_new, last_row


def kernel(x_prompt, x_sample, cache_k, cache_v, cache_mem_k, cache_mem_v, state_wkv, state_shift, page_table, mem_prompt, norm_mix_g, w_in, tok_shift_mu, rwkv_w0, rwkv_w2, rwkv_a0, rwkv_a2, rwkv_g2, rwkv_k_k, rwkv_k_a, rwkv_r_k, rwkv_lnx_g, rwkv_lnx_b, diff_lq1, diff_lk1, diff_lq2, diff_lk2, diff_subln_g, w_o, norm_cross_g, norm_mem_g, w_cq, w_ck, w_cv, w_co, norm_ffn_g, w_gate, w_up, w_down, final_norm_g):
    b, t, _ = x_prompt.shape
    db, dt, _ = x_sample.shape
    depth = w_in.shape[0]
    n_mem = mem_prompt.shape[1]
    n_pool, page = cache_k.shape[1], cache_k.shape[2]
    xp = x_prompt.reshape(b * t, D_MODEL)
    xs = x_sample.reshape(db * dt, D_MODEL)
    outs = {n: [] for n in ('k_p', 'v_p', 'k_s', 'v_s', 'S_p', 'S_s', 'sh_p', 'sh_s', 'mk_p', 'mv_p')}
    for l in range(depth):
        lam_init = 0.8 - 0.6 * math.exp(-0.3 * l)
        lp = dict(norm_mix_g=norm_mix_g[l], mu=tok_shift_mu[l], w0=rwkv_w0[l], w2=rwkv_w2[l], a0=rwkv_a0[l],
                  a2=rwkv_a2[l], g2=rwkv_g2[l], k_k=rwkv_k_k[l], k_a=rwkv_k_a[l], r_k=rwkv_r_k[l],
                  lnx_g=rwkv_lnx_g[l], lnx_b=rwkv_lnx_b[l], lq1=diff_lq1[l], lk1=diff_lk1[l], lq2=diff_lq2[l],
                  lk2=diff_lk2[l], subln_g=diff_subln_g[l], norm_cross_g=norm_cross_g[l], norm_ffn_g=norm_ffn_g[l])
        wl = w_in[l].astype(BF16)
        wts = dict(in_r=wl[:, :RWKV_PROJ], in_da=wl[:, RWKV_PROJ:], o_r=w_o[l][:RWKV_DIM].astype(BF16),
                   o_d=w_o[l][RWKV_DIM:].astype(BF16), cq=w_cq[l].astype(BF16), co=w_co[l].astype(BF16),
                   gate=w_gate[l].astype(BF16), up=w_up[l].astype(BF16), down=w_down[l].astype(BF16),
                   final_g=final_norm_g, is_last=l == depth - 1)

        mem2 = mem_prompt.reshape(b * n_mem, D_MODEL)
        mk = _mm([mem2], [w_ck[l].astype(BF16)], name="mem_k", norm_g=norm_mem_g[l], tm=1024)
        mv = _mm([mem2], [w_cv[l].astype(BF16)], name="mem_v", norm_g=norm_mem_g[l], tm=1024)

        def rwkv_prompt(pr):
            return _rwkv_chunk_scan(pr, lp, b)

        def attn_prompt(qb, kf, vf, kb, vb):
            return _diff_attn_prompt(qb, kb, vb, lp, lam_init, b)

        xp, kf, vf, s_new, last = _layer(xp, b, lp, wts, lam_init, mk.reshape(b, n_mem, D_MODEL),
                                         mv.reshape(b, n_mem, D_MODEL), rwkv_prompt, attn_prompt)
        outs['k_p'].append(kf.reshape(b, t, DA_HEADS, 2 * DA_QK))
        outs['v_p'].append(vf.reshape(b, t, DA_HEADS, DA_V))
        outs['S_p'].append(s_new)
        outs['sh_p'].append(last)
        outs['mk_p'].append(mk.reshape(b, n_mem, CROSS_HEADS, CROSS_HD))
        outs['mv_p'].append(mv.reshape(b, n_mem, CROSS_HEADS, CROSS_HD))

        assert dt == 1

        def rwkv_sample(pr):
            r, k2, v, kk, a, lw, g = _rwkv_prep(pr, state_shift[l], lp)
            y, s_fin = _rwkv_step(state_wkv[l], r, k2, v, kk, a, lw)
            return _rwkv_post(y, r, k2, v, g, lp), s_fin

        def attn_sample(qb, kf, vf, kb, vb):
            return _diff_attn_paged(qb.astype(F32), kf, vf, cache_k[l], cache_v[l], page_table, lp, lam_init)

        xs, kf, vf, s_new, last = _layer(xs, db, lp, wts, lam_init, cache_mem_k[l], cache_mem_v[l],
                                         rwkv_sample, attn_sample)
        outs['k_s'].append(kf.reshape(db, dt, DA_HEADS, 2 * DA_QK))
        outs['v_s'].append(vf.reshape(db, dt, DA_HEADS, DA_V))
        outs['S_s'].append(s_new)
        outs['sh_s'].append(last)
    y_prompt = xp.reshape(b, t, D_MODEL)
    y_sample = xs.reshape(db, dt, D_MODEL)
    st = lambda n: jnp.stack(outs[n])
    return (y_prompt, y_sample, st('k_p'), st('v_p'), st('k_s'), st('v_s'), st('S_p'), st('S_s'),
            st('sh_p'), st('sh_s'), st('mk_p'), st('mv_p'))
```

```python
import functools
import math

import jax
import jax.numpy as jnp
from jax import lax
from jax.experimental import pallas as pl
from jax.experimental.pallas import tpu as pltpu

F32 = jnp.float32
BF16 = jnp.bfloat16

D_MODEL = 2048
RWKV_DIM = 1024
HEAD = 64
RWKV_HEADS = 16
DECAY_LORA = 64
AAA_LORA = 64
GATE_LORA = 128
RWKV_PROJ = 3 * RWKV_DIM + DECAY_LORA + AAA_LORA + GATE_LORA
DA_DIM = 1024
DA_QK = 64
DA_V = 128
DA_HEADS = 8
CROSS_HEADS = 4
CROSS_HD = D_MODEL // CROSS_HEADS
RMS_EPS = 1e-6
GN_EPS = 64e-5
NEG_INF = -1e30
LOG2_E = 1.4426950408889634

LANES = 128
CHUNK = 64
PAIRS = RWKV_HEADS // 2
VMEM_LIMIT = 56 * 1024 * 1024

_NT = (((1,), (1,)), ((), ()))


def _cparams(sem):
    return pltpu.CompilerParams(dimension_semantics=sem, vmem_limit_bytes=VMEM_LIMIT)


def _dot(a, b):
    return jnp.dot(a, b, preferred_element_type=F32)


def _dot_nt(a, b):
    return lax.dot_general(a, b, _NT, preferred_element_type=F32)


def _rms_rows(x, g):
    x = x.astype(F32)
    ms = jnp.mean(x * x, axis=-1, keepdims=True)
    return x * lax.rsqrt(ms + RMS_EPS) * g


def _sigmoid(x):
    return 1.0 / (1.0 + jnp.exp(-x))


def _split_dot(x, m_bf16, terms, left=False):
    acc = None
    rem = x
    for t in range(terms):
        piece = rem.astype(BF16)
        d = _dot(m_bf16, piece) if left else _dot(piece, m_bf16)
        acc = d if acc is None else acc + d
        if t + 1 < terms:
            rem = rem - piece.astype(F32)
    return acc


def _head_ones():
    i = lax.broadcasted_iota(jnp.int32, (LANES, LANES), 0) // HEAD
    j = lax.broadcasted_iota(jnp.int32, (LANES, LANES), 1) // HEAD
    return jnp.where(i == j, 1.0, 0.0).astype(BF16)


def _mm_kernel(*refs, n_a, has_norm, swiglu, has_res, scale):
    refs = list(refs)
    a_refs = [refs.pop(0) for _ in range(n_a)]
    g_ref = refs.pop(0) if has_norm else None
    w_refs = [refs.pop(0) for _ in range(2 if swiglu else n_a)]
    res_ref = refs.pop(0) if has_res else None
    o_ref = refs.pop(0)
    if has_norm:
        xn_ref = refs.pop(0)

        @pl.when(pl.program_id(1) == 0)
        def _():
            xn_ref[...] = _rms_rows(a_refs[0][...], g_ref[...]).astype(BF16)

        lhs = [xn_ref[...]]
    else:
        lhs = [r[...] for r in a_refs]
    if swiglu:
        gate = _dot(lhs[0], w_refs[0][...])
        up = _dot(lhs[0], w_refs[1][...])
        acc = gate * _sigmoid(gate) * up
    else:
        acc = _dot(lhs[0], w_refs[0][...])
        for a, w in zip(lhs[1:], w_refs[1:]):
            acc = acc + _dot(a, w[...])
    if scale != 1.0:
        acc = acc * scale
    if has_res:
        acc = acc + res_ref[...]
    o_ref[...] = acc.astype(o_ref.dtype)


def _mm(a_list, w_list, *, name, norm_g=None, swiglu=False, res=None, scale=1.0, out_dtype=F32, tm=512, tn=None):
    m = a_list[0].shape[0]
    n = w_list[0].shape[1]
    tm = min(tm, m)
    resident = tn is None
    tn = n if resident else min(tn, n)
    assert m % tm == 0 and n % tn == 0
    has_norm = norm_g is not None
    in_specs = [pl.BlockSpec((tm, a.shape[1]), lambda i, j: (i, 0)) for a in a_list]
    args = list(a_list)
    if has_norm:
        in_specs.append(pl.BlockSpec((1, a_list[0].shape[1]), lambda i, j: (0, 0)))
        args.append(norm_g.reshape(1, -1))
    for w in w_list:
        if resident:
            in_specs.append(pl.BlockSpec((w.shape[0], n), lambda i, j: (0, 0), pipeline_mode=pl.Buffered(1)))
        else:
            in_specs.append(pl.BlockSpec((w.shape[0], tn), lambda i, j: (0, j)))
        args.append(w)
    if res is not None:
        in_specs.append(pl.BlockSpec((tm, tn), lambda i, j: (i, j)))
        args.append(res)
    scratch = [pltpu.VMEM((tm, a_list[0].shape[1]), BF16)] if has_norm else []
    return pl.pallas_call(
        functools.partial(_mm_kernel, n_a=len(a_list), has_norm=has_norm, swiglu=swiglu,
                          has_res=res is not None, scale=scale),
        grid=(m // tm, n // tn),
        in_specs=in_specs,
        out_specs=pl.BlockSpec((tm, tn), lambda i, j: (i, j)),
        out_shape=jax.ShapeDtypeStruct((m, n), out_dtype),
        scratch_shapes=scratch,
        compiler_params=_cparams(("parallel", "arbitrary")),
        name=name,
    )(*args)


def _da_proj_kernel(x_ref, g_ref, w_ref, q_ref, kf_ref, vf_ref, kb_ref, vb_ref):
    xn = _rms_rows(x_ref[...], g_ref[...]).astype(BF16)
    q_ref[...] = (_dot(xn, w_ref[:, :DA_DIM]) * (DA_QK ** -0.5 * LOG2_E)).astype(BF16)
    k = _dot(xn, w_ref[:, DA_DIM:2 * DA_DIM])
    kf_ref[...] = k
    kb_ref[...] = k.astype(BF16)
    v = _dot(xn, w_ref[:, 2 * DA_DIM:])
    vf_ref[...] = v
    vb_ref[...] = v.astype(BF16)


def _da_proj(x, norm_g, w_da, tm=512):
    m, k = x.shape
    tm = min(tm, m)
    blk = pl.BlockSpec((tm, DA_DIM), lambda i: (i, 0))
    return pl.pallas_call(
        _da_proj_kernel,
        grid=(m // tm,),
        in_specs=[pl.BlockSpec((tm, k), lambda i: (i, 0)),
                  pl.BlockSpec((1, k), lambda i: (0, 0)),
                  pl.BlockSpec((k, 3 * DA_DIM), lambda i: (0, 0), pipeline_mode=pl.Buffered(1))],
        out_specs=[blk] * 5,
        out_shape=[jax.ShapeDtypeStruct((m, DA_DIM), dt) for dt in (BF16, F32, F32, BF16, BF16)],
        compiler_params=_cparams(("parallel",)),
        name="da_proj",
    )(x, norm_g.reshape(1, -1), w_da)


def _mm_down_kernel(a_ref, w_ref, res_ref, g_ref, o_ref, *, final_norm):
    kk = pl.program_id(1)

    @pl.when(kk == 0)
    def _():
        o_ref[...] = res_ref[...]

    o_ref[...] += _dot(a_ref[...], w_ref[...])

    if final_norm:
        @pl.when(kk == pl.num_programs(1) - 1)
        def _():
            o_ref[...] = _rms_rows(o_ref[...], g_ref[...])


def _mm_down(a, w, res, final_g, final_norm, tm=512, tk=512):
    m, k = a.shape
    n = w.shape[1]
    tm = min(tm, m)
    assert m % tm == 0 and k % tk == 0
    return pl.pallas_call(
        functools.partial(_mm_down_kernel, final_norm=final_norm),
        grid=(m // tm, k // tk),
        in_specs=[pl.BlockSpec((tm, tk), lambda i, kk: (i, kk)),
                  pl.BlockSpec((tk, n), lambda i, kk: (kk, 0)),
                  pl.BlockSpec((tm, n), lambda i, kk: (i, 0)),
                  pl.BlockSpec((1, n), lambda i, kk: (0, 0))],
        out_specs=pl.BlockSpec((tm, n), lambda i, kk: (i, 0)),
        out_shape=jax.ShapeDtypeStruct((m, n), F32),
        compiler_params=_cparams(("parallel", "arbitrary")),
        name="ffn_down",
    )(a, w, res, final_g.reshape(1, -1))


def _head_sum(x):
    lo = lax.broadcasted_iota(jnp.int32, (1, LANES), 1) < HEAD
    s_lo = jnp.sum(jnp.where(lo, x, 0.0), axis=-1, keepdims=True)
    s_all = jnp.sum(x, axis=-1, keepdims=True)
    return jnp.where(lo, s_lo, s_all - s_lo)


def _token_mix(p, p_prev, mu, w0, w2, a0, a2, g2, k_k, k_a):
    ps = p + (p_prev - p) * mu
    o1, o2, o3 = RWKV_DIM, 2 * RWKV_DIM, 3 * RWKV_DIM
    o4 = o3 + DECAY_LORA
    o5 = o4 + AAA_LORA
    r, k, v = ps[:, :o1], ps[:, o1:o2], ps[:, o2:o3]
    wd, ad, gd = ps[:, o3:o4], ps[:, o4:o5], ps[:, o5:]
    z = w0 + _dot(jnp.tanh(wd).astype(BF16), w2)
    w_log = -(jnp.maximum(-z, 0.0) + jnp.log(1.0 + jnp.exp(-jnp.abs(z)))) - 0.5
    lw = -jnp.exp(w_log)
    a = _sigmoid(a0 + _dot(ad.astype(BF16), a2))
    g = _dot(_sigmoid(gd).astype(BF16), g2)
    kraw = k * k_k
    kk = []
    for pr in range(PAIRS):
        kp = kraw[:, pr * LANES:(pr + 1) * LANES]
        kk.append(kp * lax.rsqrt(jnp.maximum(_head_sum(kp * kp), 1e-24)))
    return r, k * (1.0 + (a - 1.0) * k_a), v, kk, a, lw, g


def _mix_out(y, r, k2, v, g, r_k, ln_g, ln_b):
    mean = _head_sum(y) * (1.0 / HEAD)
    d = y - mean
    var = _head_sum(d * d) * (1.0 / HEAD)
    yn = d * lax.rsqrt(var + GN_EPS) * ln_g + ln_b
    return (yn + _head_sum(r * k2 * r_k) * v) * g


def _prep_kernel(p_ref, prev_ref, mu_ref, w0_ref, w2_ref, a0_ref, a2_ref, g2_ref, kk_ref, ka_ref,
                 r_o, k_o, v_o, kk_o, a_o, lw_o, g_o):
    r, k2, v, kk, a, lw, g = _token_mix(p_ref[...], prev_ref[...], mu_ref[...], w0_ref[...], w2_ref[...],
                                        a0_ref[...], a2_ref[...], g2_ref[...], kk_ref[...], ka_ref[...])
    for pr in range(PAIRS):
        kk_o[:, pr * LANES:(pr + 1) * LANES] = kk[pr]
    r_o[...] = r
    k_o[...] = k2
    v_o[...] = v
    a_o[...] = a
    lw_o[...] = lw
    g_o[...] = g


def _mix_params(lp):
    row = lambda x: x.reshape(1, -1)
    return (row(lp['mu']), row(lp['w0']), lp['w2'].astype(BF16), row(lp['a0']), lp['a2'].astype(BF16),
            lp['g2'].astype(BF16), row(lp['k_k']), row(lp['k_a']))


def _mix_param_specs(index_map):
    shapes = [(1, RWKV_PROJ), (1, RWKV_DIM), (DECAY_LORA, RWKV_DIM), (1, RWKV_DIM), (AAA_LORA, RWKV_DIM),
              (GATE_LORA, RWKV_DIM), (1, RWKV_DIM), (1, RWKV_DIM)]
    return [pl.BlockSpec(s, index_map) for s in shapes]


def _rwkv_prep(p, prev, lp):
    m = p.shape[0]
    blk_in = pl.BlockSpec((m, RWKV_PROJ), lambda i: (0, 0))
    out_blk = pl.BlockSpec((m, RWKV_DIM), lambda i: (0, 0))
    return pl.pallas_call(
        _prep_kernel,
        grid=(1,),
        in_specs=[blk_in, blk_in] + _mix_param_specs(lambda i: (0, 0)),
        out_specs=[out_blk] * 7,
        out_shape=[jax.ShapeDtypeStruct((m, RWKV_DIM), F32)] * 7,
        compiler_params=_cparams(("arbitrary",)),
        name="rwkv_prep",
    )(p, prev, *_mix_params(lp))


def _chunk_kernel(p_ref, mu_ref, w0_ref, w2_ref, a0_ref, a2_ref, g2_ref, kk_ref, ka_ref, rk_ref, lg_ref, lb_ref,
                  y_ref, s_out_ref, s_ref, prev_ref):
    c = pl.program_id(1)

    @pl.when(c == 0)
    def _():
        s_ref[...] = jnp.zeros_like(s_ref)
        prev_ref[...] = jnp.zeros_like(prev_ref)

    p = p_ref[...]
    rows = p.shape[0]
    n_chunks = rows // CHUNK
    first_row = lax.broadcasted_iota(jnp.int32, p.shape, 0) == 0
    p_prev = jnp.where(first_row, prev_ref[...], pltpu.roll(p, 1, 0))
    prev_ref[...] = p[rows - 1:rows, :]
    r_all, k_all, v_all, kk_all, a_all, lw_all, g_all = _token_mix(
        p, p_prev, mu_ref[...], w0_ref[...], w2_ref[...], a0_ref[...], a2_ref[...], g2_ref[...], kk_ref[...],
        ka_ref[...])

    lane = lax.broadcasted_iota(jnp.int32, (1, LANES), 1)
    m0 = jnp.where(lane < HEAD, 1.0, 0.0)
    m1 = 1.0 - m0
    ii = lax.broadcasted_iota(jnp.int32, (LANES, LANES), 0)
    jj = lax.broadcasted_iota(jnp.int32, (LANES, LANES), 1)
    strict = ii > jj
    incl = ii >= jj
    eye = jnp.where(ii == jj, 1.0, 0.0)
    ti = lax.broadcasted_iota(jnp.int32, (CHUNK, CHUNK), 0)
    tj = lax.broadcasted_iota(jnp.int32, (CHUNK, CHUNK), 1)
    tril = jnp.where(ti >= tj, 1.0, 0.0).astype(BF16)

    def stack(x):
        return jnp.concatenate([x * m0, x * m1], axis=0)

    pairs = range(PAIRS)
    items = [(ci, pr) for ci in range(n_chunks) for pr in pairs]
    tok = [slice(ci * CHUNK, (ci + 1) * CHUNK) for ci, _ in items]
    sls = [slice(pr * LANES, (pr + 1) * LANES) for _, pr in items]
    each = lambda fn, *cols: [fn(*xs) for xs in zip(*cols)]
    cut = lambda x: [x[t, sl] for t, sl in zip(tok, sls)]
    lw, k2, r, v = cut(lw_all), cut(k_all), cut(r_all), cut(v_all)
    kk = [kk_all[pr][t, :] for (_, pr), t in zip(items, tok)]
    bb = each(lambda x, y: x * y, kk, cut(a_all))
    cum_all = [_split_dot(lw_all[ci * CHUNK:(ci + 1) * CHUNK, :], tril, 3, left=True) for ci in range(n_chunks)]
    cum = [cum_all[ci][:, sl] for (ci, _), sl in zip(items, sls)]
    tot = [x[CHUNK - 1:CHUNK, :] for x in cum]
    inv_g = [jnp.exp(-x) for x in cum]
    to_end = each(lambda t, x: jnp.exp(t - x), tot, cum)
    a_t = each(lambda x, cm, l: stack(-x * jnp.exp(cm - l)).astype(BF16), kk, cum, lw)
    b_t = each(lambda x, g: stack(x * g).astype(BF16), bb, inv_g)
    k_t = each(lambda x, g: stack(x * g).astype(BF16), k2, inv_g)
    r_f = each(lambda x, cm: stack(x * jnp.exp(cm)), r, cum)
    r_t = [x.astype(BF16) for x in r_f]
    v_f = [stack(x) for x in v]
    v_s = [x.astype(BF16) for x in v_f]
    bk_end = each(lambda x, y, e: jnp.concatenate([stack(x * e), stack(y * e)], axis=0).astype(BF16), bb, k2, to_end)

    a_ab = each(lambda x, y: jnp.where(strict, _dot_nt(x, y), 0.0), a_t, b_t)
    a_ak = each(lambda x, y: jnp.where(strict, _dot_nt(x, y), 0.0).astype(BF16), a_t, k_t)
    a_rb = each(lambda x, y: jnp.where(incl, _dot_nt(x, y), 0.0).astype(BF16), r_t, b_t)
    a_rk = each(lambda x, y: jnp.where(incl, _dot_nt(x, y), 0.0).astype(BF16), r_t, k_t)
    inv = [eye + x for x in a_ab]
    power = [x.astype(BF16) for x in a_ab]
    for _ in range(5):
        power = [_dot(x, x).astype(BF16) for x in power]
        inv = each(lambda x, p: x + _dot(x.astype(BF16), p), inv, power)
    w1 = each(lambda x, y: _dot(x, y).astype(BF16), a_ak, v_s)
    pu = each(lambda x, y, z: _dot(x.astype(BF16), jnp.concatenate([y, z], axis=1)), inv, a_t, w1)
    pu_b = [x.astype(BF16) for x in pu]
    qy = each(_dot, a_rb, pu_b)
    q_m = each(lambda x, y: (x + y[:, :LANES]).astype(BF16), r_f, qy)
    y_v = each(lambda x, y, z: x[:, LANES:] + _dot(y, z), qy, a_rk, v_s)
    pq = each(lambda x, y: jnp.concatenate([x[:, :LANES], y], axis=0), pu_b, q_m)
    state = [s_ref[pr] for pr in pairs]
    g_cut = cut(g_all)
    for ci in range(n_chunks):
        grp = slice(ci * PAIRS, (ci + 1) * PAIRS)
        uy = each(lambda x, s: _dot_nt(x, s.astype(BF16)), pq[grp], state)
        uv_t = each(lambda x, y, z: jnp.concatenate([x[:LANES] + y[:, LANES:], z], axis=0).T.astype(BF16),
                    uy, pu[grp], v_f[grp])
        state = each(lambda s, t, x, y: s * jnp.exp(t) + _dot(x, y), state, tot[grp], uv_t, bk_end[grp])
        for pr in pairs:
            i = ci * PAIRS + pr
            y_m = uy[pr][LANES:] + y_v[i]
            out = _mix_out(y_m[:CHUNK] + y_m[CHUNK:], r[i], k2[i], v[i], g_cut[i], rk_ref[:, sls[i]],
                           lg_ref[:, sls[i]], lb_ref[:, sls[i]])
            y_ref[tok[i], sls[i]] = out.astype(y_ref.dtype)
    for pr in pairs:
        s_ref[pr] = state[pr]

    @pl.when(c == pl.num_programs(1) - 1)
    def _():
        s_out_ref[0] = s_ref[...]


def _rwkv_chunk_scan(p, lp, batch):
    m = p.shape[0]
    n_chunks = m // batch // CHUNK
    cps = 2 if n_chunks % 2 == 0 else 1
    nc = n_chunks // cps
    row = lambda x: x.reshape(1, -1)
    const = lambda b, c: (0, 0)
    vec = pl.BlockSpec((1, RWKV_DIM), const)
    y, s_pairs = pl.pallas_call(
        _chunk_kernel,
        grid=(batch, nc),
        in_specs=([pl.BlockSpec((cps * CHUNK, RWKV_PROJ), lambda b, c: (b * nc + c, 0))] + _mix_param_specs(const)
                  + [vec] * 3),
        out_specs=[pl.BlockSpec((cps * CHUNK, RWKV_DIM), lambda b, c: (b * nc + c, 0)),
                   pl.BlockSpec((1, PAIRS, LANES, LANES), lambda b, c: (b, 0, 0, 0))],
        out_shape=[jax.ShapeDtypeStruct((m, RWKV_DIM), BF16),
                   jax.ShapeDtypeStruct((batch, PAIRS, LANES, LANES), F32)],
        scratch_shapes=[pltpu.VMEM((PAIRS, LANES, LANES), F32), pltpu.VMEM((1, RWKV_PROJ), F32)],
        compiler_params=_cparams(("parallel", "arbitrary")),
        name="rwkv_chunk",
    )(p, *_mix_params(lp), row(lp['r_k']), row(lp['lnx_g']), row(lp['lnx_b']))
    s_fin = jnp.stack([s_pairs[:, :, :HEAD, :HEAD], s_pairs[:, :, HEAD:, HEAD:]], axis=2)
    return y, s_fin.reshape(batch, RWKV_HEADS, HEAD, HEAD)


def _step_kernel(s_ref, r_ref, k_ref, kk_ref, a_ref, lw_ref, v_ref, s_o, y_o):
    s = s_ref[0]
    kk = kk_ref[0]
    sa = jnp.sum(s * (-kk), axis=-1, keepdims=True)
    s_new = s * jnp.exp(lw_ref[0]) + sa * (kk * a_ref[0]) + v_ref[0] * k_ref[0]
    s_o[0] = s_new
    y_o[0] = jnp.sum(s_new * r_ref[0], axis=-1, keepdims=True)


def _rwkv_step(s0, r, k2, v, kk, a, lw):
    b = s0.shape[0]
    rowv = lambda x: x.reshape(b, RWKV_HEADS, 1, HEAD)
    row_spec = pl.BlockSpec((1, RWKV_HEADS, 1, HEAD), lambda i: (i, 0, 0, 0))
    col_spec = pl.BlockSpec((1, RWKV_HEADS, HEAD, 1), lambda i: (i, 0, 0, 0))
    s_spec = pl.BlockSpec((1, RWKV_HEADS, HEAD, HEAD), lambda i: (i, 0, 0, 0))
    s_new, y = pl.pallas_call(
        _step_kernel,
        grid=(b,),
        in_specs=[s_spec] + [row_spec] * 5 + [col_spec],
        out_specs=[s_spec, col_spec],
        out_shape=[jax.ShapeDtypeStruct(s0.shape, F32), jax.ShapeDtypeStruct((b, RWKV_HEADS, HEAD, 1), F32)],
        compiler_params=_cparams(("parallel",)),
        name="rwkv_step",
    )(s0, rowv(r), rowv(k2), rowv(kk), rowv(a), rowv(lw), v.reshape(b, RWKV_HEADS, HEAD, 1))
    return y.reshape(b, RWKV_DIM), s_new


def _post_kernel(y_ref, r_ref, k_ref, v_ref, g_ref, rk_ref, lg_ref, lb_ref, o_ref):
    for pr in range(PAIRS):
        sl = slice(pr * LANES, (pr + 1) * LANES)
        out = _mix_out(y_ref[:, sl], r_ref[:, sl], k_ref[:, sl], v_ref[:, sl], g_ref[:, sl], rk_ref[:, sl],
                       lg_ref[:, sl], lb_ref[:, sl])
        o_ref[:, sl] = out.astype(o_ref.dtype)


def _rwkv_post(y, r, k2, v, g, lp):
    m = y.shape[0]
    blk = pl.BlockSpec((m, RWKV_DIM), lambda i: (0, 0))
    par = pl.BlockSpec((1, RWKV_DIM), lambda i: (0, 0))
    return pl.pallas_call(
        _post_kernel,
        grid=(1,),
        in_specs=[blk] * 5 + [par] * 3,
        out_specs=blk,
        out_shape=jax.ShapeDtypeStruct((m, RWKV_DIM), BF16),
        compiler_params=_cparams(("arbitrary",)),
        name="rwkv_post",
    )(y, r, k2, v, g, lp['r_k'].reshape(1, -1), lp['lnx_g'].reshape(1, -1), lp['lnx_b'].reshape(1, -1))


def _lambda(lq1_ref, lk1_ref, lq2_ref, lk2_ref, lam_init):
    s1 = jnp.sum(lq1_ref[...] * lk1_ref[...], axis=-1, keepdims=True)
    s2 = jnp.sum(lq2_ref[...] * lk2_ref[...], axis=-1, keepdims=True)
    return jnp.exp(s1) - jnp.exp(s2) + lam_init


def _dattn_kernel(lq1_ref, lk1_ref, lq2_ref, lk2_ref, sg_ref, q_ref, k_ref, v_ref, o_ref,
                  acc_ref, *, tq, lam_init):
    nq = q_ref.shape[0] // tq
    lane = lax.broadcasted_iota(jnp.int32, (1, LANES), 1)
    row = lax.broadcasted_iota(jnp.int32, (tq, tq), 0)
    col = lax.broadcasted_iota(jnp.int32, (tq, tq), 1)
    ones = jnp.ones((tq, LANES), BF16)
    lam = _lambda(lq1_ref, lk1_ref, lq2_ref, lk2_ref, lam_init)
    rows = lambda ref, i: ref[i * tq:(i + 1) * tq, :]

    def scores(qi, j):
        q = rows(q_ref, qi)
        zero = jnp.zeros_like(q)
        ks = rows(k_ref, j)
        return [_dot_nt(jnp.where(lane < DA_QK, q, zero), ks), _dot_nt(jnp.where(lane >= DA_QK, q, zero), ks)]

    blocks = [(qi, j) for qi in range(nq) for j in range(qi + 1)]
    n_blocks = len(blocks)
    s_buf, p_buf = {}, {}
    m_run = None
    for t in range(n_blocks + 2):
        if t < n_blocks:
            s_buf[t] = scores(*blocks[t])
        pv = None
        if 0 <= t - 2 < n_blocks:
            p, p_alpha = p_buf.pop(t - 2)
            v_aug = jnp.concatenate([rows(v_ref, blocks[t - 2][1]), ones], axis=1)
            pv = [_dot(x, v_aug) for x in p]
        if 0 <= t - 1 < n_blocks:
            qi, j = blocks[t - 1]
            s = s_buf.pop(t - 1)
            if j == qi:
                s = [jnp.where(col <= row, x, NEG_INF) for x in s]
            row_max = [jnp.max(x, axis=-1, keepdims=True) for x in s]
            if j == 0:
                m_new, alpha = row_max, None
            else:
                m_new = [jnp.maximum(mo, mx) for mo, mx in zip(m_run, row_max)]
                alpha = [jnp.exp2(mo - mn) for mo, mn in zip(m_run, m_new)]
            p_buf[t - 1] = ([jnp.exp2(x - mn).astype(BF16) for x, mn in zip(s, m_new)], alpha)
            m_run = m_new
        if pv is not None:
            qi, j = blocks[t - 2]
            acc = pv if j == 0 else [alpha_c * acc_ref[c] + pv[c] for c, alpha_c in enumerate(p_alpha)]
            if j == qi:
                a0, a1 = acc
                o = a0[:, :LANES] / a0[:, LANES:] - lam * (a1[:, :LANES] / a1[:, LANES:])
                o_ref[qi * tq:(qi + 1) * tq, :] = (_rms_rows(o, sg_ref[...]) * (1.0 - lam_init)).astype(o_ref.dtype)
            else:
                for c in range(2):
                    acc_ref[c] = acc[c]


def _diff_attn_prompt(qb, kb, vb, lp, lam_init, batch, tq=512):
    m = qb.shape[0]
    t = m // batch
    tq = min(tq, t)
    vec = pl.BlockSpec((1, DA_QK), lambda b, h: (0, 0))
    seq_spec = pl.BlockSpec((t, LANES), lambda b, h: (b, h))
    return pl.pallas_call(
        functools.partial(_dattn_kernel, tq=tq, lam_init=lam_init),
        grid=(batch, DA_HEADS),
        in_specs=[vec] * 4 + [pl.BlockSpec((1, DA_V), lambda b, h: (0, 0)), seq_spec, seq_spec, seq_spec],
        out_specs=seq_spec,
        out_shape=jax.ShapeDtypeStruct((m, DA_DIM), BF16),
        scratch_shapes=[pltpu.VMEM((2, tq, 2 * LANES), F32)],
        compiler_params=_cparams(("parallel", "parallel")),
        name="diff_attn_prompt",
    )(*[lp[n].reshape(1, -1) for n in ('lq1', 'lk1', 'lq2', 'lk2')], lp['subln_g'].reshape(1, -1), qb, kb, vb)


def _paged_kernel(pt_ref, lq1_ref, lk1_ref, lq2_ref, lk2_ref, sg_ref, q_ref, kn_ref, vn_ref, *rest, pps, lam_init):
    del pt_ref
    k_refs, v_refs = rest[:pps], rest[pps:2 * pps]
    o_ref, m_ref, l_ref, acc_ref = rest[2 * pps:]
    pg = pl.program_id(1)

    @pl.when(pg == 0)
    def _():
        m_ref[...] = jnp.full_like(m_ref, NEG_INF)
        l_ref[...] = jnp.zeros_like(l_ref)
        acc_ref[...] = jnp.zeros_like(acc_ref)

    q = q_ref[0]
    half_sum = _head_ones()
    swap = lambda x: pltpu.roll(x, DA_QK, 1)

    def update(k_list, v_list):
        n = k_list[0].shape[0]
        rows = n * DA_HEADS
        s = [_dot((k3 * q).reshape(rows, LANES).astype(BF16), half_sum) for k3 in k_list]
        m_old = m_ref[...]
        m_new = m_old
        for x in s:
            m_new = jnp.maximum(m_new, jnp.max(x.reshape(n, DA_HEADS, LANES), axis=0))
        alpha = jnp.exp2(m_old - m_new)
        l_add = acc_a = acc_b = jnp.zeros((DA_HEADS, LANES), F32)
        for x, v3 in zip(s, v_list):
            p = jnp.exp2(x.reshape(n, DA_HEADS, LANES) - m_new)
            p_swapped = swap(p.reshape(rows, LANES)).reshape(n, DA_HEADS, LANES)
            l_add = l_add + jnp.sum(p, axis=0)
            acc_a = acc_a + jnp.sum(p * v3, axis=0)
            acc_b = acc_b + jnp.sum(p_swapped * v3, axis=0)
        l_ref[...] = alpha * l_ref[...] + l_add
        acc_ref[0] = alpha * acc_ref[0] + acc_a
        acc_ref[1] = swap(alpha) * acc_ref[1] + acc_b
        m_ref[...] = m_new

    update([r[0] for r in k_refs], [r[0] for r in v_refs])

    @pl.when(pg == pl.num_programs(1) - 1)
    def _():
        update([kn_ref[...]], [vn_ref[...]])
        lam = _lambda(lq1_ref, lk1_ref, lq2_ref, lk2_ref, lam_init)
        l = l_ref[...]
        straight = acc_ref[0] / l
        crossed = acc_ref[1] / swap(l)
        low = lax.broadcasted_iota(jnp.int32, (1, LANES), 1) < DA_QK
        o = jnp.where(low, straight, crossed) - lam * jnp.where(low, crossed, straight)
        o_ref[0] = (_rms_rows(o, sg_ref[...]) * (1.0 - lam_init)).astype(o_ref.dtype)


def _diff_attn_paged(q, k_new, v_new, cache_k, cache_v, page_table, lp, lam_init):
    b, n_pages = page_table.shape
    page = cache_k.shape[1]
    pps = next(c for c in (8, 4, 2, 1) if n_pages % c == 0)
    vec = pl.BlockSpec((1, DA_QK), lambda i, p, pt: (0, 0))
    rowb = pl.BlockSpec((1, DA_HEADS, LANES), lambda i, p, pt: (i, 0, 0))

    def page_spec(j):
        return pl.BlockSpec((1, page, DA_HEADS, LANES),
                            lambda i, p, pt: (pt[i * n_pages + p * pps + j], 0, 0, 0))

    grid_spec = pltpu.PrefetchScalarGridSpec(
        num_scalar_prefetch=1,
        grid=(b, n_pages // pps),
        in_specs=([vec] * 4 + [pl.BlockSpec((1, DA_V), lambda i, p, pt: (0, 0)), rowb, rowb, rowb]
                  + [page_spec(j) for j in range(pps)] * 2),
        out_specs=rowb,
        scratch_shapes=[pltpu.VMEM((DA_HEADS, LANES), F32)] * 2 + [pltpu.VMEM((2, DA_HEADS, LANES), F32)],
    )
    r3 = lambda x: x.reshape(b, DA_HEADS, LANES)
    out = pl.pallas_call(
        functools.partial(_paged_kernel, pps=pps, lam_init=lam_init),
        grid_spec=grid_spec,
        out_shape=jax.ShapeDtypeStruct((b, DA_HEADS, LANES), BF16),
        compiler_params=_cparams(("parallel", "arbitrary")),
        name="diff_attn_paged",
    )(page_table.reshape(-1), *[lp[n].reshape(1, -1) for n in ('lq1', 'lk1', 'lq2', 'lk2')],
      lp['subln_g'].reshape(1, -1), r3(q), r3(k_new), r3(v_new), *([cache_k] * pps), *([cache_v] * pps))
    return out.reshape(b, DA_DIM)


def _cross_kernel(q_ref, mk_ref, mv_ref, o_ref):
    for h in range(CROSS_HEADS):
        sl = slice(h * CROSS_HD, (h + 1) * CROSS_HD)
        s = _dot_nt(q_ref[0, :, sl], mk_ref[0, :, sl].astype(BF16))
        m = jnp.max(s, axis=-1, keepdims=True)
        p = jnp.exp(s - m)
        l = jnp.sum(p, axis=-1, keepdims=True)
        o_ref[0, :, sl] = (_dot(p.astype(BF16), mv_ref[0, :, sl].astype(BF16)) / l).astype(o_ref.dtype)


def _cross_decode_kernel(q_ref, mk_ref, mv_ref, o_ref):
    q = q_ref[0].astype(F32)
    s = jnp.sum(mk_ref[0] * q, axis=-1, keepdims=True)
    p = jnp.exp(s - jnp.max(s, axis=0, keepdims=True))
    o = jnp.sum(p * mv_ref[0], axis=0) / jnp.sum(p, axis=0)
    o_ref[0] = o.astype(o_ref.dtype)


def _cross_attn_decode(q, mk, mv):
    b = q.shape[0]
    mt = mk.shape[1]
    q_spec = pl.BlockSpec((1, CROSS_HEADS, CROSS_HD), lambda i: (i, 0, 0))
    m_spec = pl.BlockSpec((1, mt, CROSS_HEADS, CROSS_HD), lambda i: (i, 0, 0, 0))
    out = pl.pallas_call(
        _cross_decode_kernel,
        grid=(b,),
        in_specs=[q_spec, m_spec, m_spec],
        out_specs=q_spec,
        out_shape=jax.ShapeDtypeStruct((b, CROSS_HEADS, CROSS_HD), BF16),
        compiler_params=_cparams(("parallel",)),
        name="cross_attn_decode",
    )(q.reshape(b, CROSS_HEADS, CROSS_HD), mk, mv)
    return out.reshape(b, D_MODEL)


def _cross_attn(q, mk, mv, tq=512):
    b, t, _ = q.shape
    mt = mk.shape[1]
    tq = min(tq, t)
    q_spec = pl.BlockSpec((1, tq, D_MODEL), lambda i, j: (i, j, 0))
    m_spec = pl.BlockSpec((1, mt, D_MODEL), lambda i, j: (i, 0, 0))
    return pl.pallas_call(
        _cross_kernel,
        grid=(b, t // tq),
        in_specs=[q_spec, m_spec, m_spec],
        out_specs=q_spec,
        out_shape=jax.ShapeDtypeStruct(q.shape, BF16),
        compiler_params=_cparams(("parallel", "parallel")),
        name="cross_attn",
    )(q, mk, mv)


def _layer(x, batch, lp, wts, lam_init, mk, mv, rwkv_fn, attn_fn):
    m = x.shape[0]
    t = m // batch
    pr = _mm([x], [wts['in_r']], name="in_proj_rwkv", norm_g=lp['norm_mix_g'], tm=512)
    qb, kf, vf, kb, vb = _da_proj(x, lp['norm_mix_g'], wts['in_da'])
    y_r, s_new = rwkv_fn(pr)
    y_d = attn_fn(qb, kf, vf, kb, vb)
    x = _mm([y_r, y_d], [wts['o_r'], wts['o_d']], name="out_proj", res=x, tm=512)
    qc = _mm([x], [wts['cq']], name="cross_q", norm_g=lp['norm_cross_g'], scale=CROSS_HD ** -0.5, out_dtype=BF16,
             tm=1024)
    if mk.ndim == 4:
        oc = _cross_attn_decode(qc, mk, mv)
    else:
        oc = _cross_attn(qc.reshape(batch, t, D_MODEL), mk, mv).reshape(m, D_MODEL)
    x = _mm([oc], [wts['co']], name="cross_out", res=x, tm=512)
    hmid = _mm([x], [wts['gate'], wts['up']], name="ffn_swiglu", norm_g=lp['norm_ffn_g'], swiglu=True,
               out_dtype=BF16, tm=1024, tn=512)
    x = _mm_down(hmid, wts['down'], x, wts['final_g'], wts['is_last'], tm=1024, tk=hmid.shape[1] // 4)
    last_row = pr.reshape(batch, t, RWKV_PROJ)[:, -1]
    return x, kf, vf, s_new, last_row


def kernel(x_prompt, x_sample, cache_k, cache_v, cache_mem_k, cache_mem_v, state_wkv, state_shift, page_table, mem_prompt, norm_mix_g, w_in, tok_shift_mu, rwkv_w0, rwkv_w2, rwkv_a0, rwkv_a2, rwkv_g2, rwkv_k_k, rwkv_k_a, rwkv_r_k, rwkv_lnx_g, rwkv_lnx_b, diff_lq1, diff_lk1, diff_lq2, diff_lk2, diff_subln_g, w_o, norm_cross_g, norm_mem_g, w_cq, w_ck, w_cv, w_co, norm_ffn_g, w_gate, w_up, w_down, final_norm_g):
    b, t, _ = x_prompt.shape
    db, dt, _ = x_sample.shape
    depth = w_in.shape[0]
    n_mem = mem_prompt.shape[1]
    n_pool, page = cache_k.shape[1], cache_k.shape[2]
    xp = x_prompt.reshape(b * t, D_MODEL)
    xs = x_sample.reshape(db * dt, D_MODEL)
    outs = {n: [] for n in ('k_p', 'v_p', 'k_s', 'v_s', 'S_p', 'S_s', 'sh_p', 'sh_s', 'mk_p', 'mv_p')}
    for l in range(depth):
        lam_init = 0.8 - 0.6 * math.exp(-0.3 * l)
        lp = dict(norm_mix_g=norm_mix_g[l], mu=tok_shift_mu[l], w0=rwkv_w0[l], w2=rwkv_w2[l], a0=rwkv_a0[l],
                  a2=rwkv_a2[l], g2=rwkv_g2[l], k_k=rwkv_k_k[l], k_a=rwkv_k_a[l], r_k=rwkv_r_k[l],
                  lnx_g=rwkv_lnx_g[l], lnx_b=rwkv_lnx_b[l], lq1=diff_lq1[l], lk1=diff_lk1[l], lq2=diff_lq2[l],
                  lk2=diff_lk2[l], subln_g=diff_subln_g[l], norm_cross_g=norm_cross_g[l], norm_ffn_g=norm_ffn_g[l])
        wl = w_in[l].astype(BF16)
        wts = dict(in_r=wl[:, :RWKV_PROJ], in_da=wl[:, RWKV_PROJ:], o_r=w_o[l][:RWKV_DIM].astype(BF16),
                   o_d=w_o[l][RWKV_DIM:].astype(BF16), cq=w_cq[l].astype(BF16), co=w_co[l].astype(BF16),
                   gate=w_gate[l].astype(BF16), up=w_up[l].astype(BF16), down=w_down[l].astype(BF16),
                   final_g=final_norm_g, is_last=l == depth - 1)

        mem2 = mem_prompt.reshape(b * n_mem, D_MODEL)
        mk = _mm([mem2], [w_ck[l].astype(BF16)], name="mem_k", norm_g=norm_mem_g[l], tm=1024)
        mv = _mm([mem2], [w_cv[l].astype(BF16)], name="mem_v", norm_g=norm_mem_g[l], tm=1024)

        def rwkv_prompt(pr):
            return _rwkv_chunk_scan(pr, lp, b)

        def attn_prompt(qb, kf, vf, kb, vb):
            return _diff_attn_prompt(qb, kb, vb, lp, lam_init, b)

        xp, kf, vf, s_new, last = _layer(xp, b, lp, wts, lam_init, mk.reshape(b, n_mem, D_MODEL),
                                         mv.reshape(b, n_mem, D_MODEL), rwkv_prompt, attn_prompt)
        outs['k_p'].append(kf.reshape(b, t, DA_HEADS, 2 * DA_QK))
        outs['v_p'].append(vf.reshape(b, t, DA_HEADS, DA_V))
        outs['S_p'].append(s_new)
        outs['sh_p'].append(last)
        outs['mk_p'].append(mk.reshape(b, n_mem, CROSS_HEADS, CROSS_HD))
        outs['mv_p'].append(mv.reshape(b, n_mem, CROSS_HEADS, CROSS_HD))

        assert dt == 1

        def rwkv_sample(pr):
            r, k2, v, kk, a, lw, g = _rwkv_prep(pr, state_shift[l], lp)
            y, s_fin = _rwkv_step(state_wkv[l], r, k2, v, kk, a, lw)
            return _rwkv_post(y, r, k2, v, g, lp), s_fin

        def attn_sample(qb, kf, vf, kb, vb):
            return _diff_attn_paged(qb.astype(F32), kf, vf, cache_k[l], cache_v[l], page_table, lp, lam_init)

        xs, kf, vf, s_new, last = _layer(xs, db, lp, wts, lam_init, cache_mem_k[l], cache_mem_v[l],
                                         rwkv_sample, attn_sample)
        outs['k_s'].append(kf.reshape(db, dt, DA_HEADS, 2 * DA_QK))
        outs['v_s'].append(vf.reshape(db, dt, DA_HEADS, DA_V))
        outs['S_s'].append(s_new)
        outs['sh_s'].append(last)
    y_prompt = xp.reshape(b, t, D_MODEL)
    y_sample = xs.reshape(db, dt, D_MODEL)
    st = lambda n: jnp.stack(outs[n])
    return (y_prompt, y_sample, st('k_p'), st('v_p'), st('k_s'), st('v_s'), st('S_p'), st('S_s'),
            st('sh_p'), st('sh_s'), st('mk_p'), st('mv_p'))
```

```python
import functools
import math

import jax
import jax.numpy as jnp
from jax import lax
from jax.experimental import pallas as pl
from jax.experimental.pallas import tpu as pltpu

F32 = jnp.float32
BF16 = jnp.bfloat16

D_MODEL = 2048
RWKV_DIM = 1024
HEAD = 64
RWKV_HEADS = 16
DECAY_LORA = 64
AAA_LORA = 64
GATE_LORA = 128
RWKV_PROJ = 3 * RWKV_DIM + DECAY_LORA + AAA_LORA + GATE_LORA
DA_DIM = 1024
DA_QK = 64
DA_V = 128
DA_HEADS = 8
CROSS_HEADS = 4
CROSS_HD = D_MODEL // CROSS_HEADS
RMS_EPS = 1e-6
GN_EPS = 64e-5
NEG_INF = -1e30
LOG2_E = 1.4426950408889634

LANES = 128
CHUNK = 64
PAIRS = RWKV_HEADS // 2
VMEM_LIMIT = 56 * 1024 * 1024

_NT = (((1,), (1,)), ((), ()))


def _cparams(sem):
    return pltpu.CompilerParams(dimension_semantics=sem, vmem_limit_bytes=VMEM_LIMIT)


def _dot(a, b):
    return jnp.dot(a, b, preferred_element_type=F32)


def _dot_nt(a, b):
    return lax.dot_general(a, b, _NT, preferred_element_type=F32)


def _rms_rows(x, g):
    x = x.astype(F32)
    ms = jnp.mean(x * x, axis=-1, keepdims=True)
    return x * lax.rsqrt(ms + RMS_EPS) * g


def _sigmoid(x):
    return 1.0 / (1.0 + jnp.exp(-x))


def _split_dot(x, m_bf16, terms, left=False):
    acc = None
    rem = x
    for t in range(terms):
        piece = rem.astype(BF16)
        d = _dot(m_bf16, piece) if left else _dot(piece, m_bf16)
        acc = d if acc is None else acc + d
        if t + 1 < terms:
            rem = rem - piece.astype(F32)
    return acc


def _head_ones():
    i = lax.broadcasted_iota(jnp.int32, (LANES, LANES), 0) // HEAD
    j = lax.broadcasted_iota(jnp.int32, (LANES, LANES), 1) // HEAD
    return jnp.where(i == j, 1.0, 0.0).astype(BF16)


def _mm_kernel(*refs, n_a, has_norm, swiglu, has_res, scale):
    refs = list(refs)
    a_refs = [refs.pop(0) for _ in range(n_a)]
    g_ref = refs.pop(0) if has_norm else None
    w_refs = [refs.pop(0) for _ in range(2 if swiglu else n_a)]
    res_ref = refs.pop(0) if has_res else None
    o_ref = refs.pop(0)
    if has_norm:
        xn_ref = refs.pop(0)

        @pl.when(pl.program_id(1) == 0)
        def _():
            xn_ref[...] = _rms_rows(a_refs[0][...], g_ref[...]).astype(BF16)

        lhs = [xn_ref[...]]
    else:
        lhs = [r[...] for r in a_refs]
    if swiglu:
        gate = _dot(lhs[0], w_refs[0][...])
        up = _dot(lhs[0], w_refs[1][...])
        acc = gate * _sigmoid(gate) * up
    else:
        acc = _dot(lhs[0], w_refs[0][...])
        for a, w in zip(lhs[1:], w_refs[1:]):
            acc = acc + _dot(a, w[...])
    if scale != 1.0:
        acc = acc * scale
    if has_res:
        acc = acc + res_ref[...]
    o_ref[...] = acc.astype(o_ref.dtype)


def _mm(a_list, w_list, *, name, norm_g=None, swiglu=False, res=None, scale=1.0, out_dtype=F32, tm=512, tn=None):
    m = a_list[0].shape[0]
    n = w_list[0].shape[1]
    tm = min(tm, m)
    resident = tn is None
    tn = n if resident else min(tn, n)
    assert m % tm == 0 and n % tn == 0
    has_norm = norm_g is not None
    in_specs = [pl.BlockSpec((tm, a.shape[1]), lambda i, j: (i, 0)) for a in a_list]
    args = list(a_list)
    if has_norm:
        in_specs.append(pl.BlockSpec((1, a_list[0].shape[1]), lambda i, j: (0, 0)))
        args.append(norm_g.reshape(1, -1))
    for w in w_list:
        if resident:
            in_specs.append(pl.BlockSpec((w.shape[0], n), lambda i, j: (0, 0), pipeline_mode=pl.Buffered(1)))
        else:
            in_specs.append(pl.BlockSpec((w.shape[0], tn), lambda i, j: (0, j)))
        args.append(w)
    if res is not None:
        in_specs.append(pl.BlockSpec((tm, tn), lambda i, j: (i, j)))
        args.append(res)
    scratch = [pltpu.VMEM((tm, a_list[0].shape[1]), BF16)] if has_norm else []
    return pl.pallas_call(
        functools.partial(_mm_kernel, n_a=len(a_list), has_norm=has_norm, swiglu=swiglu,
                          has_res=res is not None, scale=scale),
        grid=(m // tm, n // tn),
        in_specs=in_specs,
        out_specs=pl.BlockSpec((tm, tn), lambda i, j: (i, j)),
        out_shape=jax.ShapeDtypeStruct((m, n), out_dtype),
        scratch_shapes=scratch,
        compiler_params=_cparams(("parallel", "arbitrary")),
        name=name,
    )(*args)


def _da_proj_kernel(x_ref, g_ref, w_ref, q_ref, kf_ref, vf_ref, kb_ref, vb_ref):
    xn = _rms_rows(x_ref[...], g_ref[...]).astype(BF16)
    q_ref[...] = (_dot(xn, w_ref[:, :DA_DIM]) * (DA_QK ** -0.5 * LOG2_E)).astype(BF16)
    k = _dot(xn, w_ref[:, DA_DIM:2 * DA_DIM])
    kf_ref[...] = k
    kb_ref[...] = k.astype(BF16)
    v = _dot(xn, w_ref[:, 2 * DA_DIM:])
    vf_ref[...] = v
    vb_ref[...] = v.astype(BF16)


def _da_proj(x, norm_g, w_da, tm=512):
    m, k = x.shape
    tm = min(tm, m)
    blk = pl.BlockSpec((tm, DA_DIM), lambda i: (i, 0))
    return pl.pallas_call(
        _da_proj_kernel,
        grid=(m // tm,),
        in_specs=[pl.BlockSpec((tm, k), lambda i: (i, 0)),
                  pl.BlockSpec((1, k), lambda i: (0, 0)),
                  pl.BlockSpec((k, 3 * DA_DIM), lambda i: (0, 0), pipeline_mode=pl.Buffered(1))],
        out_specs=[blk] * 5,
        out_shape=[jax.ShapeDtypeStruct((m, DA_DIM), dt) for dt in (BF16, F32, F32, BF16, BF16)],
        compiler_params=_cparams(("parallel",)),
        name="da_proj",
    )(x, norm_g.reshape(1, -1), w_da)


def _mm_down_kernel(a_ref, w_ref, res_ref, g_ref, o_ref, *, final_norm):
    kk = pl.program_id(1)

    @pl.when(kk == 0)
    def _():
        o_ref[...] = res_ref[...]

    o_ref[...] += _dot(a_ref[...], w_ref[...])

    if final_norm:
        @pl.when(kk == pl.num_programs(1) - 1)
        def _():
            o_ref[...] = _rms_rows(o_ref[...], g_ref[...])


def _mm_down(a, w, res, final_g, final_norm, tm=512, tk=512):
    m, k = a.shape
    n = w.shape[1]
    tm = min(tm, m)
    assert m % tm == 0 and k % tk == 0
    return pl.pallas_call(
        functools.partial(_mm_down_kernel, final_norm=final_norm),
        grid=(m // tm, k // tk),
        in_specs=[pl.BlockSpec((tm, tk), lambda i, kk: (i, kk)),
                  pl.BlockSpec((tk, n), lambda i, kk: (kk, 0)),
                  pl.BlockSpec((tm, n), lambda i, kk: (i, 0)),
                  pl.BlockSpec((1, n), lambda i, kk: (0, 0))],
        out_specs=pl.BlockSpec((tm, n), lambda i, kk: (i, 0)),
        out_shape=jax.ShapeDtypeStruct((m, n), F32),
        compiler_params=_cparams(("parallel", "arbitrary")),
        name="ffn_down",
    )(a, w, res, final_g.reshape(1, -1))


def _head_sum(x):
    lo = lax.broadcasted_iota(jnp.int32, (1, LANES), 1) < HEAD
    s_lo = jnp.sum(jnp.where(lo, x, 0.0), axis=-1, keepdims=True)
    s_all = jnp.sum(x, axis=-1, keepdims=True)
    return jnp.where(lo, s_lo, s_all - s_lo)


def _token_mix(p, p_prev, mu, w0, w2, a0, a2, g2, k_k, k_a):
    ps = p + (p_prev - p) * mu
    o1, o2, o3 = RWKV_DIM, 2 * RWKV_DIM, 3 * RWKV_DIM
    o4 = o3 + DECAY_LORA
    o5 = o4 + AAA_LORA
    r, k, v = ps[:, :o1], ps[:, o1:o2], ps[:, o2:o3]
    wd, ad, gd = ps[:, o3:o4], ps[:, o4:o5], ps[:, o5:]
    z = w0 + _dot(jnp.tanh(wd).astype(BF16), w2)
    w_log = -(jnp.maximum(-z, 0.0) + jnp.log(1.0 + jnp.exp(-jnp.abs(z)))) - 0.5
    lw = -jnp.exp(w_log)
    a = _sigmoid(a0 + _dot(ad.astype(BF16), a2))
    g = _dot(_sigmoid(gd).astype(BF16), g2)
    kraw = k * k_k
    kk = []
    for pr in range(PAIRS):
        kp = kraw[:, pr * LANES:(pr + 1) * LANES]
        kk.append(kp * lax.rsqrt(jnp.maximum(_head_sum(kp * kp), 1e-24)))
    return r, k * (1.0 + (a - 1.0) * k_a), v, kk, a, lw, g


def _mix_out(y, r, k2, v, g, r_k, ln_g, ln_b):
    mean = _head_sum(y) * (1.0 / HEAD)
    d = y - mean
    var = _head_sum(d * d) * (1.0 / HEAD)
    yn = d * lax.rsqrt(var + GN_EPS) * ln_g + ln_b
    return (yn + _head_sum(r * k2 * r_k) * v) * g


def _prep_kernel(p_ref, prev_ref, mu_ref, w0_ref, w2_ref, a0_ref, a2_ref, g2_ref, kk_ref, ka_ref,
                 r_o, k_o, v_o, kk_o, a_o, lw_o, g_o):
    r, k2, v, kk, a, lw, g = _token_mix(p_ref[...], prev_ref[...], mu_ref[...], w0_ref[...], w2_ref[...],
                                        a0_ref[...], a2_ref[...], g2_ref[...], kk_ref[...], ka_ref[...])
    for pr in range(PAIRS):
        kk_o[:, pr * LANES:(pr + 1) * LANES] = kk[pr]
    r_o[...] = r
    k_o[...] = k2
    v_o[...] = v
    a_o[...] = a
    lw_o[...] = lw
    g_o[...] = g


def _mix_params(lp):
    row = lambda x: x.reshape(1, -1)
    return (row(lp['mu']), row(lp['w0']), lp['w2'].astype(BF16), row(lp['a0']), lp['a2'].astype(BF16),
            lp['g2'].astype(BF16), row(lp['k_k']), row(lp['k_a']))


def _mix_param_specs(index_map):
    shapes = [(1, RWKV_PROJ), (1, RWKV_DIM), (DECAY_LORA, RWKV_DIM), (1, RWKV_DIM), (AAA_LORA, RWKV_DIM),
              (GATE_LORA, RWKV_DIM), (1, RWKV_DIM), (1, RWKV_DIM)]
    return [pl.BlockSpec(s, index_map) for s in shapes]


def _rwkv_prep(p, prev, lp):
    m = p.shape[0]
    blk_in = pl.BlockSpec((m, RWKV_PROJ), lambda i: (0, 0))
    out_blk = pl.BlockSpec((m, RWKV_DIM), lambda i: (0, 0))
    return pl.pallas_call(
        _prep_kernel,
        grid=(1,),
        in_specs=[blk_in, blk_in] + _mix_param_specs(lambda i: (0, 0)),
        out_specs=[out_blk] * 7,
        out_shape=[jax.ShapeDtypeStruct((m, RWKV_DIM), F32)] * 7,
        compiler_params=_cparams(("arbitrary",)),
        name="rwkv_prep",
    )(p, prev, *_mix_params(lp))


def _chunk_kernel(p_ref, mu_ref, w0_ref, w2_ref, a0_ref, a2_ref, g2_ref, kk_ref, ka_ref, rk_ref, lg_ref, lb_ref,
                  y_ref, s_out_ref, s_ref, prev_ref):
    c = pl.program_id(1)

    @pl.when(c == 0)
    def _():
        s_ref[...] = jnp.zeros_like(s_ref)
        prev_ref[...] = jnp.zeros_like(prev_ref)

    p = p_ref[...]
    rows = p.shape[0]
    n_chunks = rows // CHUNK
    first_row = lax.broadcasted_iota(jnp.int32, p.shape, 0) == 0
    p_prev = jnp.where(first_row, prev_ref[...], pltpu.roll(p, 1, 0))
    prev_ref[...] = p[rows - 1:rows, :]
    r_all, k_all, v_all, kk_all, a_all, lw_all, g_all = _token_mix(
        p, p_prev, mu_ref[...], w0_ref[...], w2_ref[...], a0_ref[...], a2_ref[...], g2_ref[...], kk_ref[...],
        ka_ref[...])

    lane = lax.broadcasted_iota(jnp.int32, (1, LANES), 1)
    m0 = jnp.where(lane < HEAD, 1.0, 0.0)
    m1 = 1.0 - m0
    ii = lax.broadcasted_iota(jnp.int32, (LANES, LANES), 0)
    jj = lax.broadcasted_iota(jnp.int32, (LANES, LANES), 1)
    strict = ii > jj
    incl = ii >= jj
    eye = jnp.where(ii == jj, 1.0, 0.0)
    ti = lax.broadcasted_iota(jnp.int32, (CHUNK, CHUNK), 0)
    tj = lax.broadcasted_iota(jnp.int32, (CHUNK, CHUNK), 1)
    tril = jnp.where(ti >= tj, 1.0, 0.0).astype(BF16)

    def stack(x):
        return jnp.concatenate([x * m0, x * m1], axis=0)

    pairs = range(PAIRS)
    items = [(ci, pr) for ci in range(n_chunks) for pr in pairs]
    tok = [slice(ci * CHUNK, (ci + 1) * CHUNK) for ci, _ in items]
    sls = [slice(pr * LANES, (pr + 1) * LANES) for _, pr in items]
    each = lambda fn, *cols: [fn(*xs) for xs in zip(*cols)]
    cut = lambda x: [x[t, sl] for t, sl in zip(tok, sls)]
    lw, k2, r, v = cut(lw_all), cut(k_all), cut(r_all), cut(v_all)
    kk = [kk_all[pr][t, :] for (_, pr), t in zip(items, tok)]
    bb = each(lambda x, y: x * y, kk, cut(a_all))
    cum_all = [_split_dot(lw_all[ci * CHUNK:(ci + 1) * CHUNK, :], tril, 3, left=True) for ci in range(n_chunks)]
    cum = [cum_all[ci][:, sl] for (ci, _), sl in zip(items, sls)]
    tot = [x[CHUNK - 1:CHUNK, :] for x in cum]
    inv_g = [jnp.exp(-x) for x in cum]
    to_end = each(lambda t, x: jnp.exp(t - x), tot, cum)
    a_t = each(lambda x, cm, l: stack(-x * jnp.exp(cm - l)).astype(BF16), kk, cum, lw)
    b_t = each(lambda x, g: stack(x * g).astype(BF16), bb, inv_g)
    k_t = each(lambda x, g: stack(x * g).astype(BF16), k2, inv_g)
    r_f = each(lambda x, cm: stack(x * jnp.exp(cm)), r, cum)
    r_t = [x.astype(BF16) for x in r_f]
    v_f = [stack(x) for x in v]
    v_s = [x.astype(BF16) for x in v_f]
    bk_end = each(lambda x, y, e: jnp.concatenate([stack(x * e), stack(y * e)], axis=0).astype(BF16), bb, k2, to_end)

    a_ab = each(lambda x, y: jnp.where(strict, _dot_nt(x, y), 0.0), a_t, b_t)
    a_ak = each(lambda x, y: jnp.where(strict, _dot_nt(x, y), 0.0).astype(BF16), a_t, k_t)
    a_rb = each(lambda x, y: jnp.where(incl, _dot_nt(x, y), 0.0).astype(BF16), r_t, b_t)
    a_rk = each(lambda x, y: jnp.where(incl, _dot_nt(x, y), 0.0).astype(BF16), r_t, k_t)
    inv = [eye + x for x in a_ab]
    power = [x.astype(BF16) for x in a_ab]
    for _ in range(5):
        power = [_dot(x, x).astype(BF16) for x in power]
        inv = each(lambda x, p: x + _dot(x.astype(BF16), p), inv, power)
    w1 = each(lambda x, y: _dot(x, y).astype(BF16), a_ak, v_s)
    pu = each(lambda x, y, z: _dot(x.astype(BF16), jnp.concatenate([y, z], axis=1)), inv, a_t, w1)
    pu_b = [x.astype(BF16) for x in pu]
    qy = each(_dot, a_rb, pu_b)
    q_m = each(lambda x, y: (x + y[:, :LANES]).astype(BF16), r_f, qy)
    y_v = each(lambda x, y, z: x[:, LANES:] + _dot(y, z), qy, a_rk, v_s)
    pq = each(lambda x, y: jnp.concatenate([x[:, :LANES], y], axis=0), pu_b, q_m)
    state = [s_ref[pr] for pr in pairs]
    g_cut = cut(g_all)
    for ci in range(n_chunks):
        grp = slice(ci * PAIRS, (ci + 1) * PAIRS)
        uy = each(lambda x, s: _dot_nt(x, s.astype(BF16)), pq[grp], state)
        uv_t = each(lambda x, y, z: jnp.concatenate([x[:LANES] + y[:, LANES:], z], axis=0).T.astype(BF16),
                    uy, pu[grp], v_f[grp])
        state = each(lambda s, t, x, y: s * jnp.exp(t) + _dot(x, y), state, tot[grp], uv_t, bk_end[grp])
        for pr in pairs:
            i = ci * PAIRS + pr
            y_m = uy[pr][LANES:] + y_v[i]
            out = _mix_out(y_m[:CHUNK] + y_m[CHUNK:], r[i], k2[i], v[i], g_cut[i], rk_ref[:, sls[i]],
                           lg_ref[:, sls[i]], lb_ref[:, sls[i]])
            y_ref[tok[i], sls[i]] = out.astype(y_ref.dtype)
    for pr in pairs:
        s_ref[pr] = state[pr]

    @pl.when(c == pl.num_programs(1) - 1)
    def _():
        s_out_ref[0] = s_ref[...]


def _rwkv_chunk_scan(p, lp, batch):
    m = p.shape[0]
    n_chunks = m // batch // CHUNK
    cps = 2 if n_chunks % 2 == 0 else 1
    nc = n_chunks // cps
    row = lambda x: x.reshape(1, -1)
    const = lambda b, c: (0, 0)
    vec = pl.BlockSpec((1, RWKV_DIM), const)
    y, s_pairs = pl.pallas_call(
        _chunk_kernel,
        grid=(batch, nc),
        in_specs=([pl.BlockSpec((cps * CHUNK, RWKV_PROJ), lambda b, c: (b * nc + c, 0))] + _mix_param_specs(const)
                  + [vec] * 3),
        out_specs=[pl.BlockSpec((cps * CHUNK, RWKV_DIM), lambda b, c: (b * nc + c, 0)),
                   pl.BlockSpec((1, PAIRS, LANES, LANES), lambda b, c: (b, 0, 0, 0))],
        out_shape=[jax.ShapeDtypeStruct((m, RWKV_DIM), BF16),
                   jax.ShapeDtypeStruct((batch, PAIRS, LANES, LANES), F32)],
        scratch_shapes=[pltpu.VMEM((PAIRS, LANES, LANES), F32), pltpu.VMEM((1, RWKV_PROJ), F32)],
        compiler_params=_cparams(("parallel", "arbitrary")),
        name="rwkv_chunk",
    )(p, *_mix_params(lp), row(lp['r_k']), row(lp['lnx_g']), row(lp['lnx_b']))
    s_fin = jnp.stack([s_pairs[:, :, :HEAD, :HEAD], s_pairs[:, :, HEAD:, HEAD:]], axis=2)
    return y, s_fin.reshape(batch, RWKV_HEADS, HEAD, HEAD)


def _step_kernel(s_ref, r_ref, k_ref, kk_ref, a_ref, lw_ref, v_ref, s_o, y_o):
    s = s_ref[0]
    kk = kk_ref[0]
    sa = jnp.sum(s * (-kk), axis=-1, keepdims=True)
    s_new = s * jnp.exp(lw_ref[0]) + sa * (kk * a_ref[0]) + v_ref[0] * k_ref[0]
    s_o[0] = s_new
    y_o[0] = jnp.sum(s_new * r_ref[0], axis=-1, keepdims=True)


def _rwkv_step(s0, r, k2, v, kk, a, lw):
    b = s0.shape[0]
    rowv = lambda x: x.reshape(b, RWKV_HEADS, 1, HEAD)
    row_spec = pl.BlockSpec((1, RWKV_HEADS, 1, HEAD), lambda i: (i, 0, 0, 0))
    col_spec = pl.BlockSpec((1, RWKV_HEADS, HEAD, 1), lambda i: (i, 0, 0, 0))
    s_spec = pl.BlockSpec((1, RWKV_HEADS, HEAD, HEAD), lambda i: (i, 0, 0, 0))
    s_new, y = pl.pallas_call(
        _step_kernel,
        grid=(b,),
        in_specs=[s_spec] + [row_spec] * 5 + [col_spec],
        out_specs=[s_spec, col_spec],
        out_shape=[jax.ShapeDtypeStruct(s0.shape, F32), jax.ShapeDtypeStruct((b, RWKV_HEADS, HEAD, 1), F32)],
        compiler_params=_cparams(("parallel",)),
        name="rwkv_step",
    )(s0, rowv(r), rowv(k2), rowv(kk), rowv(a), rowv(lw), v.reshape(b, RWKV_HEADS, HEAD, 1))
    return y.reshape(b, RWKV_DIM), s_new


def _post_kernel(y_ref, r_ref, k_ref, v_ref, g_ref, rk_ref, lg_ref, lb_ref, o_ref):
    for pr in range(PAIRS):
        sl = slice(pr * LANES, (pr + 1) * LANES)
        out = _mix_out(y_ref[:, sl], r_ref[:, sl], k_ref[:, sl], v_ref[:, sl], g_ref[:, sl], rk_ref[:, sl],
                       lg_ref[:, sl], lb_ref[:, sl])
        o_ref[:, sl] = out.astype(o_ref.dtype)


def _rwkv_post(y, r, k2, v, g, lp):
    m = y.shape[0]
    blk = pl.BlockSpec((m, RWKV_DIM), lambda i: (0, 0))
    par = pl.BlockSpec((1, RWKV_DIM), lambda i: (0, 0))
    return pl.pallas_call(
        _post_kernel,
        grid=(1,),
        in_specs=[blk] * 5 + [par] * 3,
        out_specs=blk,
        out_shape=jax.ShapeDtypeStruct((m, RWKV_DIM), BF16),
        compiler_params=_cparams(("arbitrary",)),
        name="rwkv_post",
    )(y, r, k2, v, g, lp['r_k'].reshape(1, -1), lp['lnx_g'].reshape(1, -1), lp['lnx_b'].reshape(1, -1))


def _lambda(lq1_ref, lk1_ref, lq2_ref, lk2_ref, lam_init):
    s1 = jnp.sum(lq1_ref[...] * lk1_ref[...], axis=-1, keepdims=True)
    s2 = jnp.sum(lq2_ref[...] * lk2_ref[...], axis=-1, keepdims=True)
    return jnp.exp(s1) - jnp.exp(s2) + lam_init


def _dattn_kernel(lq1_ref, lk1_ref, lq2_ref, lk2_ref, sg_ref, q_ref, k_ref, v_ref, o_ref,
                  acc_ref, *, tq, lam_init):
    nq = q_ref.shape[0] // tq
    lane = lax.broadcasted_iota(jnp.int32, (1, LANES), 1)
    row = lax.broadcasted_iota(jnp.int32, (tq, tq), 0)
    col = lax.broadcasted_iota(jnp.int32, (tq, tq), 1)
    ones = jnp.ones((tq, LANES), BF16)
    lam = _lambda(lq1_ref, lk1_ref, lq2_ref, lk2_ref, lam_init)
    rows = lambda ref, i: ref[i * tq:(i + 1) * tq, :]

    def scores(qi, j):
        q = rows(q_ref, qi)
        zero = jnp.zeros_like(q)
        ks = rows(k_ref, j)
        return [_dot_nt(jnp.where(lane < DA_QK, q, zero), ks), _dot_nt(jnp.where(lane >= DA_QK, q, zero), ks)]

    blocks = [(qi, j) for qi in range(nq) for j in range(qi + 1)]
    n_blocks = len(blocks)
    s_buf, p_buf = {}, {}
    m_run = None
    for t in range(n_blocks + 2):
        if t < n_blocks:
            s_buf[t] = scores(*blocks[t])
        pv = None
        if 0 <= t - 2 < n_blocks:
            p, p_alpha = p_buf.pop(t - 2)
            v_aug = jnp.concatenate([rows(v_ref, blocks[t - 2][1]), ones], axis=1)
            pv = [_dot(x, v_aug) for x in p]
        if 0 <= t - 1 < n_blocks:
            qi, j = blocks[t - 1]
            s = s_buf.pop(t - 1)
            if j == qi:
                s = [jnp.where(col <= row, x, NEG_INF) for x in s]
            row_max = [jnp.max(x, axis=-1, keepdims=True) for x in s]
            if j == 0:
                m_new, alpha = row_max, None
            else:
                m_new = [jnp.maximum(mo, mx) for mo, mx in zip(m_run, row_max)]
                alpha = [jnp.exp2(mo - mn) for mo, mn in zip(m_run, m_new)]
            p_buf[t - 1] = ([jnp.exp2(x - mn).astype(BF16) for x, mn in zip(s, m_new)], alpha)
            m_run = m_new
        if pv is not None:
            qi, j = blocks[t - 2]
            acc = pv if j == 0 else [alpha_c * acc_ref[c] + pv[c] for c, alpha_c in enumerate(p_alpha)]
            if j == qi:
                a0, a1 = acc
                o = a0[:, :LANES] / a0[:, LANES:] - lam * (a1[:, :LANES] / a1[:, LANES:])
                o_ref[qi * tq:(qi + 1) * tq, :] = (_rms_rows(o, sg_ref[...]) * (1.0 - lam_init)).astype(o_ref.dtype)
            else:
                for c in range(2):
                    acc_ref[c] = acc[c]


def _diff_attn_prompt(qb, kb, vb, lp, lam_init, batch, tq=512):
    m = qb.shape[0]
    t = m // batch
    tq = min(tq, t)
    vec = pl.BlockSpec((1, DA_QK), lambda b, h: (0, 0))
    seq_spec = pl.BlockSpec((t, LANES), lambda b, h: (b, h))
    return pl.pallas_call(
        functools.partial(_dattn_kernel, tq=tq, lam_init=lam_init),
        grid=(batch, DA_HEADS),
        in_specs=[vec] * 4 + [pl.BlockSpec((1, DA_V), lambda b, h: (0, 0)), seq_spec, seq_spec, seq_spec],
        out_specs=seq_spec,
        out_shape=jax.ShapeDtypeStruct((m, DA_DIM), BF16),
        scratch_shapes=[pltpu.VMEM((2, tq, 2 * LANES), F32)],
        compiler_params=_cparams(("parallel", "parallel")),
        name="diff_attn_prompt",
    )(*[lp[n].reshape(1, -1) for n in ('lq1', 'lk1', 'lq2', 'lk2')], lp['subln_g'].reshape(1, -1), qb, kb, vb)


def _paged_kernel(pt_ref, lq1_ref, lk1_ref, lq2_ref, lk2_ref, sg_ref, q_ref, kn_ref, vn_ref, *rest, pps, lam_init):
    del pt_ref
    k_refs, v_refs = rest[:pps], rest[pps:2 * pps]
    o_ref, m_ref, l_ref, acc_ref = rest[2 * pps:]
    pg = pl.program_id(1)
    n_rows = 2 * DA_HEADS

    @pl.when(pg == 0)
    def _():
        m_ref[...] = jnp.full_like(m_ref, NEG_INF)
        l_ref[...] = jnp.zeros_like(l_ref)
        acc_ref[...] = jnp.zeros_like(acc_ref)

    q = q_ref[0]
    lane = lax.broadcasted_iota(jnp.int32, (1, LANES), 1)
    q_rows = jnp.concatenate([jnp.where(lane < DA_QK, q, 0.0), jnp.where(lane >= DA_QK, q, 0.0)], axis=0)
    q_bf = q_rows.astype(BF16)
    cols = k_refs[0].shape[1] * DA_HEADS
    same_head = ((lax.broadcasted_iota(jnp.int32, (n_rows, cols), 1) & (DA_HEADS - 1))
                 == (lax.broadcasted_iota(jnp.int32, (n_rows, cols), 0) & (DA_HEADS - 1)))
    flat = lambda ref: ref[0].reshape(cols, LANES).astype(BF16)

    m_run, l_run, acc = m_ref[...], l_ref[...], acc_ref[...]
    s_buf, p_buf = {}, {}
    for t in range(pps + 2):
        if t < pps:
            s_buf[t] = _dot_nt(q_bf, flat(k_refs[t]))
        pv = None
        if 0 <= t - 2 < pps:
            p, p_alpha = p_buf.pop(t - 2)
            pv = _dot(p, flat(v_refs[t - 2]))
        if 0 <= t - 1 < pps:
            s = jnp.where(same_head, s_buf.pop(t - 1), NEG_INF)
            m_new = jnp.maximum(m_run, jnp.max(s, axis=-1, keepdims=True))
            alpha = jnp.exp2(m_run - m_new)
            p = jnp.exp2(s - m_new)
            l_run = alpha * l_run + jnp.sum(p, axis=-1, keepdims=True)
            p_buf[t - 1] = (p.astype(BF16), alpha)
            m_run = m_new
        if pv is not None:
            acc = p_alpha * acc + pv
    m_ref[...] = m_run
    l_ref[...] = l_run
    acc_ref[...] = acc

    @pl.when(pg == pl.num_programs(1) - 1)
    def _():
        k_new = jnp.concatenate([kn_ref[0]] * 2, axis=0)
        v_new = jnp.concatenate([vn_ref[0]] * 2, axis=0)
        s_new = jnp.sum(q_rows * k_new, axis=-1, keepdims=True)
        m_fin = jnp.maximum(m_run, s_new)
        alpha = jnp.exp2(m_run - m_fin)
        p_new = jnp.exp2(s_new - m_fin)
        out = (alpha * acc + p_new * v_new) / (alpha * l_run + p_new)
        lam = _lambda(lq1_ref, lk1_ref, lq2_ref, lk2_ref, lam_init)
        o = out[:DA_HEADS] - lam * out[DA_HEADS:]
        o_ref[0] = (_rms_rows(o, sg_ref[...]) * (1.0 - lam_init)).astype(o_ref.dtype)


def _diff_attn_paged(q, k_new, v_new, cache_k, cache_v, page_table, lp, lam_init):
    b, n_pages = page_table.shape
    page = cache_k.shape[1]
    pps = next(c for c in (16, 8, 4, 2, 1) if n_pages % c == 0)
    vec = pl.BlockSpec((1, DA_QK), lambda i, p, pt: (0, 0))
    rowb = pl.BlockSpec((1, DA_HEADS, LANES), lambda i, p, pt: (i, 0, 0))

    def page_spec(j):
        return pl.BlockSpec((1, page, DA_HEADS, LANES),
                            lambda i, p, pt: (pt[i * n_pages + p * pps + j], 0, 0, 0))

    grid_spec = pltpu.PrefetchScalarGridSpec(
        num_scalar_prefetch=1,
        grid=(b, n_pages // pps),
        in_specs=([vec] * 4 + [pl.BlockSpec((1, DA_V), lambda i, p, pt: (0, 0)), rowb, rowb, rowb]
                  + [page_spec(j) for j in range(pps)] * 2),
        out_specs=rowb,
        scratch_shapes=[pltpu.VMEM((2 * DA_HEADS, 1), F32)] * 2 + [pltpu.VMEM((2 * DA_HEADS, LANES), F32)],
    )
    r3 = lambda x: x.reshape(b, DA_HEADS, LANES)
    out = pl.pallas_call(
        functools.partial(_paged_kernel, pps=pps, lam_init=lam_init),
        grid_spec=grid_spec,
        out_shape=jax.ShapeDtypeStruct((b, DA_HEADS, LANES), BF16),
        compiler_params=_cparams(("parallel", "arbitrary")),
        name="diff_attn_paged",
    )(page_table.reshape(-1), *[lp[n].reshape(1, -1) for n in ('lq1', 'lk1', 'lq2', 'lk2')],
      lp['subln_g'].reshape(1, -1), r3(q), r3(k_new), r3(v_new), *([cache_k] * pps), *([cache_v] * pps))
    return out.reshape(b, DA_DIM)


def _cross_kernel(q_ref, mk_ref, mv_ref, o_ref):
    for h in range(CROSS_HEADS):
        sl = slice(h * CROSS_HD, (h + 1) * CROSS_HD)
        s = _dot_nt(q_ref[0, :, sl], mk_ref[0, :, sl].astype(BF16))
        m = jnp.max(s, axis=-1, keepdims=True)
        p = jnp.exp(s - m)
        l = jnp.sum(p, axis=-1, keepdims=True)
        o_ref[0, :, sl] = (_dot(p.astype(BF16), mv_ref[0, :, sl].astype(BF16)) / l).astype(o_ref.dtype)


def _cross_decode_kernel(q_ref, mk_ref, mv_ref, o_ref):
    q = q_ref[0].astype(F32)
    s = jnp.sum(mk_ref[0] * q, axis=-1, keepdims=True)
    p = jnp.exp(s - jnp.max(s, axis=0, keepdims=True))
    o = jnp.sum(p * mv_ref[0], axis=0) / jnp.sum(p, axis=0)
    o_ref[0] = o.astype(o_ref.dtype)


def _cross_attn_decode(q, mk, mv):
    b = q.shape[0]
    mt = mk.shape[1]
    q_spec = pl.BlockSpec((1, CROSS_HEADS, CROSS_HD), lambda i: (i, 0, 0))
    m_spec = pl.BlockSpec((1, mt, CROSS_HEADS, CROSS_HD), lambda i: (i, 0, 0, 0))
    out = pl.pallas_call(
        _cross_decode_kernel,
        grid=(b,),
        in_specs=[q_spec, m_spec, m_spec],
        out_specs=q_spec,
        out_shape=jax.ShapeDtypeStruct((b, CROSS_HEADS, CROSS_HD), BF16),
        compiler_params=_cparams(("parallel",)),
        name="cross_attn_decode",
    )(q.reshape(b, CROSS_HEADS, CROSS_HD), mk, mv)
    return out.reshape(b, D_MODEL)


def _cross_attn(q, mk, mv, tq=512):
    b, t, _ = q.shape
    mt = mk.shape[1]
    tq = min(tq, t)
    q_spec = pl.BlockSpec((1, tq, D_MODEL), lambda i, j: (i, j, 0))
    m_spec = pl.BlockSpec((1, mt, D_MODEL), lambda i, j: (i, 0, 0))
    return pl.pallas_call(
        _cross_kernel,
        grid=(b, t // tq),
        in_specs=[q_spec, m_spec, m_spec],
        out_specs=q_spec,
        out_shape=jax.ShapeDtypeStruct(q.shape, BF16),
        compiler_params=_cparams(("parallel", "parallel")),
        name="cross_attn",
    )(q, mk, mv)


def _layer(x, batch, lp, wts, lam_init, mk, mv, rwkv_fn, attn_fn):
    m = x.shape[0]
    t = m // batch
    pr = _mm([x], [wts['in_r']], name="in_proj_rwkv", norm_g=lp['norm_mix_g'], tm=512)
    qb, kf, vf, kb, vb = _da_proj(x, lp['norm_mix_g'], wts['in_da'])
    y_r, s_new = rwkv_fn(pr)
    y_d = attn_fn(qb, kf, vf, kb, vb)
    x = _mm([y_r, y_d], [wts['o_r'], wts['o_d']], name="out_proj", res=x, tm=512)
    qc = _mm([x], [wts['cq']], name="cross_q", norm_g=lp['norm_cross_g'], scale=CROSS_HD ** -0.5, out_dtype=BF16,
             tm=1024)
    if mk.ndim == 4:
        oc = _cross_attn_decode(qc, mk, mv)
    else:
        oc = _cross_attn(qc.reshape(batch, t, D_MODEL), mk, mv).reshape(m, D_MODEL)
    x = _mm([oc], [wts['co']], name="cross_out", res=x, tm=512)
    hmid = _mm([x], [wts['gate'], wts['up']], name="ffn_swiglu", norm_g=lp['norm_ffn_g'], swiglu=True,
               out_dtype=BF16, tm=1024, tn=512)
    x = _mm_down(hmid, wts['down'], x, wts['final_g'], wts['is_last'], tm=1024, tk=hmid.shape[1] // 4)
    last_row = pr.reshape(batch, t, RWKV_PROJ)[:, -1]
    return x, kf, vf, s_new, last_row


def kernel(x_prompt, x_sample, cache_k, cache_v, cache_mem_k, cache_mem_v, state_wkv, state_shift, page_table, mem_prompt, norm_mix_g, w_in, tok_shift_mu, rwkv_w0, rwkv_w2, rwkv_a0, rwkv_a2, rwkv_g2, rwkv_k_k, rwkv_k_a, rwkv_r_k, rwkv_lnx_g, rwkv_lnx_b, diff_lq1, diff_lk1, diff_lq2, diff_lk2, diff_subln_g, w_o, norm_cross_g, norm_mem_g, w_cq, w_ck, w_cv, w_co, norm_ffn_g, w_gate, w_up, w_down, final_norm_g):
    b, t, _ = x_prompt.shape
    db, dt, _ = x_sample.shape
    depth = w_in.shape[0]
    n_mem = mem_prompt.shape[1]
    n_pool, page = cache_k.shape[1], cache_k.shape[2]
    xp = x_prompt.reshape(b * t, D_MODEL)
    xs = x_sample.reshape(db * dt, D_MODEL)
    outs = {n: [] for n in ('k_p', 'v_p', 'k_s', 'v_s', 'S_p', 'S_s', 'sh_p', 'sh_s', 'mk_p', 'mv_p')}
    for l in range(depth):
        lam_init = 0.8 - 0.6 * math.exp(-0.3 * l)
        lp = dict(norm_mix_g=norm_mix_g[l], mu=tok_shift_mu[l], w0=rwkv_w0[l], w2=rwkv_w2[l], a0=rwkv_a0[l],
                  a2=rwkv_a2[l], g2=rwkv_g2[l], k_k=rwkv_k_k[l], k_a=rwkv_k_a[l], r_k=rwkv_r_k[l],
                  lnx_g=rwkv_lnx_g[l], lnx_b=rwkv_lnx_b[l], lq1=diff_lq1[l], lk1=diff_lk1[l], lq2=diff_lq2[l],
                  lk2=diff_lk2[l], subln_g=diff_subln_g[l], norm_cross_g=norm_cross_g[l], norm_ffn_g=norm_ffn_g[l])
        wl = w_in[l].astype(BF16)
        wts = dict(in_r=wl[:, :RWKV_PROJ], in_da=wl[:, RWKV_PROJ:], o_r=w_o[l][:RWKV_DIM].astype(BF16),
                   o_d=w_o[l][RWKV_DIM:].astype(BF16), cq=w_cq[l].astype(BF16), co=w_co[l].astype(BF16),
                   gate=w_gate[l].astype(BF16), up=w_up[l].astype(BF16), down=w_down[l].astype(BF16),
                   final_g=final_norm_g, is_last=l == depth - 1)

        mem2 = mem_prompt.reshape(b * n_mem, D_MODEL)
        mk = _mm([mem2], [w_ck[l].astype(BF16)], name="mem_k", norm_g=norm_mem_g[l], tm=1024)
        mv = _mm([mem2], [w_cv[l].astype(BF16)], name="mem_v", norm_g=norm_mem_g[l], tm=1024)

        def rwkv_prompt(pr):
            return _rwkv_chunk_scan(pr, lp, b)

        def attn_prompt(qb, kf, vf, kb, vb):
            return _diff_attn_prompt(qb, kb, vb, lp, lam_init, b)

        xp, kf, vf, s_new, last = _layer(xp, b, lp, wts, lam_init, mk.reshape(b, n_mem, D_MODEL),
                                         mv.reshape(b, n_mem, D_MODEL), rwkv_prompt, attn_prompt)
        outs['k_p'].append(kf.reshape(b, t, DA_HEADS, 2 * DA_QK))
        outs['v_p'].append(vf.reshape(b, t, DA_HEADS, DA_V))
        outs['S_p'].append(s_new)
        outs['sh_p'].append(last)
        outs['mk_p'].append(mk.reshape(b, n_mem, CROSS_HEADS, CROSS_HD))
        outs['mv_p'].append(mv.reshape(b, n_mem, CROSS_HEADS, CROSS_HD))

        assert dt == 1

        def rwkv_sample(pr):
            r, k2, v, kk, a, lw, g = _rwkv_prep(pr, state_shift[l], lp)
            y, s_fin = _rwkv_step(state_wkv[l], r, k2, v, kk, a, lw)
            return _rwkv_post(y, r, k2, v, g, lp), s_fin

        def attn_sample(qb, kf, vf, kb, vb):
            return _diff_attn_paged(qb.astype(F32), kf, vf, cache_k[l], cache_v[l], page_table, lp, lam_init)

        xs, kf, vf, s_new, last = _layer(xs, db, lp, wts, lam_init, cache_mem_k[l], cache_mem_v[l],
                                         rwkv_sample, attn_sample)
        outs['k_s'].append(kf.reshape(db, dt, DA_HEADS, 2 * DA_QK))
        outs['v_s'].append(vf.reshape(db, dt, DA_HEADS, DA_V))
        outs['S_s'].append(s_new)
        outs['sh_s'].append(last)
    y_prompt = xp.reshape(b, t, D_MODEL)
    y_sample = xs.reshape(db, dt, D_MODEL)
    st = lambda n: jnp.stack(outs[n])
    return (y_prompt, y_sample, st('k_p'), st('v_p'), st('k_s'), st('v_s'), st('S_p'), st('S_s'),
            st('sh_p'), st('sh_s'), st('mk_p'), st('mv_p'))
```

```python
import functools
import math

import jax
import jax.numpy as jnp
from jax import lax
from jax.experimental import pallas as pl
from jax.experimental.pallas import tpu as pltpu

F32 = jnp.float32
BF16 = jnp.bfloat16

D_MODEL = 2048
RWKV_DIM = 1024
HEAD = 64
RWKV_HEADS = 16
DECAY_LORA = 64
AAA_LORA = 64
GATE_LORA = 128
RWKV_PROJ = 3 * RWKV_DIM + DECAY_LORA + AAA_LORA + GATE_LORA
DA_DIM = 1024
DA_QK = 64
DA_V = 128
DA_HEADS = 8
CROSS_HEADS = 4
CROSS_HD = D_MODEL // CROSS_HEADS
RMS_EPS = 1e-6
GN_EPS = 64e-5
NEG_INF = -1e30
LOG2_E = 1.4426950408889634

LANES = 128
CHUNK = 64
PAIRS = RWKV_HEADS // 2
VMEM_LIMIT = 56 * 1024 * 1024

_NT = (((1,), (1,)), ((), ()))


def _cparams(sem):
    return pltpu.CompilerParams(dimension_semantics=sem, vmem_limit_bytes=VMEM_LIMIT)


def _dot(a, b):
    return jnp.dot(a, b, preferred_element_type=F32)


def _dot_nt(a, b):
    return lax.dot_general(a, b, _NT, preferred_element_type=F32)


def _rms_rows(x, g):
    x = x.astype(F32)
    ms = jnp.mean(x * x, axis=-1, keepdims=True)
    return x * lax.rsqrt(ms + RMS_EPS) * g


def _sigmoid(x):
    return 1.0 / (1.0 + jnp.exp(-x))


def _split_dot(x, m_bf16, terms, left=False):
    acc = None
    rem = x
    for t in range(terms):
        piece = rem.astype(BF16)
        d = _dot(m_bf16, piece) if left else _dot(piece, m_bf16)
        acc = d if acc is None else acc + d
        if t + 1 < terms:
            rem = rem - piece.astype(F32)
    return acc


def _head_ones():
    i = lax.broadcasted_iota(jnp.int32, (LANES, LANES), 0) // HEAD
    j = lax.broadcasted_iota(jnp.int32, (LANES, LANES), 1) // HEAD
    return jnp.where(i == j, 1.0, 0.0).astype(BF16)


def _mm_kernel(*refs, n_a, has_norm, keep_norm, swiglu, has_res, scale):
    refs = list(refs)
    a_refs = [refs.pop(0) for _ in range(n_a)]
    g_ref = refs.pop(0) if has_norm else None
    w_refs = [refs.pop(0) for _ in range(2 if swiglu else n_a)]
    res_ref = refs.pop(0) if has_res else None
    o_ref = refs.pop(0)
    if has_norm and keep_norm:
        xn_ref = refs.pop(0)

        @pl.when(pl.program_id(1) == 0)
        def _():
            xn_ref[...] = _rms_rows(a_refs[0][...], g_ref[...]).astype(BF16)

        lhs = [xn_ref[...]]
    elif has_norm:
        lhs = [_rms_rows(a_refs[0][...], g_ref[...]).astype(BF16)]
    else:
        lhs = [r[...] for r in a_refs]
    if swiglu:
        gate = _dot(lhs[0], w_refs[0][...])
        up = _dot(lhs[0], w_refs[1][...])
        acc = gate * _sigmoid(gate) * up
    else:
        acc = _dot(lhs[0], w_refs[0][...])
        for a, w in zip(lhs[1:], w_refs[1:]):
            acc = acc + _dot(a, w[...])
    if scale != 1.0:
        acc = acc * scale
    if has_res:
        acc = acc + res_ref[...]
    o_ref[...] = acc.astype(o_ref.dtype)


def _mm(a_list, w_list, *, name, norm_g=None, swiglu=False, res=None, scale=1.0, out_dtype=F32, tm=512, tn=None):
    m = a_list[0].shape[0]
    n = w_list[0].shape[1]
    tm = min(tm, m)
    resident = tn is None
    tn = n if resident else min(tn, n)
    assert m % tm == 0 and n % tn == 0
    has_norm = norm_g is not None
    in_specs = [pl.BlockSpec((tm, a.shape[1]), lambda i, j: (i, 0)) for a in a_list]
    args = list(a_list)
    if has_norm:
        in_specs.append(pl.BlockSpec((1, a_list[0].shape[1]), lambda i, j: (0, 0)))
        args.append(norm_g.reshape(1, -1))
    for w in w_list:
        if resident:
            in_specs.append(pl.BlockSpec((w.shape[0], n), lambda i, j: (0, 0), pipeline_mode=pl.Buffered(1)))
        else:
            in_specs.append(pl.BlockSpec((w.shape[0], tn), lambda i, j: (0, j)))
        args.append(w)
    if res is not None:
        in_specs.append(pl.BlockSpec((tm, tn), lambda i, j: (i, j)))
        args.append(res)
    keep_norm = has_norm and n // tn > 1
    scratch = [pltpu.VMEM((tm, a_list[0].shape[1]), BF16)] if keep_norm else []
    return pl.pallas_call(
        functools.partial(_mm_kernel, n_a=len(a_list), has_norm=has_norm, keep_norm=keep_norm, swiglu=swiglu,
                          has_res=res is not None, scale=scale),
        grid=(m // tm, n // tn),
        in_specs=in_specs,
        out_specs=pl.BlockSpec((tm, tn), lambda i, j: (i, j)),
        out_shape=jax.ShapeDtypeStruct((m, n), out_dtype),
        scratch_shapes=scratch,
        compiler_params=_cparams(("parallel", "arbitrary")),
        name=name,
    )(*args)


def _da_proj_kernel(x_ref, g_ref, w_ref, q_ref, kf_ref, vf_ref, kb_ref, vb_ref):
    xn = _rms_rows(x_ref[...], g_ref[...]).astype(BF16)
    q_ref[...] = (_dot(xn, w_ref[:, :DA_DIM]) * (DA_QK ** -0.5 * LOG2_E)).astype(BF16)
    k = _dot(xn, w_ref[:, DA_DIM:2 * DA_DIM])
    kf_ref[...] = k
    kb_ref[...] = k.astype(BF16)
    v = _dot(xn, w_ref[:, 2 * DA_DIM:])
    vf_ref[...] = v
    vb_ref[...] = v.astype(BF16)


def _da_proj(x, norm_g, w_da, tm=512):
    m, k = x.shape
    tm = min(tm, m)
    blk = pl.BlockSpec((tm, DA_DIM), lambda i: (i, 0))
    return pl.pallas_call(
        _da_proj_kernel,
        grid=(m // tm,),
        in_specs=[pl.BlockSpec((tm, k), lambda i: (i, 0)),
                  pl.BlockSpec((1, k), lambda i: (0, 0)),
                  pl.BlockSpec((k, 3 * DA_DIM), lambda i: (0, 0), pipeline_mode=pl.Buffered(1))],
        out_specs=[blk] * 5,
        out_shape=[jax.ShapeDtypeStruct((m, DA_DIM), dt) for dt in (BF16, F32, F32, BF16, BF16)],
        compiler_params=_cparams(("parallel",)),
        name="da_proj",
    )(x, norm_g.reshape(1, -1), w_da)


def _mm_down_kernel(a_ref, w_ref, res_ref, g_ref, o_ref, *, final_norm):
    kk = pl.program_id(1)

    @pl.when(kk == 0)
    def _():
        o_ref[...] = res_ref[...]

    o_ref[...] += _dot(a_ref[...], w_ref[...])

    if final_norm:
        @pl.when(kk == pl.num_programs(1) - 1)
        def _():
            o_ref[...] = _rms_rows(o_ref[...], g_ref[...])


def _mm_down(a, w, res, final_g, final_norm, tm=512, tk=512):
    m, k = a.shape
    n = w.shape[1]
    tm = min(tm, m)
    assert m % tm == 0 and k % tk == 0
    return pl.pallas_call(
        functools.partial(_mm_down_kernel, final_norm=final_norm),
        grid=(m // tm, k // tk),
        in_specs=[pl.BlockSpec((tm, tk), lambda i, kk: (i, kk)),
                  pl.BlockSpec((tk, n), lambda i, kk: (kk, 0)),
                  pl.BlockSpec((tm, n), lambda i, kk: (i, 0)),
                  pl.BlockSpec((1, n), lambda i, kk: (0, 0))],
        out_specs=pl.BlockSpec((tm, n), lambda i, kk: (i, 0)),
        out_shape=jax.ShapeDtypeStruct((m, n), F32),
        compiler_params=_cparams(("parallel", "arbitrary")),
        name="ffn_down",
    )(a, w, res, final_g.reshape(1, -1))


def _head_sum(x):
    lo = lax.broadcasted_iota(jnp.int32, (1, LANES), 1) < HEAD
    s_lo = jnp.sum(jnp.where(lo, x, 0.0), axis=-1, keepdims=True)
    s_all = jnp.sum(x, axis=-1, keepdims=True)
    return jnp.where(lo, s_lo, s_all - s_lo)


def _token_mix(p, p_prev, mu, w0, w2, a0, a2, g2, k_k, k_a):
    ps = p + (p_prev - p) * mu
    o1, o2, o3 = RWKV_DIM, 2 * RWKV_DIM, 3 * RWKV_DIM
    o4 = o3 + DECAY_LORA
    o5 = o4 + AAA_LORA
    r, k, v = ps[:, :o1], ps[:, o1:o2], ps[:, o2:o3]
    wd, ad, gd = ps[:, o3:o4], ps[:, o4:o5], ps[:, o5:]
    z = w0 + _dot(jnp.tanh(wd).astype(BF16), w2)
    w_log = -(jnp.maximum(-z, 0.0) + jnp.log(1.0 + jnp.exp(-jnp.abs(z)))) - 0.5
    lw = -jnp.exp(w_log)
    a = _sigmoid(a0 + _dot(ad.astype(BF16), a2))
    g = _dot(_sigmoid(gd).astype(BF16), g2)
    kraw = k * k_k
    kk = []
    for pr in range(PAIRS):
        kp = kraw[:, pr * LANES:(pr + 1) * LANES]
        kk.append(kp * lax.rsqrt(jnp.maximum(_head_sum(kp * kp), 1e-24)))
    return r, k * (1.0 + (a - 1.0) * k_a), v, kk, a, lw, g


def _mix_out(y, r, k2, v, g, r_k, ln_g, ln_b):
    mean = _head_sum(y) * (1.0 / HEAD)
    d = y - mean
    var = _head_sum(d * d) * (1.0 / HEAD)
    yn = d * lax.rsqrt(var + GN_EPS) * ln_g + ln_b
    return (yn + _head_sum(r * k2 * r_k) * v) * g


def _prep_kernel(p_ref, prev_ref, mu_ref, w0_ref, w2_ref, a0_ref, a2_ref, g2_ref, kk_ref, ka_ref,
                 r_o, k_o, v_o, kk_o, a_o, lw_o, g_o):
    r, k2, v, kk, a, lw, g = _token_mix(p_ref[...], prev_ref[...], mu_ref[...], w0_ref[...], w2_ref[...],
                                        a0_ref[...], a2_ref[...], g2_ref[...], kk_ref[...], ka_ref[...])
    for pr in range(PAIRS):
        kk_o[:, pr * LANES:(pr + 1) * LANES] = kk[pr]
    r_o[...] = r
    k_o[...] = k2
    v_o[...] = v
    a_o[...] = a
    lw_o[...] = lw
    g_o[...] = g


def _mix_params(lp):
    row = lambda x: x.reshape(1, -1)
    return (row(lp['mu']), row(lp['w0']), lp['w2'].astype(BF16), row(lp['a0']), lp['a2'].astype(BF16),
            lp['g2'].astype(BF16), row(lp['k_k']), row(lp['k_a']))


def _mix_param_specs(index_map):
    shapes = [(1, RWKV_PROJ), (1, RWKV_DIM), (DECAY_LORA, RWKV_DIM), (1, RWKV_DIM), (AAA_LORA, RWKV_DIM),
              (GATE_LORA, RWKV_DIM), (1, RWKV_DIM), (1, RWKV_DIM)]
    return [pl.BlockSpec(s, index_map) for s in shapes]


def _rwkv_prep(p, prev, lp):
    m = p.shape[0]
    blk_in = pl.BlockSpec((m, RWKV_PROJ), lambda i: (0, 0))
    out_blk = pl.BlockSpec((m, RWKV_DIM), lambda i: (0, 0))
    return pl.pallas_call(
        _prep_kernel,
        grid=(1,),
        in_specs=[blk_in, blk_in] + _mix_param_specs(lambda i: (0, 0)),
        out_specs=[out_blk] * 7,
        out_shape=[jax.ShapeDtypeStruct((m, RWKV_DIM), F32)] * 7,
        compiler_params=_cparams(("arbitrary",)),
        name="rwkv_prep",
    )(p, prev, *_mix_params(lp))


def _chunk_kernel(p_ref, mu_ref, w0_ref, w2_ref, a0_ref, a2_ref, g2_ref, kk_ref, ka_ref, rk_ref, lg_ref, lb_ref,
                  y_ref, s_out_ref, s_ref, prev_ref):
    c = pl.program_id(1)

    @pl.when(c == 0)
    def _():
        s_ref[...] = jnp.zeros_like(s_ref)
        prev_ref[...] = jnp.zeros_like(prev_ref)

    p = p_ref[...]
    rows = p.shape[0]
    n_chunks = rows // CHUNK
    first_row = lax.broadcasted_iota(jnp.int32, p.shape, 0) == 0
    p_prev = jnp.where(first_row, prev_ref[...], pltpu.roll(p, 1, 0))
    prev_ref[...] = p[rows - 1:rows, :]
    r_all, k_all, v_all, kk_all, a_all, lw_all, g_all = _token_mix(
        p, p_prev, mu_ref[...], w0_ref[...], w2_ref[...], a0_ref[...], a2_ref[...], g2_ref[...], kk_ref[...],
        ka_ref[...])

    lane = lax.broadcasted_iota(jnp.int32, (1, LANES), 1)
    m0 = jnp.where(lane < HEAD, 1.0, 0.0)
    m1 = 1.0 - m0
    ii = lax.broadcasted_iota(jnp.int32, (LANES, LANES), 0)
    jj = lax.broadcasted_iota(jnp.int32, (LANES, LANES), 1)
    strict = ii > jj
    incl = ii >= jj
    eye = jnp.where(ii == jj, 1.0, 0.0)
    ti = lax.broadcasted_iota(jnp.int32, (CHUNK, CHUNK), 0)
    tj = lax.broadcasted_iota(jnp.int32, (CHUNK, CHUNK), 1)
    tril = jnp.where(ti >= tj, 1.0, 0.0).astype(BF16)

    def stack(x):
        return jnp.concatenate([x * m0, x * m1], axis=0)

    pairs = range(PAIRS)
    items = [(ci, pr) for ci in range(n_chunks) for pr in pairs]
    tok = [slice(ci * CHUNK, (ci + 1) * CHUNK) for ci, _ in items]
    sls = [slice(pr * LANES, (pr + 1) * LANES) for _, pr in items]
    each = lambda fn, *cols: [fn(*xs) for xs in zip(*cols)]
    cut = lambda x: [x[t, sl] for t, sl in zip(tok, sls)]
    lw, k2, r, v = cut(lw_all), cut(k_all), cut(r_all), cut(v_all)
    kk = [kk_all[pr][t, :] for (_, pr), t in zip(items, tok)]
    bb = each(lambda x, y: x * y, kk, cut(a_all))
    cum_all = [_split_dot(lw_all[ci * CHUNK:(ci + 1) * CHUNK, :], tril, 3, left=True) for ci in range(n_chunks)]
    cum = [cum_all[ci][:, sl] for (ci, _), sl in zip(items, sls)]
    tot = [x[CHUNK - 1:CHUNK, :] for x in cum]
    inv_g = [jnp.exp(-x) for x in cum]
    to_end = each(lambda t, x: jnp.exp(t - x), tot, cum)
    a_t = each(lambda x, cm, l: stack(-x * jnp.exp(cm - l)).astype(BF16), kk, cum, lw)
    b_t = each(lambda x, g: stack(x * g).astype(BF16), bb, inv_g)
    k_t = each(lambda x, g: stack(x * g).astype(BF16), k2, inv_g)
    r_f = each(lambda x, cm: stack(x * jnp.exp(cm)), r, cum)
    r_t = [x.astype(BF16) for x in r_f]
    v_f = [stack(x) for x in v]
    v_s = [x.astype(BF16) for x in v_f]
    bk_end = each(lambda x, y, e: jnp.concatenate([stack(x * e), stack(y * e)], axis=0).astype(BF16), bb, k2, to_end)

    a_ab = each(lambda x, y: jnp.where(strict, _dot_nt(x, y), 0.0), a_t, b_t)
    a_ak = each(lambda x, y: jnp.where(strict, _dot_nt(x, y), 0.0).astype(BF16), a_t, k_t)
    a_rb = each(lambda x, y: jnp.where(incl, _dot_nt(x, y), 0.0).astype(BF16), r_t, b_t)
    a_rk = each(lambda x, y: jnp.where(incl, _dot_nt(x, y), 0.0).astype(BF16), r_t, k_t)
    inv = [eye + x for x in a_ab]
    power = [x.astype(BF16) for x in a_ab]
    for _ in range(5):
        power = [_dot(x, x).astype(BF16) for x in power]
        inv = each(lambda x, p: x + _dot(x.astype(BF16), p), inv, power)
    w1 = each(lambda x, y: _dot(x, y).astype(BF16), a_ak, v_s)
    pu = each(lambda x, y, z: _dot(x.astype(BF16), jnp.concatenate([y, z], axis=1)), inv, a_t, w1)
    pu_b = [x.astype(BF16) for x in pu]
    qy = each(_dot, a_rb, pu_b)
    q_m = each(lambda x, y: (x + y[:, :LANES]).astype(BF16), r_f, qy)
    y_v = each(lambda x, y, z: x[:, LANES:] + _dot(y, z), qy, a_rk, v_s)
    pq = each(lambda x, y: jnp.concatenate([x[:, :LANES], y], axis=0), pu_b, q_m)
    state = [s_ref[pr] for pr in pairs]
    g_cut = cut(g_all)
    for ci in range(n_chunks):
        grp = slice(ci * PAIRS, (ci + 1) * PAIRS)
        uy = each(lambda x, s: _dot_nt(x, s.astype(BF16)), pq[grp], state)
        uv_t = each(lambda x, y, z: jnp.concatenate([x[:LANES] + y[:, LANES:], z], axis=0).T.astype(BF16),
                    uy, pu[grp], v_f[grp])
        state = each(lambda s, t, x, y: s * jnp.exp(t) + _dot(x, y), state, tot[grp], uv_t, bk_end[grp])
        for pr in pairs:
            i = ci * PAIRS + pr
            y_m = uy[pr][LANES:] + y_v[i]
            out = _mix_out(y_m[:CHUNK] + y_m[CHUNK:], r[i], k2[i], v[i], g_cut[i], rk_ref[:, sls[i]],
                           lg_ref[:, sls[i]], lb_ref[:, sls[i]])
            y_ref[tok[i], sls[i]] = out.astype(y_ref.dtype)
    for pr in pairs:
        s_ref[pr] = state[pr]

    @pl.when(c == pl.num_programs(1) - 1)
    def _():
        s_out_ref[0] = s_ref[...]


def _rwkv_chunk_scan(p, lp, batch):
    m = p.shape[0]
    n_chunks = m // batch // CHUNK
    cps = next(c for c in (4, 2, 1) if n_chunks % c == 0)
    nc = n_chunks // cps
    row = lambda x: x.reshape(1, -1)
    const = lambda b, c: (0, 0)
    vec = pl.BlockSpec((1, RWKV_DIM), const)
    y, s_pairs = pl.pallas_call(
        _chunk_kernel,
        grid=(batch, nc),
        in_specs=([pl.BlockSpec((cps * CHUNK, RWKV_PROJ), lambda b, c: (b * nc + c, 0))] + _mix_param_specs(const)
                  + [vec] * 3),
        out_specs=[pl.BlockSpec((cps * CHUNK, RWKV_DIM), lambda b, c: (b * nc + c, 0)),
                   pl.BlockSpec((1, PAIRS, LANES, LANES), lambda b, c: (b, 0, 0, 0))],
        out_shape=[jax.ShapeDtypeStruct((m, RWKV_DIM), BF16),
                   jax.ShapeDtypeStruct((batch, PAIRS, LANES, LANES), F32)],
        scratch_shapes=[pltpu.VMEM((PAIRS, LANES, LANES), F32), pltpu.VMEM((1, RWKV_PROJ), F32)],
        compiler_params=_cparams(("parallel", "arbitrary")),
        name="rwkv_chunk",
    )(p, *_mix_params(lp), row(lp['r_k']), row(lp['lnx_g']), row(lp['lnx_b']))
    s_fin = jnp.stack([s_pairs[:, :, :HEAD, :HEAD], s_pairs[:, :, HEAD:, HEAD:]], axis=2)
    return y, s_fin.reshape(batch, RWKV_HEADS, HEAD, HEAD)


def _step_kernel(s_ref, r_ref, k_ref, kk_ref, a_ref, lw_ref, v_ref, s_o, y_o):
    s = s_ref[0]
    kk = kk_ref[0]
    sa = jnp.sum(s * (-kk), axis=-1, keepdims=True)
    s_new = s * jnp.exp(lw_ref[0]) + sa * (kk * a_ref[0]) + v_ref[0] * k_ref[0]
    s_o[0] = s_new
    y_o[0] = jnp.sum(s_new * r_ref[0], axis=-1, keepdims=True)


def _rwkv_step(s0, r, k2, v, kk, a, lw):
    b = s0.shape[0]
    rowv = lambda x: x.reshape(b, RWKV_HEADS, 1, HEAD)
    row_spec = pl.BlockSpec((1, RWKV_HEADS, 1, HEAD), lambda i: (i, 0, 0, 0))
    col_spec = pl.BlockSpec((1, RWKV_HEADS, HEAD, 1), lambda i: (i, 0, 0, 0))
    s_spec = pl.BlockSpec((1, RWKV_HEADS, HEAD, HEAD), lambda i: (i, 0, 0, 0))
    s_new, y = pl.pallas_call(
        _step_kernel,
        grid=(b,),
        in_specs=[s_spec] + [row_spec] * 5 + [col_spec],
        out_specs=[s_spec, col_spec],
        out_shape=[jax.ShapeDtypeStruct(s0.shape, F32), jax.ShapeDtypeStruct((b, RWKV_HEADS, HEAD, 1), F32)],
        compiler_params=_cparams(("parallel",)),
        name="rwkv_step",
    )(s0, rowv(r), rowv(k2), rowv(kk), rowv(a), rowv(lw), v.reshape(b, RWKV_HEADS, HEAD, 1))
    return y.reshape(b, RWKV_DIM), s_new


def _post_kernel(y_ref, r_ref, k_ref, v_ref, g_ref, rk_ref, lg_ref, lb_ref, o_ref):
    for pr in range(PAIRS):
        sl = slice(pr * LANES, (pr + 1) * LANES)
        out = _mix_out(y_ref[:, sl], r_ref[:, sl], k_ref[:, sl], v_ref[:, sl], g_ref[:, sl], rk_ref[:, sl],
                       lg_ref[:, sl], lb_ref[:, sl])
        o_ref[:, sl] = out.astype(o_ref.dtype)


def _rwkv_post(y, r, k2, v, g, lp):
    m = y.shape[0]
    blk = pl.BlockSpec((m, RWKV_DIM), lambda i: (0, 0))
    par = pl.BlockSpec((1, RWKV_DIM), lambda i: (0, 0))
    return pl.pallas_call(
        _post_kernel,
        grid=(1,),
        in_specs=[blk] * 5 + [par] * 3,
        out_specs=blk,
        out_shape=jax.ShapeDtypeStruct((m, RWKV_DIM), BF16),
        compiler_params=_cparams(("arbitrary",)),
        name="rwkv_post",
    )(y, r, k2, v, g, lp['r_k'].reshape(1, -1), lp['lnx_g'].reshape(1, -1), lp['lnx_b'].reshape(1, -1))


def _lambda(lq1_ref, lk1_ref, lq2_ref, lk2_ref, lam_init):
    s1 = jnp.sum(lq1_ref[...] * lk1_ref[...], axis=-1, keepdims=True)
    s2 = jnp.sum(lq2_ref[...] * lk2_ref[...], axis=-1, keepdims=True)
    return jnp.exp(s1) - jnp.exp(s2) + lam_init


def _dattn_kernel(lq1_ref, lk1_ref, lq2_ref, lk2_ref, sg_ref, q_ref, k_ref, v_ref, o_ref,
                  acc_ref, *, tq, lam_init):
    nq = q_ref.shape[0] // tq
    lane = lax.broadcasted_iota(jnp.int32, (1, LANES), 1)
    row = lax.broadcasted_iota(jnp.int32, (tq, tq), 0)
    col = lax.broadcasted_iota(jnp.int32, (tq, tq), 1)
    ones = jnp.ones((tq, LANES), BF16)
    lam = _lambda(lq1_ref, lk1_ref, lq2_ref, lk2_ref, lam_init)
    rows = lambda ref, i: ref[i * tq:(i + 1) * tq, :]

    def scores(qi, j):
        q = rows(q_ref, qi)
        zero = jnp.zeros_like(q)
        ks = rows(k_ref, j)
        return [_dot_nt(jnp.where(lane < DA_QK, q, zero), ks), _dot_nt(jnp.where(lane >= DA_QK, q, zero), ks)]

    blocks = [(qi, j) for qi in range(nq) for j in range(qi + 1)]
    n_blocks = len(blocks)
    s_buf, p_buf = {}, {}
    m_run = None
    for t in range(n_blocks + 2):
        if t < n_blocks:
            s_buf[t] = scores(*blocks[t])
        pv = None
        if 0 <= t - 2 < n_blocks:
            p, p_alpha = p_buf.pop(t - 2)
            v_aug = jnp.concatenate([rows(v_ref, blocks[t - 2][1]), ones], axis=1)
            pv = [_dot(x, v_aug) for x in p]
        if 0 <= t - 1 < n_blocks:
            qi, j = blocks[t - 1]
            s = s_buf.pop(t - 1)
            if j == qi:
                s = [jnp.where(col <= row, x, NEG_INF) for x in s]
            row_max = [jnp.max(x, axis=-1, keepdims=True) for x in s]
            if j == 0:
                m_new, alpha = row_max, None
            else:
                m_new = [jnp.maximum(mo, mx) for mo, mx in zip(m_run, row_max)]
                alpha = [jnp.exp2(mo - mn) for mo, mn in zip(m_run, m_new)]
            p_buf[t - 1] = ([jnp.exp2(x - mn).astype(BF16) for x, mn in zip(s, m_new)], alpha)
            m_run = m_new
        if pv is not None:
            qi, j = blocks[t - 2]
            acc = pv if j == 0 else [alpha_c * acc_ref[c] + pv[c] for c, alpha_c in enumerate(p_alpha)]
            if j == qi:
                a0, a1 = acc
                o = a0[:, :LANES] / a0[:, LANES:] - lam * (a1[:, :LANES] / a1[:, LANES:])
                o_ref[qi * tq:(qi + 1) * tq, :] = (_rms_rows(o, sg_ref[...]) * (1.0 - lam_init)).astype(o_ref.dtype)
            else:
                for c in range(2):
                    acc_ref[c] = acc[c]


def _diff_attn_prompt(qb, kb, vb, lp, lam_init, batch, tq=512):
    m = qb.shape[0]
    t = m // batch
    tq = min(tq, t)
    vec = pl.BlockSpec((1, DA_QK), lambda b, h: (0, 0))
    seq_spec = pl.BlockSpec((t, LANES), lambda b, h: (b, h))
    return pl.pallas_call(
        functools.partial(_dattn_kernel, tq=tq, lam_init=lam_init),
        grid=(batch, DA_HEADS),
        in_specs=[vec] * 4 + [pl.BlockSpec((1, DA_V), lambda b, h: (0, 0)), seq_spec, seq_spec, seq_spec],
        out_specs=seq_spec,
        out_shape=jax.ShapeDtypeStruct((m, DA_DIM), BF16),
        scratch_shapes=[pltpu.VMEM((2, tq, 2 * LANES), F32)],
        compiler_params=_cparams(("parallel", "parallel")),
        name="diff_attn_prompt",
    )(*[lp[n].reshape(1, -1) for n in ('lq1', 'lk1', 'lq2', 'lk2')], lp['subln_g'].reshape(1, -1), qb, kb, vb)


def _paged_kernel(pt_ref, lq1_ref, lk1_ref, lq2_ref, lk2_ref, sg_ref, q_ref, kn_ref, vn_ref, *rest, pps, lam_init):
    del pt_ref
    k_refs, v_refs = rest[:pps], rest[pps:2 * pps]
    o_ref, m_ref, l_ref, acc_ref = rest[2 * pps:]
    pg = pl.program_id(1)
    n_rows = 2 * DA_HEADS

    @pl.when(pg == 0)
    def _():
        m_ref[...] = jnp.full_like(m_ref, NEG_INF)
        l_ref[...] = jnp.zeros_like(l_ref)
        acc_ref[...] = jnp.zeros_like(acc_ref)

    q = q_ref[0]
    lane = lax.broadcasted_iota(jnp.int32, (1, LANES), 1)
    q_rows = jnp.concatenate([jnp.where(lane < DA_QK, q, 0.0), jnp.where(lane >= DA_QK, q, 0.0)], axis=0)
    q_bf = q_rows.astype(BF16)
    cols = k_refs[0].shape[1] * DA_HEADS
    same_head = ((lax.broadcasted_iota(jnp.int32, (n_rows, cols), 1) & (DA_HEADS - 1))
                 == (lax.broadcasted_iota(jnp.int32, (n_rows, cols), 0) & (DA_HEADS - 1)))
    flat = lambda ref: ref[0].reshape(cols, LANES).astype(BF16)

    m_run, l_run, acc = m_ref[...], l_ref[...], acc_ref[...]
    s_buf, p_buf = {}, {}
    for t in range(pps + 2):
        if t < pps:
            s_buf[t] = _dot_nt(q_bf, flat(k_refs[t]))
        pv = None
        if 0 <= t - 2 < pps:
            p, p_alpha = p_buf.pop(t - 2)
            pv = _dot(p, flat(v_refs[t - 2]))
        if 0 <= t - 1 < pps:
            s = jnp.where(same_head, s_buf.pop(t - 1), NEG_INF)
            m_new = jnp.maximum(m_run, jnp.max(s, axis=-1, keepdims=True))
            alpha = jnp.exp2(m_run - m_new)
            p = jnp.exp2(s - m_new)
            l_run = alpha * l_run + jnp.sum(p, axis=-1, keepdims=True)
            p_buf[t - 1] = (p.astype(BF16), alpha)
            m_run = m_new
        if pv is not None:
            acc = p_alpha * acc + pv
    m_ref[...] = m_run
    l_ref[...] = l_run
    acc_ref[...] = acc

    @pl.when(pg == pl.num_programs(1) - 1)
    def _():
        k_new = jnp.concatenate([kn_ref[0]] * 2, axis=0)
        v_new = jnp.concatenate([vn_ref[0]] * 2, axis=0)
        s_new = jnp.sum(q_rows * k_new, axis=-1, keepdims=True)
        m_fin = jnp.maximum(m_run, s_new)
        alpha = jnp.exp2(m_run - m_fin)
        p_new = jnp.exp2(s_new - m_fin)
        out = (alpha * acc + p_new * v_new) / (alpha * l_run + p_new)
        lam = _lambda(lq1_ref, lk1_ref, lq2_ref, lk2_ref, lam_init)
        o = out[:DA_HEADS] - lam * out[DA_HEADS:]
        o_ref[0] = (_rms_rows(o, sg_ref[...]) * (1.0 - lam_init)).astype(o_ref.dtype)


def _diff_attn_paged(q, k_new, v_new, cache_k, cache_v, page_table, lp, lam_init):
    b, n_pages = page_table.shape
    page = cache_k.shape[1]
    pps = next(c for c in (16, 8, 4, 2, 1) if n_pages % c == 0)
    vec = pl.BlockSpec((1, DA_QK), lambda i, p, pt: (0, 0))
    rowb = pl.BlockSpec((1, DA_HEADS, LANES), lambda i, p, pt: (i, 0, 0))

    def page_spec(j):
        return pl.BlockSpec((1, page, DA_HEADS, LANES),
                            lambda i, p, pt: (pt[i * n_pages + p * pps + j], 0, 0, 0))

    grid_spec = pltpu.PrefetchScalarGridSpec(
        num_scalar_prefetch=1,
        grid=(b, n_pages // pps),
        in_specs=([vec] * 4 + [pl.BlockSpec((1, DA_V), lambda i, p, pt: (0, 0)), rowb, rowb, rowb]
                  + [page_spec(j) for j in range(pps)] * 2),
        out_specs=rowb,
        scratch_shapes=[pltpu.VMEM((2 * DA_HEADS, 1), F32)] * 2 + [pltpu.VMEM((2 * DA_HEADS, LANES), F32)],
    )
    r3 = lambda x: x.reshape(b, DA_HEADS, LANES)
    out = pl.pallas_call(
        functools.partial(_paged_kernel, pps=pps, lam_init=lam_init),
        grid_spec=grid_spec,
        out_shape=jax.ShapeDtypeStruct((b, DA_HEADS, LANES), BF16),
        compiler_params=_cparams(("parallel", "arbitrary")),
        name="diff_attn_paged",
    )(page_table.reshape(-1), *[lp[n].reshape(1, -1) for n in ('lq1', 'lk1', 'lq2', 'lk2')],
      lp['subln_g'].reshape(1, -1), r3(q), r3(k_new), r3(v_new), *([cache_k] * pps), *([cache_v] * pps))
    return out.reshape(b, DA_DIM)


def _cross_kernel(q_ref, mk_ref, mv_ref, o_ref):
    for h in range(CROSS_HEADS):
        sl = slice(h * CROSS_HD, (h + 1) * CROSS_HD)
        s = _dot_nt(q_ref[0, :, sl], mk_ref[0, :, sl].astype(BF16))
        m = jnp.max(s, axis=-1, keepdims=True)
        p = jnp.exp(s - m)
        l = jnp.sum(p, axis=-1, keepdims=True)
        o_ref[0, :, sl] = (_dot(p.astype(BF16), mv_ref[0, :, sl].astype(BF16)) / l).astype(o_ref.dtype)


def _cross_decode_kernel(q_ref, mk_ref, mv_ref, o_ref):
    q = q_ref[0].astype(F32)
    s = jnp.sum(mk_ref[0] * q, axis=-1, keepdims=True)
    p = jnp.exp(s - jnp.max(s, axis=0, keepdims=True))
    o = jnp.sum(p * mv_ref[0], axis=0) / jnp.sum(p, axis=0)
    o_ref[0] = o.astype(o_ref.dtype)


def _cross_attn_decode(q, mk, mv):
    b = q.shape[0]
    mt = mk.shape[1]
    q_spec = pl.BlockSpec((1, CROSS_HEADS, CROSS_HD), lambda i: (i, 0, 0))
    m_spec = pl.BlockSpec((1, mt, CROSS_HEADS, CROSS_HD), lambda i: (i, 0, 0, 0))
    out = pl.pallas_call(
        _cross_decode_kernel,
        grid=(b,),
        in_specs=[q_spec, m_spec, m_spec],
        out_specs=q_spec,
        out_shape=jax.ShapeDtypeStruct((b, CROSS_HEADS, CROSS_HD), BF16),
        compiler_params=_cparams(("parallel",)),
        name="cross_attn_decode",
    )(q.reshape(b, CROSS_HEADS, CROSS_HD), mk, mv)
    return out.reshape(b, D_MODEL)


def _cross_attn(q, mk, mv, tq=512):
    b, t, _ = q.shape
    mt = mk.shape[1]
    tq = min(tq, t)
    q_spec = pl.BlockSpec((1, tq, D_MODEL), lambda i, j: (i, j, 0))
    m_spec = pl.BlockSpec((1, mt, D_MODEL), lambda i, j: (i, 0, 0))
    return pl.pallas_call(
        _cross_kernel,
        grid=(b, t // tq),
        in_specs=[q_spec, m_spec, m_spec],
        out_specs=q_spec,
        out_shape=jax.ShapeDtypeStruct(q.shape, BF16),
        compiler_params=_cparams(("parallel", "parallel")),
        name="cross_attn",
    )(q, mk, mv)


def _layer(x, batch, lp, wts, lam_init, mk, mv, rwkv_fn, attn_fn):
    m = x.shape[0]
    t = m // batch
    pr = _mm([x], [wts['in_r']], name="in_proj_rwkv", norm_g=lp['norm_mix_g'], tm=512)
    qb, kf, vf, kb, vb = _da_proj(x, lp['norm_mix_g'], wts['in_da'])
    y_r, s_new = rwkv_fn(pr)
    y_d = attn_fn(qb, kf, vf, kb, vb)
    x = _mm([y_r, y_d], [wts['o_r'], wts['o_d']], name="out_proj", res=x, tm=512)
    qc = _mm([x], [wts['cq']], name="cross_q", norm_g=lp['norm_cross_g'], scale=CROSS_HD ** -0.5, out_dtype=BF16,
             tm=1024)
    if mk.ndim == 4:
        oc = _cross_attn_decode(qc, mk, mv)
    else:
        oc = _cross_attn(qc.reshape(batch, t, D_MODEL), mk, mv).reshape(m, D_MODEL)
    x = _mm([oc], [wts['co']], name="cross_out", res=x, tm=512)
    hmid = _mm([x], [wts['gate'], wts['up']], name="ffn_swiglu", norm_g=lp['norm_ffn_g'], swiglu=True,
               out_dtype=BF16, tm=1024, tn=512)
    x = _mm_down(hmid, wts['down'], x, wts['final_g'], wts['is_last'], tm=1024, tk=hmid.shape[1] // 4)
    last_row = pr.reshape(batch, t, RWKV_PROJ)[:, -1]
    return x, kf, vf, s_new, last_row


def kernel(x_prompt, x_sample, cache_k, cache_v, cache_mem_k, cache_mem_v, state_wkv, state_shift, page_table, mem_prompt, norm_mix_g, w_in, tok_shift_mu, rwkv_w0, rwkv_w2, rwkv_a0, rwkv_a2, rwkv_g2, rwkv_k_k, rwkv_k_a, rwkv_r_k, rwkv_lnx_g, rwkv_lnx_b, diff_lq1, diff_lk1, diff_lq2, diff_lk2, diff_subln_g, w_o, norm_cross_g, norm_mem_g, w_cq, w_ck, w_cv, w_co, norm_ffn_g, w_gate, w_up, w_down, final_norm_g):
    b, t, _ = x_prompt.shape
    db, dt, _ = x_sample.shape
    depth = w_in.shape[0]
    n_mem = mem_prompt.shape[1]
    n_pool, page = cache_k.shape[1], cache_k.shape[2]
    xp = x_prompt.reshape(b * t, D_MODEL)
    xs = x_sample.reshape(db * dt, D_MODEL)
    outs = {n: [] for n in ('k_p', 'v_p', 'k_s', 'v_s', 'S_p', 'S_s', 'sh_p', 'sh_s', 'mk_p', 'mv_p')}
    for l in range(depth):
        lam_init = 0.8 - 0.6 * math.exp(-0.3 * l)
        lp = dict(norm_mix_g=norm_mix_g[l], mu=tok_shift_mu[l], w0=rwkv_w0[l], w2=rwkv_w2[l], a0=rwkv_a0[l],
                  a2=rwkv_a2[l], g2=rwkv_g2[l], k_k=rwkv_k_k[l], k_a=rwkv_k_a[l], r_k=rwkv_r_k[l],
                  lnx_g=rwkv_lnx_g[l], lnx_b=rwkv_lnx_b[l], lq1=diff_lq1[l], lk1=diff_lk1[l], lq2=diff_lq2[l],
                  lk2=diff_lk2[l], subln_g=diff_subln_g[l], norm_cross_g=norm_cross_g[l], norm_ffn_g=norm_ffn_g[l])
        wl = w_in[l].astype(BF16)
        wts = dict(in_r=wl[:, :RWKV_PROJ], in_da=wl[:, RWKV_PROJ:], o_r=w_o[l][:RWKV_DIM].astype(BF16),
                   o_d=w_o[l][RWKV_DIM:].astype(BF16), cq=w_cq[l].astype(BF16), co=w_co[l].astype(BF16),
                   gate=w_gate[l].astype(BF16), up=w_up[l].astype(BF16), down=w_down[l].astype(BF16),
                   final_g=final_norm_g, is_last=l == depth - 1)

        mem2 = mem_prompt.reshape(b * n_mem, D_MODEL)
        mk = _mm([mem2], [w_ck[l].astype(BF16)], name="mem_k", norm_g=norm_mem_g[l], tm=1024)
        mv = _mm([mem2], [w_cv[l].astype(BF16)], name="mem_v", norm_g=norm_mem_g[l], tm=1024)

        def rwkv_prompt(pr):
            return _rwkv_chunk_scan(pr, lp, b)

        def attn_prompt(qb, kf, vf, kb, vb):
            return _diff_attn_prompt(qb, kb, vb, lp, lam_init, b)

        xp, kf, vf, s_new, last = _layer(xp, b, lp, wts, lam_init, mk.reshape(b, n_mem, D_MODEL),
                                         mv.reshape(b, n_mem, D_MODEL), rwkv_prompt, attn_prompt)
        outs['k_p'].append(kf.reshape(b, t, DA_HEADS, 2 * DA_QK))
        outs['v_p'].append(vf.reshape(b, t, DA_HEADS, DA_V))
        outs['S_p'].append(s_new)
        outs['sh_p'].append(last)
        outs['mk_p'].append(mk.reshape(b, n_mem, CROSS_HEADS, CROSS_HD))
        outs['mv_p'].append(mv.reshape(b, n_mem, CROSS_HEADS, CROSS_HD))

        assert dt == 1

        def rwkv_sample(pr):
            r, k2, v, kk, a, lw, g = _rwkv_prep(pr, state_shift[l], lp)
            y, s_fin = _rwkv_step(state_wkv[l], r, k2, v, kk, a, lw)
            return _rwkv_post(y, r, k2, v, g, lp), s_fin

        def attn_sample(qb, kf, vf, kb, vb):
            return _diff_attn_paged(qb.astype(F32), kf, vf, cache_k[l], cache_v[l], page_table, lp, lam_init)

        xs, kf, vf, s_new, last = _layer(xs, db, lp, wts, lam_init, cache_mem_k[l], cache_mem_v[l],
                                         rwkv_sample, attn_sample)
        outs['k_s'].append(kf.reshape(db, dt, DA_HEADS, 2 * DA_QK))
        outs['v_s'].append(vf.reshape(db, dt, DA_HEADS, DA_V))
        outs['S_s'].append(s_new)
        outs['sh_s'].append(last)
    y_prompt = xp.reshape(b, t, D_MODEL)
    y_sample = xs.reshape(db, dt, D_MODEL)
    st = lambda n: jnp.stack(outs[n])
    return (y_prompt, y_sample, st('k_p'), st('v_p'), st('k_s'), st('v_s'), st('S_p'), st('S_s'),
            st('sh_p'), st('sh_s'), st('mk_p'), st('mv_p'))
```

```python
import functools
import math

import jax
import jax.numpy as jnp
from jax import lax
from jax.experimental import pallas as pl
from jax.experimental.pallas import tpu as pltpu

F32 = jnp.float32
BF16 = jnp.bfloat16

D_MODEL = 2048
RWKV_DIM = 1024
HEAD = 64
RWKV_HEADS = 16
DECAY_LORA = 64
AAA_LORA = 64
GATE_LORA = 128
RWKV_PROJ = 3 * RWKV_DIM + DECAY_LORA + AAA_LORA + GATE_LORA
DA_DIM = 1024
DA_QK = 64
DA_V = 128
DA_HEADS = 8
CROSS_HEADS = 4
CROSS_HD = D_MODEL // CROSS_HEADS
RMS_EPS = 1e-6
GN_EPS = 64e-5
NEG_INF = -1e30
LOG2_E = 1.4426950408889634

LANES = 128
CHUNK = 64
PAIRS = RWKV_HEADS // 2
VMEM_LIMIT = 56 * 1024 * 1024

_NT = (((1,), (1,)), ((), ()))


def _cparams(sem):
    return pltpu.CompilerParams(dimension_semantics=sem, vmem_limit_bytes=VMEM_LIMIT)


def _dot(a, b):
    return jnp.dot(a, b, preferred_element_type=F32)


def _dot_nt(a, b):
    return lax.dot_general(a, b, _NT, preferred_element_type=F32)


def _rms_rows(x, g):
    x = x.astype(F32)
    ms = jnp.mean(x * x, axis=-1, keepdims=True)
    return x * lax.rsqrt(ms + RMS_EPS) * g


def _sigmoid(x):
    return 1.0 / (1.0 + jnp.exp(-x))


def _split_dot(x, m_bf16, terms, left=False):
    acc = None
    rem = x
    for t in range(terms):
        piece = rem.astype(BF16)
        d = _dot(m_bf16, piece) if left else _dot(piece, m_bf16)
        acc = d if acc is None else acc + d
        if t + 1 < terms:
            rem = rem - piece.astype(F32)
    return acc


def _head_ones():
    i = lax.broadcasted_iota(jnp.int32, (LANES, LANES), 0) // HEAD
    j = lax.broadcasted_iota(jnp.int32, (LANES, LANES), 1) // HEAD
    return jnp.where(i == j, 1.0, 0.0).astype(BF16)


def _mm_kernel(*refs, n_a, has_norm, keep_norm, swiglu, has_res, scale):
    refs = list(refs)
    a_refs = [refs.pop(0) for _ in range(n_a)]
    g_ref = refs.pop(0) if has_norm else None
    w_refs = [refs.pop(0) for _ in range(2 if swiglu else n_a)]
    res_ref = refs.pop(0) if has_res else None
    o_ref = refs.pop(0)
    if has_norm and keep_norm:
        xn_ref = refs.pop(0)

        @pl.when(pl.program_id(1) == 0)
        def _():
            xn_ref[...] = _rms_rows(a_refs[0][...], g_ref[...]).astype(BF16)

        lhs = [xn_ref[...]]
    elif has_norm:
        lhs = [_rms_rows(a_refs[0][...], g_ref[...]).astype(BF16)]
    else:
        lhs = [r[...] for r in a_refs]
    if swiglu:
        gate = _dot(lhs[0], w_refs[0][...])
        up = _dot(lhs[0], w_refs[1][...])
        acc = gate * _sigmoid(gate) * up
    else:
        acc = _dot(lhs[0], w_refs[0][...])
        for a, w in zip(lhs[1:], w_refs[1:]):
            acc = acc + _dot(a, w[...])
    if scale != 1.0:
        acc = acc * scale
    if has_res:
        acc = acc + res_ref[...]
    o_ref[...] = acc.astype(o_ref.dtype)


def _mm(a_list, w_list, *, name, norm_g=None, swiglu=False, res=None, scale=1.0, out_dtype=F32, tm=512, tn=None):
    m = a_list[0].shape[0]
    n = w_list[0].shape[1]
    tm = min(tm, m)
    resident = tn is None
    tn = n if resident else min(tn, n)
    assert m % tm == 0 and n % tn == 0
    has_norm = norm_g is not None
    in_specs = [pl.BlockSpec((tm, a.shape[1]), lambda i, j: (i, 0)) for a in a_list]
    args = list(a_list)
    if has_norm:
        in_specs.append(pl.BlockSpec((1, a_list[0].shape[1]), lambda i, j: (0, 0)))
        args.append(norm_g.reshape(1, -1))
    for w in w_list:
        if resident:
            in_specs.append(pl.BlockSpec((w.shape[0], n), lambda i, j: (0, 0), pipeline_mode=pl.Buffered(1)))
        else:
            in_specs.append(pl.BlockSpec((w.shape[0], tn), lambda i, j: (0, j)))
        args.append(w)
    if res is not None:
        in_specs.append(pl.BlockSpec((tm, tn), lambda i, j: (i, j)))
        args.append(res)
    keep_norm = has_norm and n // tn > 1
    scratch = [pltpu.VMEM((tm, a_list[0].shape[1]), BF16)] if keep_norm else []
    return pl.pallas_call(
        functools.partial(_mm_kernel, n_a=len(a_list), has_norm=has_norm, keep_norm=keep_norm, swiglu=swiglu,
                          has_res=res is not None, scale=scale),
        grid=(m // tm, n // tn),
        in_specs=in_specs,
        out_specs=pl.BlockSpec((tm, tn), lambda i, j: (i, j)),
        out_shape=jax.ShapeDtypeStruct((m, n), out_dtype),
        scratch_shapes=scratch,
        compiler_params=_cparams(("parallel", "arbitrary")),
        name=name,
    )(*args)


def _da_proj_kernel(x_ref, g_ref, w_ref, q_ref, kf_ref, vf_ref, kb_ref, vb_ref):
    xn = _rms_rows(x_ref[...], g_ref[...]).astype(BF16)
    q_ref[...] = (_dot(xn, w_ref[:, :DA_DIM]) * (DA_QK ** -0.5 * LOG2_E)).astype(BF16)
    k = _dot(xn, w_ref[:, DA_DIM:2 * DA_DIM])
    kf_ref[...] = k
    kb_ref[...] = k.astype(BF16)
    v = _dot(xn, w_ref[:, 2 * DA_DIM:])
    vf_ref[...] = v
    vb_ref[...] = v.astype(BF16)


def _da_proj(x, norm_g, w_da, tm=512):
    m, k = x.shape
    tm = min(tm, m)
    blk = pl.BlockSpec((tm, DA_DIM), lambda i: (i, 0))
    return pl.pallas_call(
        _da_proj_kernel,
        grid=(m // tm,),
        in_specs=[pl.BlockSpec((tm, k), lambda i: (i, 0)),
                  pl.BlockSpec((1, k), lambda i: (0, 0)),
                  pl.BlockSpec((k, 3 * DA_DIM), lambda i: (0, 0), pipeline_mode=pl.Buffered(1))],
        out_specs=[blk] * 5,
        out_shape=[jax.ShapeDtypeStruct((m, DA_DIM), dt) for dt in (BF16, F32, F32, BF16, BF16)],
        compiler_params=_cparams(("parallel",)),
        name="da_proj",
    )(x, norm_g.reshape(1, -1), w_da)


def _mm_down_kernel(a_ref, w_ref, res_ref, g_ref, o_ref, *, final_norm):
    kk = pl.program_id(1)

    @pl.when(kk == 0)
    def _():
        o_ref[...] = res_ref[...]

    o_ref[...] += _dot(a_ref[...], w_ref[...])

    if final_norm:
        @pl.when(kk == pl.num_programs(1) - 1)
        def _():
            o_ref[...] = _rms_rows(o_ref[...], g_ref[...])


def _mm_down(a, w, res, final_g, final_norm, tm=512, tk=512):
    m, k = a.shape
    n = w.shape[1]
    tm = min(tm, m)
    assert m % tm == 0 and k % tk == 0
    return pl.pallas_call(
        functools.partial(_mm_down_kernel, final_norm=final_norm),
        grid=(m // tm, k // tk),
        in_specs=[pl.BlockSpec((tm, tk), lambda i, kk: (i, kk)),
                  pl.BlockSpec((tk, n), lambda i, kk: (kk, 0)),
                  pl.BlockSpec((tm, n), lambda i, kk: (i, 0)),
                  pl.BlockSpec((1, n), lambda i, kk: (0, 0))],
        out_specs=pl.BlockSpec((tm, n), lambda i, kk: (i, 0)),
        out_shape=jax.ShapeDtypeStruct((m, n), F32),
        compiler_params=_cparams(("parallel", "arbitrary")),
        name="ffn_down",
    )(a, w, res, final_g.reshape(1, -1))


def _head_sum(x):
    lo = lax.broadcasted_iota(jnp.int32, (1, LANES), 1) < HEAD
    s_lo = jnp.sum(jnp.where(lo, x, 0.0), axis=-1, keepdims=True)
    s_all = jnp.sum(x, axis=-1, keepdims=True)
    return jnp.where(lo, s_lo, s_all - s_lo)


def _token_mix(p, p_prev, mu, w0, w2, a0, a2, g2, k_k, k_a):
    ps = p + (p_prev - p) * mu
    o1, o2, o3 = RWKV_DIM, 2 * RWKV_DIM, 3 * RWKV_DIM
    o4 = o3 + DECAY_LORA
    o5 = o4 + AAA_LORA
    r, k, v = ps[:, :o1], ps[:, o1:o2], ps[:, o2:o3]
    wd, ad, gd = ps[:, o3:o4], ps[:, o4:o5], ps[:, o5:]
    z = w0 + _dot(jnp.tanh(wd).astype(BF16), w2)
    w_log = -(jnp.maximum(-z, 0.0) + jnp.log(1.0 + jnp.exp(-jnp.abs(z)))) - 0.5
    lw = -jnp.exp(w_log)
    a = _sigmoid(a0 + _dot(ad.astype(BF16), a2))
    g = _dot(_sigmoid(gd).astype(BF16), g2)
    kraw = k * k_k
    kk = []
    for pr in range(PAIRS):
        kp = kraw[:, pr * LANES:(pr + 1) * LANES]
        kk.append(kp * lax.rsqrt(jnp.maximum(_head_sum(kp * kp), 1e-24)))
    return r, k * (1.0 + (a - 1.0) * k_a), v, kk, a, lw, g


def _mix_out(y, r, k2, v, g, r_k, ln_g, ln_b):
    mean = _head_sum(y) * (1.0 / HEAD)
    d = y - mean
    var = _head_sum(d * d) * (1.0 / HEAD)
    yn = d * lax.rsqrt(var + GN_EPS) * ln_g + ln_b
    return (yn + _head_sum(r * k2 * r_k) * v) * g


def _prep_kernel(p_ref, prev_ref, mu_ref, w0_ref, w2_ref, a0_ref, a2_ref, g2_ref, kk_ref, ka_ref,
                 r_o, k_o, v_o, kk_o, a_o, lw_o, g_o):
    r, k2, v, kk, a, lw, g = _token_mix(p_ref[...], prev_ref[...], mu_ref[...], w0_ref[...], w2_ref[...],
                                        a0_ref[...], a2_ref[...], g2_ref[...], kk_ref[...], ka_ref[...])
    for pr in range(PAIRS):
        kk_o[:, pr * LANES:(pr + 1) * LANES] = kk[pr]
    r_o[...] = r
    k_o[...] = k2
    v_o[...] = v
    a_o[...] = a
    lw_o[...] = lw
    g_o[...] = g


def _mix_params(lp):
    row = lambda x: x.reshape(1, -1)
    return (row(lp['mu']), row(lp['w0']), lp['w2'].astype(BF16), row(lp['a0']), lp['a2'].astype(BF16),
            lp['g2'].astype(BF16), row(lp['k_k']), row(lp['k_a']))


def _mix_param_specs(index_map):
    shapes = [(1, RWKV_PROJ), (1, RWKV_DIM), (DECAY_LORA, RWKV_DIM), (1, RWKV_DIM), (AAA_LORA, RWKV_DIM),
              (GATE_LORA, RWKV_DIM), (1, RWKV_DIM), (1, RWKV_DIM)]
    return [pl.BlockSpec(s, index_map) for s in shapes]


def _rwkv_prep(p, prev, lp):
    m = p.shape[0]
    blk_in = pl.BlockSpec((m, RWKV_PROJ), lambda i: (0, 0))
    out_blk = pl.BlockSpec((m, RWKV_DIM), lambda i: (0, 0))
    return pl.pallas_call(
        _prep_kernel,
        grid=(1,),
        in_specs=[blk_in, blk_in] + _mix_param_specs(lambda i: (0, 0)),
        out_specs=[out_blk] * 7,
        out_shape=[jax.ShapeDtypeStruct((m, RWKV_DIM), F32)] * 7,
        compiler_params=_cparams(("arbitrary",)),
        name="rwkv_prep",
    )(p, prev, *_mix_params(lp))


def _chunk_kernel(p_ref, mu_ref, w0_ref, w2_ref, a0_ref, a2_ref, g2_ref, kk_ref, ka_ref, rk_ref, lg_ref, lb_ref,
                  y_ref, s_out_ref, s_ref, prev_ref):
    c = pl.program_id(1)

    @pl.when(c == 0)
    def _():
        s_ref[...] = jnp.zeros_like(s_ref)
        prev_ref[...] = jnp.zeros_like(prev_ref)

    p = p_ref[...]
    rows = p.shape[0]
    n_chunks = rows // CHUNK
    first_row = lax.broadcasted_iota(jnp.int32, p.shape, 0) == 0
    p_prev = jnp.where(first_row, prev_ref[...], pltpu.roll(p, 1, 0))
    prev_ref[...] = p[rows - 1:rows, :]
    r_all, k_all, v_all, kk_all, a_all, lw_all, g_all = _token_mix(
        p, p_prev, mu_ref[...], w0_ref[...], w2_ref[...], a0_ref[...], a2_ref[...], g2_ref[...], kk_ref[...],
        ka_ref[...])

    lane = lax.broadcasted_iota(jnp.int32, (1, LANES), 1)
    m0 = jnp.where(lane < HEAD, 1.0, 0.0)
    m1 = 1.0 - m0
    ii = lax.broadcasted_iota(jnp.int32, (LANES, LANES), 0)
    jj = lax.broadcasted_iota(jnp.int32, (LANES, LANES), 1)
    strict = ii > jj
    incl = ii >= jj
    eye = jnp.where(ii == jj, 1.0, 0.0)
    ti = lax.broadcasted_iota(jnp.int32, (CHUNK, CHUNK), 0)
    tj = lax.broadcasted_iota(jnp.int32, (CHUNK, CHUNK), 1)
    tril = jnp.where(ti >= tj, 1.0, 0.0).astype(BF16)

    def stack(x):
        return jnp.concatenate([x * m0, x * m1], axis=0)

    pairs = range(PAIRS)
    items = [(ci, pr) for ci in range(n_chunks) for pr in pairs]
    tok = [slice(ci * CHUNK, (ci + 1) * CHUNK) for ci, _ in items]
    sls = [slice(pr * LANES, (pr + 1) * LANES) for _, pr in items]
    each = lambda fn, *cols: [fn(*xs) for xs in zip(*cols)]
    cut = lambda x: [x[t, sl] for t, sl in zip(tok, sls)]
    lw, k2, r, v = cut(lw_all), cut(k_all), cut(r_all), cut(v_all)
    kk = [kk_all[pr][t, :] for (_, pr), t in zip(items, tok)]
    bb = each(lambda x, y: x * y, kk, cut(a_all))
    cum_all = [_split_dot(lw_all[ci * CHUNK:(ci + 1) * CHUNK, :], tril, 3, left=True) for ci in range(n_chunks)]
    cum = [cum_all[ci][:, sl] for (ci, _), sl in zip(items, sls)]
    tot = [x[CHUNK - 1:CHUNK, :] for x in cum]
    inv_g = [jnp.exp(-x) for x in cum]
    to_end = each(lambda t, x: jnp.exp(t - x), tot, cum)
    a_t = each(lambda x, cm, l: stack(-x * jnp.exp(cm - l)).astype(BF16), kk, cum, lw)
    b_t = each(lambda x, g: stack(x * g).astype(BF16), bb, inv_g)
    k_t = each(lambda x, g: stack(x * g).astype(BF16), k2, inv_g)
    r_f = each(lambda x, cm: stack(x * jnp.exp(cm)), r, cum)
    r_t = [x.astype(BF16) for x in r_f]
    v_f = [stack(x) for x in v]
    v_s = [x.astype(BF16) for x in v_f]
    bk_end = each(lambda x, y, e: jnp.concatenate([stack(x * e), stack(y * e)], axis=0).astype(BF16), bb, k2, to_end)

    a_ab = each(lambda x, y: jnp.where(strict, _dot_nt(x, y), 0.0), a_t, b_t)
    a_ak = each(lambda x, y: jnp.where(strict, _dot_nt(x, y), 0.0).astype(BF16), a_t, k_t)
    a_rb = each(lambda x, y: jnp.where(incl, _dot_nt(x, y), 0.0).astype(BF16), r_t, b_t)
    a_rk = each(lambda x, y: jnp.where(incl, _dot_nt(x, y), 0.0).astype(BF16), r_t, k_t)
    inv = [eye + x for x in a_ab]
    power = [x.astype(BF16) for x in a_ab]
    for _ in range(5):
        power = [_dot(x, x).astype(BF16) for x in power]
        inv = each(lambda x, p: x + _dot(x.astype(BF16), p), inv, power)
    w1 = each(lambda x, y: _dot(x, y).astype(BF16), a_ak, v_s)
    pu = each(lambda x, y, z: _dot(x.astype(BF16), jnp.concatenate([y, z], axis=1)), inv, a_t, w1)
    pu_b = [x.astype(BF16) for x in pu]
    qy = each(_dot, a_rb, pu_b)
    q_m = each(lambda x, y: (x + y[:, :LANES]).astype(BF16), r_f, qy)
    y_v = each(lambda x, y, z: x[:, LANES:] + _dot(y, z), qy, a_rk, v_s)
    pq = each(lambda x, y: jnp.concatenate([x[:, :LANES], y], axis=0), pu_b, q_m)
    state = [s_ref[pr] for pr in pairs]
    g_cut = cut(g_all)
    for ci in range(n_chunks):
        grp = slice(ci * PAIRS, (ci + 1) * PAIRS)
        uy = each(lambda x, s: _dot_nt(x, s.astype(BF16)), pq[grp], state)
        uv_t = each(lambda x, y, z: jnp.concatenate([x[:LANES] + y[:, LANES:], z], axis=0).T.astype(BF16),
                    uy, pu[grp], v_f[grp])
        state = each(lambda s, t, x, y: s * jnp.exp(t) + _dot(x, y), state, tot[grp], uv_t, bk_end[grp])
        for pr in pairs:
            i = ci * PAIRS + pr
            y_m = uy[pr][LANES:] + y_v[i]
            out = _mix_out(y_m[:CHUNK] + y_m[CHUNK:], r[i], k2[i], v[i], g_cut[i], rk_ref[:, sls[i]],
                           lg_ref[:, sls[i]], lb_ref[:, sls[i]])
            y_ref[tok[i], sls[i]] = out.astype(y_ref.dtype)
    for pr in pairs:
        s_ref[pr] = state[pr]

    @pl.when(c == pl.num_programs(1) - 1)
    def _():
        s_out_ref[0] = s_ref[...]


def _rwkv_chunk_scan(p, lp, batch):
    m = p.shape[0]
    n_chunks = m // batch // CHUNK
    cps = next(c for c in (4, 2, 1) if n_chunks % c == 0)
    nc = n_chunks // cps
    row = lambda x: x.reshape(1, -1)
    const = lambda b, c: (0, 0)
    vec = pl.BlockSpec((1, RWKV_DIM), const)
    y, s_pairs = pl.pallas_call(
        _chunk_kernel,
        grid=(batch, nc),
        in_specs=([pl.BlockSpec((cps * CHUNK, RWKV_PROJ), lambda b, c: (b * nc + c, 0))] + _mix_param_specs(const)
                  + [vec] * 3),
        out_specs=[pl.BlockSpec((cps * CHUNK, RWKV_DIM), lambda b, c: (b * nc + c, 0)),
                   pl.BlockSpec((1, PAIRS, LANES, LANES), lambda b, c: (b, 0, 0, 0))],
        out_shape=[jax.ShapeDtypeStruct((m, RWKV_DIM), BF16),
                   jax.ShapeDtypeStruct((batch, PAIRS, LANES, LANES), F32)],
        scratch_shapes=[pltpu.VMEM((PAIRS, LANES, LANES), F32), pltpu.VMEM((1, RWKV_PROJ), F32)],
        compiler_params=_cparams(("parallel", "arbitrary")),
        name="rwkv_chunk",
    )(p, *_mix_params(lp), row(lp['r_k']), row(lp['lnx_g']), row(lp['lnx_b']))
    s_fin = jnp.stack([s_pairs[:, :, :HEAD, :HEAD], s_pairs[:, :, HEAD:, HEAD:]], axis=2)
    return y, s_fin.reshape(batch, RWKV_HEADS, HEAD, HEAD)


def _step_kernel(s_ref, r_ref, k_ref, kk_ref, a_ref, lw_ref, v_ref, s_o, y_o):
    s = s_ref[0]
    kk = kk_ref[0]
    sa = jnp.sum(s * (-kk), axis=-1, keepdims=True)
    s_new = s * jnp.exp(lw_ref[0]) + sa * (kk * a_ref[0]) + v_ref[0] * k_ref[0]
    s_o[0] = s_new
    y_o[0] = jnp.sum(s_new * r_ref[0], axis=-1, keepdims=True)


def _rwkv_step(s0, r, k2, v, kk, a, lw):
    b = s0.shape[0]
    rowv = lambda x: x.reshape(b, RWKV_HEADS, 1, HEAD)
    row_spec = pl.BlockSpec((1, RWKV_HEADS, 1, HEAD), lambda i: (i, 0, 0, 0))
    col_spec = pl.BlockSpec((1, RWKV_HEADS, HEAD, 1), lambda i: (i, 0, 0, 0))
    s_spec = pl.BlockSpec((1, RWKV_HEADS, HEAD, HEAD), lambda i: (i, 0, 0, 0))
    s_new, y = pl.pallas_call(
        _step_kernel,
        grid=(b,),
        in_specs=[s_spec] + [row_spec] * 5 + [col_spec],
        out_specs=[s_spec, col_spec],
        out_shape=[jax.ShapeDtypeStruct(s0.shape, F32), jax.ShapeDtypeStruct((b, RWKV_HEADS, HEAD, 1), F32)],
        compiler_params=_cparams(("parallel",)),
        name="rwkv_step",
    )(s0, rowv(r), rowv(k2), rowv(kk), rowv(a), rowv(lw), v.reshape(b, RWKV_HEADS, HEAD, 1))
    return y.reshape(b, RWKV_DIM), s_new


def _post_kernel(y_ref, r_ref, k_ref, v_ref, g_ref, rk_ref, lg_ref, lb_ref, o_ref):
    for pr in range(PAIRS):
        sl = slice(pr * LANES, (pr + 1) * LANES)
        out = _mix_out(y_ref[:, sl], r_ref[:, sl], k_ref[:, sl], v_ref[:, sl], g_ref[:, sl], rk_ref[:, sl],
                       lg_ref[:, sl], lb_ref[:, sl])
        o_ref[:, sl] = out.astype(o_ref.dtype)


def _rwkv_post(y, r, k2, v, g, lp):
    m = y.shape[0]
    blk = pl.BlockSpec((m, RWKV_DIM), lambda i: (0, 0))
    par = pl.BlockSpec((1, RWKV_DIM), lambda i: (0, 0))
    return pl.pallas_call(
        _post_kernel,
        grid=(1,),
        in_specs=[blk] * 5 + [par] * 3,
        out_specs=blk,
        out_shape=jax.ShapeDtypeStruct((m, RWKV_DIM), BF16),
        compiler_params=_cparams(("arbitrary",)),
        name="rwkv_post",
    )(y, r, k2, v, g, lp['r_k'].reshape(1, -1), lp['lnx_g'].reshape(1, -1), lp['lnx_b'].reshape(1, -1))


def _lambda(lq1_ref, lk1_ref, lq2_ref, lk2_ref, lam_init):
    s1 = jnp.sum(lq1_ref[...] * lk1_ref[...], axis=-1, keepdims=True)
    s2 = jnp.sum(lq2_ref[...] * lk2_ref[...], axis=-1, keepdims=True)
    return jnp.exp(s1) - jnp.exp(s2) + lam_init


def _dattn_kernel(lq1_ref, lk1_ref, lq2_ref, lk2_ref, sg_ref, q_ref, k_ref, v_ref, o_ref,
                  acc_ref, *, tq, lam_init):
    nq = q_ref.shape[0] // tq
    lane = lax.broadcasted_iota(jnp.int32, (1, LANES), 1)
    row = lax.broadcasted_iota(jnp.int32, (tq, tq), 0)
    col = lax.broadcasted_iota(jnp.int32, (tq, tq), 1)
    ones = jnp.ones((tq, LANES), BF16)
    lam = _lambda(lq1_ref, lk1_ref, lq2_ref, lk2_ref, lam_init)
    rows = lambda ref, i: ref[i * tq:(i + 1) * tq, :]

    def scores(qi, j):
        q = rows(q_ref, qi)
        zero = jnp.zeros_like(q)
        ks = rows(k_ref, j)
        return [_dot_nt(jnp.where(lane < DA_QK, q, zero), ks), _dot_nt(jnp.where(lane >= DA_QK, q, zero), ks)]

    blocks = [(qi, j) for qi in range(nq) for j in range(qi + 1)]
    n_blocks = len(blocks)
    s_buf, p_buf = {}, {}
    m_run = None
    for t in range(n_blocks + 2):
        if t < n_blocks:
            s_buf[t] = scores(*blocks[t])
        pv = None
        if 0 <= t - 2 < n_blocks:
            p, p_alpha = p_buf.pop(t - 2)
            v_aug = jnp.concatenate([rows(v_ref, blocks[t - 2][1]), ones], axis=1)
            pv = [_dot(x, v_aug) for x in p]
        if 0 <= t - 1 < n_blocks:
            qi, j = blocks[t - 1]
            s = s_buf.pop(t - 1)
            if j == qi:
                s = [jnp.where(col <= row, x, NEG_INF) for x in s]
            row_max = [jnp.max(x, axis=-1, keepdims=True) for x in s]
            if j == 0:
                m_new, alpha = row_max, None
            else:
                m_new = [jnp.maximum(mo, mx) for mo, mx in zip(m_run, row_max)]
                alpha = [jnp.exp2(mo - mn) for mo, mn in zip(m_run, m_new)]
            p_buf[t - 1] = ([jnp.exp2(x - mn).astype(BF16) for x, mn in zip(s, m_new)], alpha)
            m_run = m_new
        if pv is not None:
            qi, j = blocks[t - 2]
            acc = pv if j == 0 else [alpha_c * acc_ref[c] + pv[c] for c, alpha_c in enumerate(p_alpha)]
            if j == qi:
                a0, a1 = acc
                o = a0[:, :LANES] / a0[:, LANES:] - lam * (a1[:, :LANES] / a1[:, LANES:])
                o_ref[qi * tq:(qi + 1) * tq, :] = (_rms_rows(o, sg_ref[...]) * (1.0 - lam_init)).astype(o_ref.dtype)
            else:
                for c in range(2):
                    acc_ref[c] = acc[c]


def _diff_attn_prompt(qb, kb, vb, lp, lam_init, batch, tq=512):
    m = qb.shape[0]
    t = m // batch
    tq = min(tq, t)
    vec = pl.BlockSpec((1, DA_QK), lambda b, h: (0, 0))
    seq_spec = pl.BlockSpec((t, LANES), lambda b, h: (b, h))
    return pl.pallas_call(
        functools.partial(_dattn_kernel, tq=tq, lam_init=lam_init),
        grid=(batch, DA_HEADS),
        in_specs=[vec] * 4 + [pl.BlockSpec((1, DA_V), lambda b, h: (0, 0)), seq_spec, seq_spec, seq_spec],
        out_specs=seq_spec,
        out_shape=jax.ShapeDtypeStruct((m, DA_DIM), BF16),
        scratch_shapes=[pltpu.VMEM((2, tq, 2 * LANES), F32)],
        compiler_params=_cparams(("parallel", "parallel")),
        name="diff_attn_prompt",
    )(*[lp[n].reshape(1, -1) for n in ('lq1', 'lk1', 'lq2', 'lk2')], lp['subln_g'].reshape(1, -1), qb, kb, vb)


def _paged_kernel(pt_ref, lq1_ref, lk1_ref, lq2_ref, lk2_ref, sg_ref, q_ref, kn_ref, vn_ref, *rest, pps, lam_init):
    del pt_ref
    k_refs, v_refs = rest[:pps], rest[pps:2 * pps]
    o_ref, m_ref, l_ref, acc_ref = rest[2 * pps:]
    pg = pl.program_id(1)
    n_rows = 2 * DA_HEADS

    @pl.when(pg == 0)
    def _():
        m_ref[...] = jnp.full_like(m_ref, NEG_INF)
        l_ref[...] = jnp.zeros_like(l_ref)
        acc_ref[...] = jnp.zeros_like(acc_ref)

    q = q_ref[0]
    lane = lax.broadcasted_iota(jnp.int32, (1, LANES), 1)
    q_rows = jnp.concatenate([jnp.where(lane < DA_QK, q, 0.0), jnp.where(lane >= DA_QK, q, 0.0)], axis=0)
    q_bf = q_rows.astype(BF16)
    cols = k_refs[0].shape[1] * DA_HEADS
    same_head = ((lax.broadcasted_iota(jnp.int32, (n_rows, cols), 1) & (DA_HEADS - 1))
                 == (lax.broadcasted_iota(jnp.int32, (n_rows, cols), 0) & (DA_HEADS - 1)))
    flat = lambda ref: ref[0].reshape(cols, LANES).astype(BF16)

    m_run, l_run, acc = m_ref[...], l_ref[...], acc_ref[...]
    s_buf, p_buf = {}, {}
    for t in range(pps + 2):
        if t < pps:
            s_buf[t] = _dot_nt(q_bf, flat(k_refs[t]))
        pv = None
        if 0 <= t - 2 < pps:
            p, p_alpha = p_buf.pop(t - 2)
            pv = _dot(p, flat(v_refs[t - 2]))
        if 0 <= t - 1 < pps:
            s = jnp.where(same_head, s_buf.pop(t - 1), NEG_INF)
            m_new = jnp.maximum(m_run, jnp.max(s, axis=-1, keepdims=True))
            alpha = jnp.exp2(m_run - m_new)
            p = jnp.exp2(s - m_new)
            l_run = alpha * l_run + jnp.sum(p, axis=-1, keepdims=True)
            p_buf[t - 1] = (p.astype(BF16), alpha)
            m_run = m_new
        if pv is not None:
            acc = p_alpha * acc + pv
    m_ref[...] = m_run
    l_ref[...] = l_run
    acc_ref[...] = acc

    @pl.when(pg == pl.num_programs(1) - 1)
    def _():
        k_new = jnp.concatenate([kn_ref[0]] * 2, axis=0)
        v_new = jnp.concatenate([vn_ref[0]] * 2, axis=0)
        s_new = jnp.sum(q_rows * k_new, axis=-1, keepdims=True)
        m_fin = jnp.maximum(m_run, s_new)
        alpha = jnp.exp2(m_run - m_fin)
        p_new = jnp.exp2(s_new - m_fin)
        out = (alpha * acc + p_new * v_new) / (alpha * l_run + p_new)
        lam = _lambda(lq1_ref, lk1_ref, lq2_ref, lk2_ref, lam_init)
        o = out[:DA_HEADS] - lam * out[DA_HEADS:]
        o_ref[0] = (_rms_rows(o, sg_ref[...]) * (1.0 - lam_init)).astype(o_ref.dtype)


def _diff_attn_paged(q, k_new, v_new, cache_k, cache_v, page_table, lp, lam_init):
    b, n_pages = page_table.shape
    page = cache_k.shape[1]
    pps = next(c for c in (16, 8, 4, 2, 1) if n_pages % c == 0)
    vec = pl.BlockSpec((1, DA_QK), lambda i, p, pt: (0, 0))
    rowb = pl.BlockSpec((1, DA_HEADS, LANES), lambda i, p, pt: (i, 0, 0))

    def page_spec(j):
        return pl.BlockSpec((1, page, DA_HEADS, LANES),
                            lambda i, p, pt: (pt[i * n_pages + p * pps + j], 0, 0, 0))

    grid_spec = pltpu.PrefetchScalarGridSpec(
        num_scalar_prefetch=1,
        grid=(b, n_pages // pps),
        in_specs=([vec] * 4 + [pl.BlockSpec((1, DA_V), lambda i, p, pt: (0, 0)), rowb, rowb, rowb]
                  + [page_spec(j) for j in range(pps)] * 2),
        out_specs=rowb,
        scratch_shapes=[pltpu.VMEM((2 * DA_HEADS, 1), F32)] * 2 + [pltpu.VMEM((2 * DA_HEADS, LANES), F32)],
    )
    r3 = lambda x: x.reshape(b, DA_HEADS, LANES)
    out = pl.pallas_call(
        functools.partial(_paged_kernel, pps=pps, lam_init=lam_init),
        grid_spec=grid_spec,
        out_shape=jax.ShapeDtypeStruct((b, DA_HEADS, LANES), BF16),
        compiler_params=_cparams(("parallel", "arbitrary")),
        name="diff_attn_paged",
    )(page_table.reshape(-1), *[lp[n].reshape(1, -1) for n in ('lq1', 'lk1', 'lq2', 'lk2')],
      lp['subln_g'].reshape(1, -1), r3(q), r3(k_new), r3(v_new), *([cache_k] * pps), *([cache_v] * pps))
    return out.reshape(b, DA_DIM)


def _cross_kernel(q_ref, mk_ref, mv_ref, o_ref):
    sls = [slice(h * CROSS_HD, (h + 1) * CROSS_HD) for h in range(CROSS_HEADS)]
    s = [_dot_nt(q_ref[0, :, sl], mk_ref[0, :, sl].astype(BF16)) for sl in sls]
    p = [jnp.exp(x - jnp.max(x, axis=-1, keepdims=True)) for x in s]
    l = [jnp.sum(x, axis=-1, keepdims=True) for x in p]
    o = [_dot(x.astype(BF16), mv_ref[0, :, sl].astype(BF16)) for x, sl in zip(p, sls)]
    for sl, x, d in zip(sls, o, l):
        o_ref[0, :, sl] = (x / d).astype(o_ref.dtype)


def _cross_decode_kernel(q_ref, mk_ref, mv_ref, o_ref):
    q = q_ref[0].astype(F32)
    s = jnp.sum(mk_ref[0] * q, axis=-1, keepdims=True)
    p = jnp.exp(s - jnp.max(s, axis=0, keepdims=True))
    o = jnp.sum(p * mv_ref[0], axis=0) / jnp.sum(p, axis=0)
    o_ref[0] = o.astype(o_ref.dtype)


def _cross_attn_decode(q, mk, mv):
    b = q.shape[0]
    mt = mk.shape[1]
    q_spec = pl.BlockSpec((1, CROSS_HEADS, CROSS_HD), lambda i: (i, 0, 0))
    m_spec = pl.BlockSpec((1, mt, CROSS_HEADS, CROSS_HD), lambda i: (i, 0, 0, 0))
    out = pl.pallas_call(
        _cross_decode_kernel,
        grid=(b,),
        in_specs=[q_spec, m_spec, m_spec],
        out_specs=q_spec,
        out_shape=jax.ShapeDtypeStruct((b, CROSS_HEADS, CROSS_HD), BF16),
        compiler_params=_cparams(("parallel",)),
        name="cross_attn_decode",
    )(q.reshape(b, CROSS_HEADS, CROSS_HD), mk, mv)
    return out.reshape(b, D_MODEL)


def _cross_attn(q, mk, mv, tq=512):
    b, t, _ = q.shape
    mt = mk.shape[1]
    tq = min(tq, t)
    q_spec = pl.BlockSpec((1, tq, D_MODEL), lambda i, j: (i, j, 0))
    m_spec = pl.BlockSpec((1, mt, D_MODEL), lambda i, j: (i, 0, 0))
    return pl.pallas_call(
        _cross_kernel,
        grid=(b, t // tq),
        in_specs=[q_spec, m_spec, m_spec],
        out_specs=q_spec,
        out_shape=jax.ShapeDtypeStruct(q.shape, BF16),
        compiler_params=_cparams(("parallel", "parallel")),
        name="cross_attn",
    )(q, mk, mv)


def _layer(x, batch, lp, wts, lam_init, mk, mv, rwkv_fn, attn_fn):
    m = x.shape[0]
    t = m // batch
    pr = _mm([x], [wts['in_r']], name="in_proj_rwkv", norm_g=lp['norm_mix_g'], tm=512)
    qb, kf, vf, kb, vb = _da_proj(x, lp['norm_mix_g'], wts['in_da'])
    y_r, s_new = rwkv_fn(pr)
    y_d = attn_fn(qb, kf, vf, kb, vb)
    x = _mm([y_r, y_d], [wts['o_r'], wts['o_d']], name="out_proj", res=x, tm=512)
    qc = _mm([x], [wts['cq']], name="cross_q", norm_g=lp['norm_cross_g'], scale=CROSS_HD ** -0.5, out_dtype=BF16,
             tm=1024)
    if mk.ndim == 4:
        oc = _cross_attn_decode(qc, mk, mv)
    else:
        oc = _cross_attn(qc.reshape(batch, t, D_MODEL), mk, mv).reshape(m, D_MODEL)
    x = _mm([oc], [wts['co']], name="cross_out", res=x, tm=512)
    hmid = _mm([x], [wts['gate'], wts['up']], name="ffn_swiglu", norm_g=lp['norm_ffn_g'], swiglu=True,
               out_dtype=BF16, tm=1024, tn=512)
    x = _mm_down(hmid, wts['down'], x, wts['final_g'], wts['is_last'], tm=1024, tk=hmid.shape[1] // 4)
    last_row = pr.reshape(batch, t, RWKV_PROJ)[:, -1]
    return x, kf, vf, s_new, last_row


def kernel(x_prompt, x_sample, cache_k, cache_v, cache_mem_k, cache_mem_v, state_wkv, state_shift, page_table, mem_prompt, norm_mix_g, w_in, tok_shift_mu, rwkv_w0, rwkv_w2, rwkv_a0, rwkv_a2, rwkv_g2, rwkv_k_k, rwkv_k_a, rwkv_r_k, rwkv_lnx_g, rwkv_lnx_b, diff_lq1, diff_lk1, diff_lq2, diff_lk2, diff_subln_g, w_o, norm_cross_g, norm_mem_g, w_cq, w_ck, w_cv, w_co, norm_ffn_g, w_gate, w_up, w_down, final_norm_g):
    b, t, _ = x_prompt.shape
    db, dt, _ = x_sample.shape
    depth = w_in.shape[0]
    n_mem = mem_prompt.shape[1]
    n_pool, page = cache_k.shape[1], cache_k.shape[2]
    xp = x_prompt.reshape(b * t, D_MODEL)
    xs = x_sample.reshape(db * dt, D_MODEL)
    outs = {n: [] for n in ('k_p', 'v_p', 'k_s', 'v_s', 'S_p', 'S_s', 'sh_p', 'sh_s', 'mk_p', 'mv_p')}
    for l in range(depth):
        lam_init = 0.8 - 0.6 * math.exp(-0.3 * l)
        lp = dict(norm_mix_g=norm_mix_g[l], mu=tok_shift_mu[l], w0=rwkv_w0[l], w2=rwkv_w2[l], a0=rwkv_a0[l],
                  a2=rwkv_a2[l], g2=rwkv_g2[l], k_k=rwkv_k_k[l], k_a=rwkv_k_a[l], r_k=rwkv_r_k[l],
                  lnx_g=rwkv_lnx_g[l], lnx_b=rwkv_lnx_b[l], lq1=diff_lq1[l], lk1=diff_lk1[l], lq2=diff_lq2[l],
                  lk2=diff_lk2[l], subln_g=diff_subln_g[l], norm_cross_g=norm_cross_g[l], norm_ffn_g=norm_ffn_g[l])
        wl = w_in[l].astype(BF16)
        wts = dict(in_r=wl[:, :RWKV_PROJ], in_da=wl[:, RWKV_PROJ:], o_r=w_o[l][:RWKV_DIM].astype(BF16),
                   o_d=w_o[l][RWKV_DIM:].astype(BF16), cq=w_cq[l].astype(BF16), co=w_co[l].astype(BF16),
                   gate=w_gate[l].astype(BF16), up=w_up[l].astype(BF16), down=w_down[l].astype(BF16),
                   final_g=final_norm_g, is_last=l == depth - 1)

        mem2 = mem_prompt.reshape(b * n_mem, D_MODEL)
        mk = _mm([mem2], [w_ck[l].astype(BF16)], name="mem_k", norm_g=norm_mem_g[l], tm=1024)
        mv = _mm([mem2], [w_cv[l].astype(BF16)], name="mem_v", norm_g=norm_mem_g[l], tm=1024)

        def rwkv_prompt(pr):
            return _rwkv_chunk_scan(pr, lp, b)

        def attn_prompt(qb, kf, vf, kb, vb):
            return _diff_attn_prompt(qb, kb, vb, lp, lam_init, b)

        xp, kf, vf, s_new, last = _layer(xp, b, lp, wts, lam_init, mk.reshape(b, n_mem, D_MODEL),
                                         mv.reshape(b, n_mem, D_MODEL), rwkv_prompt, attn_prompt)
        outs['k_p'].append(kf.reshape(b, t, DA_HEADS, 2 * DA_QK))
        outs['v_p'].append(vf.reshape(b, t, DA_HEADS, DA_V))
        outs['S_p'].append(s_new)
        outs['sh_p'].append(last)
        outs['mk_p'].append(mk.reshape(b, n_mem, CROSS_HEADS, CROSS_HD))
        outs['mv_p'].append(mv.reshape(b, n_mem, CROSS_HEADS, CROSS_HD))

        assert dt == 1

        def rwkv_sample(pr):
            r, k2, v, kk, a, lw, g = _rwkv_prep(pr, state_shift[l], lp)
            y, s_fin = _rwkv_step(state_wkv[l], r, k2, v, kk, a, lw)
            return _rwkv_post(y, r, k2, v, g, lp), s_fin

        def attn_sample(qb, kf, vf, kb, vb):
            return _diff_attn_paged(qb.astype(F32), kf, vf, cache_k[l], cache_v[l], page_table, lp, lam_init)

        xs, kf, vf, s_new, last = _layer(xs, db, lp, wts, lam_init, cache_mem_k[l], cache_mem_v[l],
                                         rwkv_sample, attn_sample)
        outs['k_s'].append(kf.reshape(db, dt, DA_HEADS, 2 * DA_QK))
        outs['v_s'].append(vf.reshape(db, dt, DA_HEADS, DA_V))
        outs['S_s'].append(s_new)
        outs['sh_s'].append(last)
    y_prompt = xp.reshape(b, t, D_MODEL)
    y_sample = xs.reshape(db, dt, D_MODEL)
    st = lambda n: jnp.stack(outs[n])
    return (y_prompt, y_sample, st('k_p'), st('v_p'), st('k_s'), st('v_s'), st('S_p'), st('S_s'),
            st('sh_p'), st('sh_s'), st('mk_p'), st('mv_p'))
```

```python
import functools
import math

import jax
import jax.numpy as jnp
from jax import lax
from jax.experimental import pallas as pl
from jax.experimental.pallas import tpu as pltpu

F32 = jnp.float32
BF16 = jnp.bfloat16

D_MODEL = 2048
RWKV_DIM = 1024
HEAD = 64
RWKV_HEADS = 16
DECAY_LORA = 64
AAA_LORA = 64
GATE_LORA = 128
RWKV_PROJ = 3 * RWKV_DIM + DECAY_LORA + AAA_LORA + GATE_LORA
DA_DIM = 1024
DA_QK = 64
DA_V = 128
DA_HEADS = 8
CROSS_HEADS = 4
CROSS_HD = D_MODEL // CROSS_HEADS
RMS_EPS = 1e-6
GN_EPS = 64e-5
NEG_INF = -1e30
LOG2_E = 1.4426950408889634

LANES = 128
CHUNK = 64
PAIRS = RWKV_HEADS // 2
VMEM_LIMIT = 56 * 1024 * 1024

_NT = (((1,), (1,)), ((), ()))


def _cparams(sem):
    return pltpu.CompilerParams(dimension_semantics=sem, vmem_limit_bytes=VMEM_LIMIT)


def _dot(a, b):
    return jnp.dot(a, b, preferred_element_type=F32)


def _dot_nt(a, b):
    return lax.dot_general(a, b, _NT, preferred_element_type=F32)


def _rms_rows(x, g):
    x = x.astype(F32)
    ms = jnp.mean(x * x, axis=-1, keepdims=True)
    return x * lax.rsqrt(ms + RMS_EPS) * g


def _sigmoid(x):
    return 1.0 / (1.0 + jnp.exp(-x))


def _split_dot(x, m_bf16, terms, left=False):
    acc = None
    rem = x
    for t in range(terms):
        piece = rem.astype(BF16)
        d = _dot(m_bf16, piece) if left else _dot(piece, m_bf16)
        acc = d if acc is None else acc + d
        if t + 1 < terms:
            rem = rem - piece.astype(F32)
    return acc


def _head_ones():
    i = lax.broadcasted_iota(jnp.int32, (LANES, LANES), 0) // HEAD
    j = lax.broadcasted_iota(jnp.int32, (LANES, LANES), 1) // HEAD
    return jnp.where(i == j, 1.0, 0.0).astype(BF16)


def _mm_kernel(*refs, n_a, has_norm, keep_norm, swiglu, has_res, scale):
    refs = list(refs)
    a_refs = [refs.pop(0) for _ in range(n_a)]
    g_ref = refs.pop(0) if has_norm else None
    w_refs = [refs.pop(0) for _ in range(2 if swiglu else n_a)]
    res_ref = refs.pop(0) if has_res else None
    o_ref = refs.pop(0)
    if has_norm and keep_norm:
        xn_ref = refs.pop(0)

        @pl.when(pl.program_id(1) == 0)
        def _():
            xn_ref[...] = _rms_rows(a_refs[0][...], g_ref[...]).astype(BF16)

        lhs = [xn_ref[...]]
    elif has_norm:
        lhs = [_rms_rows(a_refs[0][...], g_ref[...]).astype(BF16)]
    else:
        lhs = [r[...] for r in a_refs]
    if swiglu:
        gate = _dot(lhs[0], w_refs[0][...])
        up = _dot(lhs[0], w_refs[1][...])
        acc = gate * _sigmoid(gate) * up
    else:
        acc = _dot(lhs[0], w_refs[0][...])
        for a, w in zip(lhs[1:], w_refs[1:]):
            acc = acc + _dot(a, w[...])
    if scale != 1.0:
        acc = acc * scale
    if has_res:
        acc = acc + res_ref[...]
    o_ref[...] = acc.astype(o_ref.dtype)


def _mm(a_list, w_list, *, name, norm_g=None, swiglu=False, res=None, scale=1.0, out_dtype=F32, tm=512, tn=None):
    m = a_list[0].shape[0]
    n = w_list[0].shape[1]
    tm = min(tm, m)
    resident = tn is None and m // tm > 1
    if tn is None and not resident:
        tn = 512 if n % 512 == 0 else 256
    tn = n if resident else min(tn, n)
    assert m % tm == 0 and n % tn == 0
    has_norm = norm_g is not None
    in_specs = [pl.BlockSpec((tm, a.shape[1]), lambda i, j: (i, 0)) for a in a_list]
    args = list(a_list)
    if has_norm:
        in_specs.append(pl.BlockSpec((1, a_list[0].shape[1]), lambda i, j: (0, 0)))
        args.append(norm_g.reshape(1, -1))
    for w in w_list:
        if resident:
            in_specs.append(pl.BlockSpec((w.shape[0], n), lambda i, j: (0, 0), pipeline_mode=pl.Buffered(1)))
        else:
            in_specs.append(pl.BlockSpec((w.shape[0], tn), lambda i, j: (0, j)))
        args.append(w)
    if res is not None:
        in_specs.append(pl.BlockSpec((tm, tn), lambda i, j: (i, j)))
        args.append(res)
    keep_norm = has_norm and n // tn > 1
    scratch = [pltpu.VMEM((tm, a_list[0].shape[1]), BF16)] if keep_norm else []
    return pl.pallas_call(
        functools.partial(_mm_kernel, n_a=len(a_list), has_norm=has_norm, keep_norm=keep_norm, swiglu=swiglu,
                          has_res=res is not None, scale=scale),
        grid=(m // tm, n // tn),
        in_specs=in_specs,
        out_specs=pl.BlockSpec((tm, tn), lambda i, j: (i, j)),
        out_shape=jax.ShapeDtypeStruct((m, n), out_dtype),
        scratch_shapes=scratch,
        compiler_params=_cparams(("parallel", "arbitrary")),
        name=name,
    )(*args)


def _da_proj_kernel(x_ref, g_ref, w_ref, q_ref, kf_ref, vf_ref, kb_ref, vb_ref):
    xn = _rms_rows(x_ref[...], g_ref[...]).astype(BF16)
    q_ref[...] = (_dot(xn, w_ref[:, :DA_DIM]) * (DA_QK ** -0.5 * LOG2_E)).astype(BF16)
    k = _dot(xn, w_ref[:, DA_DIM:2 * DA_DIM])
    kf_ref[...] = k
    kb_ref[...] = k.astype(BF16)
    v = _dot(xn, w_ref[:, 2 * DA_DIM:])
    vf_ref[...] = v
    vb_ref[...] = v.astype(BF16)


def _da_proj(x, norm_g, w_da, tm=512):
    m, k = x.shape
    tm = min(tm, m)
    blk = pl.BlockSpec((tm, DA_DIM), lambda i: (i, 0))
    return pl.pallas_call(
        _da_proj_kernel,
        grid=(m // tm,),
        in_specs=[pl.BlockSpec((tm, k), lambda i: (i, 0)),
                  pl.BlockSpec((1, k), lambda i: (0, 0)),
                  pl.BlockSpec((k, 3 * DA_DIM), lambda i: (0, 0), pipeline_mode=pl.Buffered(1))],
        out_specs=[blk] * 5,
        out_shape=[jax.ShapeDtypeStruct((m, DA_DIM), dt) for dt in (BF16, F32, F32, BF16, BF16)],
        compiler_params=_cparams(("parallel",)),
        name="da_proj",
    )(x, norm_g.reshape(1, -1), w_da)


def _mm_down_kernel(a_ref, w_ref, res_ref, g_ref, o_ref, *, final_norm):
    kk = pl.program_id(1)

    @pl.when(kk == 0)
    def _():
        o_ref[...] = res_ref[...]

    o_ref[...] += _dot(a_ref[...], w_ref[...])

    if final_norm:
        @pl.when(kk == pl.num_programs(1) - 1)
        def _():
            o_ref[...] = _rms_rows(o_ref[...], g_ref[...])


def _mm_down(a, w, res, final_g, final_norm, tm=512, tk=512):
    m, k = a.shape
    n = w.shape[1]
    tm = min(tm, m)
    assert m % tm == 0 and k % tk == 0
    return pl.pallas_call(
        functools.partial(_mm_down_kernel, final_norm=final_norm),
        grid=(m // tm, k // tk),
        in_specs=[pl.BlockSpec((tm, tk), lambda i, kk: (i, kk)),
                  pl.BlockSpec((tk, n), lambda i, kk: (kk, 0)),
                  pl.BlockSpec((tm, n), lambda i, kk: (i, 0)),
                  pl.BlockSpec((1, n), lambda i, kk: (0, 0))],
        out_specs=pl.BlockSpec((tm, n), lambda i, kk: (i, 0)),
        out_shape=jax.ShapeDtypeStruct((m, n), F32),
        compiler_params=_cparams(("parallel", "arbitrary")),
        name="ffn_down",
    )(a, w, res, final_g.reshape(1, -1))


def _head_sum(x):
    lo = lax.broadcasted_iota(jnp.int32, (1, LANES), 1) < HEAD
    s_lo = jnp.sum(jnp.where(lo, x, 0.0), axis=-1, keepdims=True)
    s_all = jnp.sum(x, axis=-1, keepdims=True)
    return jnp.where(lo, s_lo, s_all - s_lo)


def _token_mix(p, p_prev, mu, w0, w2, a0, a2, g2, k_k, k_a):
    ps = p + (p_prev - p) * mu
    o1, o2, o3 = RWKV_DIM, 2 * RWKV_DIM, 3 * RWKV_DIM
    o4 = o3 + DECAY_LORA
    o5 = o4 + AAA_LORA
    r, k, v = ps[:, :o1], ps[:, o1:o2], ps[:, o2:o3]
    wd, ad, gd = ps[:, o3:o4], ps[:, o4:o5], ps[:, o5:]
    z = w0 + _dot(jnp.tanh(wd).astype(BF16), w2)
    w_log = -(jnp.maximum(-z, 0.0) + jnp.log(1.0 + jnp.exp(-jnp.abs(z)))) - 0.5
    lw = -jnp.exp(w_log)
    a = _sigmoid(a0 + _dot(ad.astype(BF16), a2))
    g = _dot(_sigmoid(gd).astype(BF16), g2)
    kraw = k * k_k
    kk = []
    for pr in range(PAIRS):
        kp = kraw[:, pr * LANES:(pr + 1) * LANES]
        kk.append(kp * lax.rsqrt(jnp.maximum(_head_sum(kp * kp), 1e-24)))
    return r, k * (1.0 + (a - 1.0) * k_a), v, kk, a, lw, g


def _mix_out(y, r, k2, v, g, r_k, ln_g, ln_b):
    mean = _head_sum(y) * (1.0 / HEAD)
    d = y - mean
    var = _head_sum(d * d) * (1.0 / HEAD)
    yn = d * lax.rsqrt(var + GN_EPS) * ln_g + ln_b
    return (yn + _head_sum(r * k2 * r_k) * v) * g


def _prep_kernel(p_ref, prev_ref, mu_ref, w0_ref, w2_ref, a0_ref, a2_ref, g2_ref, kk_ref, ka_ref,
                 r_o, k_o, v_o, kk_o, a_o, lw_o, g_o):
    r, k2, v, kk, a, lw, g = _token_mix(p_ref[...], prev_ref[...], mu_ref[...], w0_ref[...], w2_ref[...],
                                        a0_ref[...], a2_ref[...], g2_ref[...], kk_ref[...], ka_ref[...])
    for pr in range(PAIRS):
        kk_o[:, pr * LANES:(pr + 1) * LANES] = kk[pr]
    r_o[...] = r
    k_o[...] = k2
    v_o[...] = v
    a_o[...] = a
    lw_o[...] = lw
    g_o[...] = g


def _mix_params(lp):
    row = lambda x: x.reshape(1, -1)
    return (row(lp['mu']), row(lp['w0']), lp['w2'].astype(BF16), row(lp['a0']), lp['a2'].astype(BF16),
            lp['g2'].astype(BF16), row(lp['k_k']), row(lp['k_a']))


def _mix_param_specs(index_map):
    shapes = [(1, RWKV_PROJ), (1, RWKV_DIM), (DECAY_LORA, RWKV_DIM), (1, RWKV_DIM), (AAA_LORA, RWKV_DIM),
              (GATE_LORA, RWKV_DIM), (1, RWKV_DIM), (1, RWKV_DIM)]
    return [pl.BlockSpec(s, index_map) for s in shapes]


def _rwkv_prep(p, prev, lp):
    m = p.shape[0]
    blk_in = pl.BlockSpec((m, RWKV_PROJ), lambda i: (0, 0))
    out_blk = pl.BlockSpec((m, RWKV_DIM), lambda i: (0, 0))
    return pl.pallas_call(
        _prep_kernel,
        grid=(1,),
        in_specs=[blk_in, blk_in] + _mix_param_specs(lambda i: (0, 0)),
        out_specs=[out_blk] * 7,
        out_shape=[jax.ShapeDtypeStruct((m, RWKV_DIM), F32)] * 7,
        compiler_params=_cparams(("arbitrary",)),
        name="rwkv_prep",
    )(p, prev, *_mix_params(lp))


def _chunk_kernel(p_ref, mu_ref, w0_ref, w2_ref, a0_ref, a2_ref, g2_ref, kk_ref, ka_ref, rk_ref, lg_ref, lb_ref,
                  y_ref, s_out_ref, s_ref, prev_ref):
    c = pl.program_id(1)

    @pl.when(c == 0)
    def _():
        s_ref[...] = jnp.zeros_like(s_ref)
        prev_ref[...] = jnp.zeros_like(prev_ref)

    p = p_ref[...]
    rows = p.shape[0]
    n_chunks = rows // CHUNK
    first_row = lax.broadcasted_iota(jnp.int32, p.shape, 0) == 0
    p_prev = jnp.where(first_row, prev_ref[...], pltpu.roll(p, 1, 0))
    prev_ref[...] = p[rows - 1:rows, :]
    r_all, k_all, v_all, kk_all, a_all, lw_all, g_all = _token_mix(
        p, p_prev, mu_ref[...], w0_ref[...], w2_ref[...], a0_ref[...], a2_ref[...], g2_ref[...], kk_ref[...],
        ka_ref[...])

    lane = lax.broadcasted_iota(jnp.int32, (1, LANES), 1)
    m0 = jnp.where(lane < HEAD, 1.0, 0.0)
    m1 = 1.0 - m0
    ii = lax.broadcasted_iota(jnp.int32, (LANES, LANES), 0)
    jj = lax.broadcasted_iota(jnp.int32, (LANES, LANES), 1)
    strict = ii > jj
    incl = ii >= jj
    eye = jnp.where(ii == jj, 1.0, 0.0)
    ti = lax.broadcasted_iota(jnp.int32, (CHUNK, CHUNK), 0)
    tj = lax.broadcasted_iota(jnp.int32, (CHUNK, CHUNK), 1)
    tril = jnp.where(ti >= tj, 1.0, 0.0).astype(BF16)

    def stack(x):
        return jnp.concatenate([x * m0, x * m1], axis=0)

    pairs = range(PAIRS)
    items = [(ci, pr) for ci in range(n_chunks) for pr in pairs]
    tok = [slice(ci * CHUNK, (ci + 1) * CHUNK) for ci, _ in items]
    sls = [slice(pr * LANES, (pr + 1) * LANES) for _, pr in items]
    each = lambda fn, *cols: [fn(*xs) for xs in zip(*cols)]
    cut = lambda x: [x[t, sl] for t, sl in zip(tok, sls)]
    lw, k2, r, v = cut(lw_all), cut(k_all), cut(r_all), cut(v_all)
    kk = [kk_all[pr][t, :] for (_, pr), t in zip(items, tok)]
    bb = each(lambda x, y: x * y, kk, cut(a_all))
    cum_all = [_split_dot(lw_all[ci * CHUNK:(ci + 1) * CHUNK, :], tril, 3, left=True) for ci in range(n_chunks)]
    cum = [cum_all[ci][:, sl] for (ci, _), sl in zip(items, sls)]
    tot = [x[CHUNK - 1:CHUNK, :] for x in cum]
    inv_g = [jnp.exp(-x) for x in cum]
    to_end = each(lambda t, x: jnp.exp(t - x), tot, cum)
    a_t = each(lambda x, cm, l: stack(-x * jnp.exp(cm - l)).astype(BF16), kk, cum, lw)
    b_t = each(lambda x, g: stack(x * g).astype(BF16), bb, inv_g)
    k_t = each(lambda x, g: stack(x * g).astype(BF16), k2, inv_g)
    r_f = each(lambda x, cm: stack(x * jnp.exp(cm)), r, cum)
    r_t = [x.astype(BF16) for x in r_f]
    v_f = [stack(x) for x in v]
    v_s = [x.astype(BF16) for x in v_f]
    bk_end = each(lambda x, y, e: jnp.concatenate([stack(x * e), stack(y * e)], axis=0).astype(BF16), bb, k2, to_end)

    a_ab = each(lambda x, y: jnp.where(strict, _dot_nt(x, y), 0.0), a_t, b_t)
    a_ak = each(lambda x, y: jnp.where(strict, _dot_nt(x, y), 0.0).astype(BF16), a_t, k_t)
    a_rb = each(lambda x, y: jnp.where(incl, _dot_nt(x, y), 0.0).astype(BF16), r_t, b_t)
    a_rk = each(lambda x, y: jnp.where(incl, _dot_nt(x, y), 0.0).astype(BF16), r_t, k_t)
    inv = [eye + x for x in a_ab]
    power = [x.astype(BF16) for x in a_ab]
    for _ in range(5):
        power = [_dot(x, x).astype(BF16) for x in power]
        inv = each(lambda x, p: x + _dot(x.astype(BF16), p), inv, power)
    w1 = each(lambda x, y: _dot(x, y).astype(BF16), a_ak, v_s)
    pu = each(lambda x, y, z: _dot(x.astype(BF16), jnp.concatenate([y, z], axis=1)), inv, a_t, w1)
    pu_b = [x.astype(BF16) for x in pu]
    qy = each(_dot, a_rb, pu_b)
    q_m = each(lambda x, y: (x + y[:, :LANES]).astype(BF16), r_f, qy)
    y_v = each(lambda x, y, z: x[:, LANES:] + _dot(y, z), qy, a_rk, v_s)
    pq = each(lambda x, y: jnp.concatenate([x[:, :LANES], y], axis=0), pu_b, q_m)
    state = [s_ref[pr] for pr in pairs]
    g_cut = cut(g_all)
    for ci in range(n_chunks):
        grp = slice(ci * PAIRS, (ci + 1) * PAIRS)
        uy = each(lambda x, s: _dot_nt(x, s.astype(BF16)), pq[grp], state)
        uv_t = each(lambda x, y, z: jnp.concatenate([x[:LANES] + y[:, LANES:], z], axis=0).T.astype(BF16),
                    uy, pu[grp], v_f[grp])
        state = each(lambda s, t, x, y: s * jnp.exp(t) + _dot(x, y), state, tot[grp], uv_t, bk_end[grp])
        for pr in pairs:
            i = ci * PAIRS + pr
            y_m = uy[pr][LANES:] + y_v[i]
            out = _mix_out(y_m[:CHUNK] + y_m[CHUNK:], r[i], k2[i], v[i], g_cut[i], rk_ref[:, sls[i]],
                           lg_ref[:, sls[i]], lb_ref[:, sls[i]])
            y_ref[tok[i], sls[i]] = out.astype(y_ref.dtype)
    for pr in pairs:
        s_ref[pr] = state[pr]

    @pl.when(c == pl.num_programs(1) - 1)
    def _():
        s_out_ref[0] = s_ref[...]


def _rwkv_chunk_scan(p, lp, batch):
    m = p.shape[0]
    n_chunks = m // batch // CHUNK
    cps = next(c for c in (4, 2, 1) if n_chunks % c == 0)
    nc = n_chunks // cps
    row = lambda x: x.reshape(1, -1)
    const = lambda b, c: (0, 0)
    vec = pl.BlockSpec((1, RWKV_DIM), const)
    y, s_pairs = pl.pallas_call(
        _chunk_kernel,
        grid=(batch, nc),
        in_specs=([pl.BlockSpec((cps * CHUNK, RWKV_PROJ), lambda b, c: (b * nc + c, 0))] + _mix_param_specs(const)
                  + [vec] * 3),
        out_specs=[pl.BlockSpec((cps * CHUNK, RWKV_DIM), lambda b, c: (b * nc + c, 0)),
                   pl.BlockSpec((1, PAIRS, LANES, LANES), lambda b, c: (b, 0, 0, 0))],
        out_shape=[jax.ShapeDtypeStruct((m, RWKV_DIM), BF16),
                   jax.ShapeDtypeStruct((batch, PAIRS, LANES, LANES), F32)],
        scratch_shapes=[pltpu.VMEM((PAIRS, LANES, LANES), F32), pltpu.VMEM((1, RWKV_PROJ), F32)],
        compiler_params=_cparams(("parallel", "arbitrary")),
        name="rwkv_chunk",
    )(p, *_mix_params(lp), row(lp['r_k']), row(lp['lnx_g']), row(lp['lnx_b']))
    s_fin = jnp.stack([s_pairs[:, :, :HEAD, :HEAD], s_pairs[:, :, HEAD:, HEAD:]], axis=2)
    return y, s_fin.reshape(batch, RWKV_HEADS, HEAD, HEAD)


def _step_kernel(s_ref, r_ref, k_ref, kk_ref, a_ref, lw_ref, v_ref, s_o, y_o):
    s = s_ref[0]
    kk = kk_ref[0]
    sa = jnp.sum(s * (-kk), axis=-1, keepdims=True)
    s_new = s * jnp.exp(lw_ref[0]) + sa * (kk * a_ref[0]) + v_ref[0] * k_ref[0]
    s_o[0] = s_new
    y_o[0] = jnp.sum(s_new * r_ref[0], axis=-1, keepdims=True)


def _rwkv_step(s0, r, k2, v, kk, a, lw):
    b = s0.shape[0]
    rowv = lambda x: x.reshape(b, RWKV_HEADS, 1, HEAD)
    row_spec = pl.BlockSpec((1, RWKV_HEADS, 1, HEAD), lambda i: (i, 0, 0, 0))
    col_spec = pl.BlockSpec((1, RWKV_HEADS, HEAD, 1), lambda i: (i, 0, 0, 0))
    s_spec = pl.BlockSpec((1, RWKV_HEADS, HEAD, HEAD), lambda i: (i, 0, 0, 0))
    s_new, y = pl.pallas_call(
        _step_kernel,
        grid=(b,),
        in_specs=[s_spec] + [row_spec] * 5 + [col_spec],
        out_specs=[s_spec, col_spec],
        out_shape=[jax.ShapeDtypeStruct(s0.shape, F32), jax.ShapeDtypeStruct((b, RWKV_HEADS, HEAD, 1), F32)],
        compiler_params=_cparams(("parallel",)),
        name="rwkv_step",
    )(s0, rowv(r), rowv(k2), rowv(kk), rowv(a), rowv(lw), v.reshape(b, RWKV_HEADS, HEAD, 1))
    return y.reshape(b, RWKV_DIM), s_new


def _post_kernel(y_ref, r_ref, k_ref, v_ref, g_ref, rk_ref, lg_ref, lb_ref, o_ref):
    for pr in range(PAIRS):
        sl = slice(pr * LANES, (pr + 1) * LANES)
        out = _mix_out(y_ref[:, sl], r_ref[:, sl], k_ref[:, sl], v_ref[:, sl], g_ref[:, sl], rk_ref[:, sl],
                       lg_ref[:, sl], lb_ref[:, sl])
        o_ref[:, sl] = out.astype(o_ref.dtype)


def _rwkv_post(y, r, k2, v, g, lp):
    m = y.shape[0]
    blk = pl.BlockSpec((m, RWKV_DIM), lambda i: (0, 0))
    par = pl.BlockSpec((1, RWKV_DIM), lambda i: (0, 0))
    return pl.pallas_call(
        _post_kernel,
        grid=(1,),
        in_specs=[blk] * 5 + [par] * 3,
        out_specs=blk,
        out_shape=jax.ShapeDtypeStruct((m, RWKV_DIM), BF16),
        compiler_params=_cparams(("arbitrary",)),
        name="rwkv_post",
    )(y, r, k2, v, g, lp['r_k'].reshape(1, -1), lp['lnx_g'].reshape(1, -1), lp['lnx_b'].reshape(1, -1))


def _lambda(lq1_ref, lk1_ref, lq2_ref, lk2_ref, lam_init):
    s1 = jnp.sum(lq1_ref[...] * lk1_ref[...], axis=-1, keepdims=True)
    s2 = jnp.sum(lq2_ref[...] * lk2_ref[...], axis=-1, keepdims=True)
    return jnp.exp(s1) - jnp.exp(s2) + lam_init


def _dattn_kernel(lq1_ref, lk1_ref, lq2_ref, lk2_ref, sg_ref, q_ref, k_ref, v_ref, o_ref,
                  acc_ref, *, tq, lam_init):
    nq = q_ref.shape[0] // tq
    lane = lax.broadcasted_iota(jnp.int32, (1, LANES), 1)
    row = lax.broadcasted_iota(jnp.int32, (tq, tq), 0)
    col = lax.broadcasted_iota(jnp.int32, (tq, tq), 1)
    ones = jnp.ones((tq, LANES), BF16)
    lam = _lambda(lq1_ref, lk1_ref, lq2_ref, lk2_ref, lam_init)
    rows = lambda ref, i: ref[i * tq:(i + 1) * tq, :]

    def scores(qi, j):
        q = rows(q_ref, qi)
        zero = jnp.zeros_like(q)
        ks = rows(k_ref, j)
        return [_dot_nt(jnp.where(lane < DA_QK, q, zero), ks), _dot_nt(jnp.where(lane >= DA_QK, q, zero), ks)]

    blocks = [(qi, j) for qi in range(nq) for j in range(qi + 1)]
    n_blocks = len(blocks)
    s_buf, p_buf = {}, {}
    m_run = None
    for t in range(n_blocks + 2):
        if t < n_blocks:
            s_buf[t] = scores(*blocks[t])
        pv = None
        if 0 <= t - 2 < n_blocks:
            p, p_alpha = p_buf.pop(t - 2)
            v_aug = jnp.concatenate([rows(v_ref, blocks[t - 2][1]), ones], axis=1)
            pv = [_dot(x, v_aug) for x in p]
        if 0 <= t - 1 < n_blocks:
            qi, j = blocks[t - 1]
            s = s_buf.pop(t - 1)
            if j == qi:
                s = [jnp.where(col <= row, x, NEG_INF) for x in s]
            row_max = [jnp.max(x, axis=-1, keepdims=True) for x in s]
            if j == 0:
                m_new, alpha = row_max, None
            else:
                m_new = [jnp.maximum(mo, mx) for mo, mx in zip(m_run, row_max)]
                alpha = [jnp.exp2(mo - mn) for mo, mn in zip(m_run, m_new)]
            p_buf[t - 1] = ([jnp.exp2(x - mn).astype(BF16) for x, mn in zip(s, m_new)], alpha)
            m_run = m_new
        if pv is not None:
            qi, j = blocks[t - 2]
            acc = pv if j == 0 else [alpha_c * acc_ref[c] + pv[c] for c, alpha_c in enumerate(p_alpha)]
            if j == qi:
                a0, a1 = acc
                o = a0[:, :LANES] / a0[:, LANES:] - lam * (a1[:, :LANES] / a1[:, LANES:])
                o_ref[qi * tq:(qi + 1) * tq, :] = (_rms_rows(o, sg_ref[...]) * (1.0 - lam_init)).astype(o_ref.dtype)
            else:
                for c in range(2):
                    acc_ref[c] = acc[c]


def _diff_attn_prompt(qb, kb, vb, lp, lam_init, batch, tq=512):
    m = qb.shape[0]
    t = m // batch
    tq = min(tq, t)
    vec = pl.BlockSpec((1, DA_QK), lambda b, h: (0, 0))
    seq_spec = pl.BlockSpec((t, LANES), lambda b, h: (b, h))
    return pl.pallas_call(
        functools.partial(_dattn_kernel, tq=tq, lam_init=lam_init),
        grid=(batch, DA_HEADS),
        in_specs=[vec] * 4 + [pl.BlockSpec((1, DA_V), lambda b, h: (0, 0)), seq_spec, seq_spec, seq_spec],
        out_specs=seq_spec,
        out_shape=jax.ShapeDtypeStruct((m, DA_DIM), BF16),
        scratch_shapes=[pltpu.VMEM((2, tq, 2 * LANES), F32)],
        compiler_params=_cparams(("parallel", "parallel")),
        name="diff_attn_prompt",
    )(*[lp[n].reshape(1, -1) for n in ('lq1', 'lk1', 'lq2', 'lk2')], lp['subln_g'].reshape(1, -1), qb, kb, vb)


def _paged_kernel(pt_ref, lq1_ref, lk1_ref, lq2_ref, lk2_ref, sg_ref, q_ref, kn_ref, vn_ref, *rest, pps, lam_init):
    del pt_ref
    k_refs, v_refs = rest[:pps], rest[pps:2 * pps]
    o_ref, m_ref, l_ref, acc_ref = rest[2 * pps:]
    pg = pl.program_id(1)
    n_rows = 2 * DA_HEADS

    @pl.when(pg == 0)
    def _():
        m_ref[...] = jnp.full_like(m_ref, NEG_INF)
        l_ref[...] = jnp.zeros_like(l_ref)
        acc_ref[...] = jnp.zeros_like(acc_ref)

    q = q_ref[0]
    lane = lax.broadcasted_iota(jnp.int32, (1, LANES), 1)
    q_rows = jnp.concatenate([jnp.where(lane < DA_QK, q, 0.0), jnp.where(lane >= DA_QK, q, 0.0)], axis=0)
    q_bf = q_rows.astype(BF16)
    cols = k_refs[0].shape[1] * DA_HEADS
    same_head = ((lax.broadcasted_iota(jnp.int32, (n_rows, cols), 1) & (DA_HEADS - 1))
                 == (lax.broadcasted_iota(jnp.int32, (n_rows, cols), 0) & (DA_HEADS - 1)))
    flat = lambda ref: ref[0].reshape(cols, LANES).astype(BF16)

    m_run, l_run, acc = m_ref[...], l_ref[...], acc_ref[...]
    s_buf, p_buf = {}, {}
    for t in range(pps + 2):
        if t < pps:
            s_buf[t] = _dot_nt(q_bf, flat(k_refs[t]))
        pv = None
        if 0 <= t - 2 < pps:
            p, p_alpha = p_buf.pop(t - 2)
            pv = _dot(p, flat(v_refs[t - 2]))
        if 0 <= t - 1 < pps:
            s = jnp.where(same_head, s_buf.pop(t - 1), NEG_INF)
            m_new = jnp.maximum(m_run, jnp.max(s, axis=-1, keepdims=True))
            alpha = jnp.exp2(m_run - m_new)
            p = jnp.exp2(s - m_new)
            l_run = alpha * l_run + jnp.sum(p, axis=-1, keepdims=True)
            p_buf[t - 1] = (p.astype(BF16), alpha)
            m_run = m_new
        if pv is not None:
            acc = p_alpha * acc + pv
    m_ref[...] = m_run
    l_ref[...] = l_run
    acc_ref[...] = acc

    @pl.when(pg == pl.num_programs(1) - 1)
    def _():
        k_new = jnp.concatenate([kn_ref[0]] * 2, axis=0)
        v_new = jnp.concatenate([vn_ref[0]] * 2, axis=0)
        s_new = jnp.sum(q_rows * k_new, axis=-1, keepdims=True)
        m_fin = jnp.maximum(m_run, s_new)
        alpha = jnp.exp2(m_run - m_fin)
        p_new = jnp.exp2(s_new - m_fin)
        out = (alpha * acc + p_new * v_new) / (alpha * l_run + p_new)
        lam = _lambda(lq1_ref, lk1_ref, lq2_ref, lk2_ref, lam_init)
        o = out[:DA_HEADS] - lam * out[DA_HEADS:]
        o_ref[0] = (_rms_rows(o, sg_ref[...]) * (1.0 - lam_init)).astype(o_ref.dtype)


def _diff_attn_paged(q, k_new, v_new, cache_k, cache_v, page_table, lp, lam_init):
    b, n_pages = page_table.shape
    page = cache_k.shape[1]
    pps = next(c for c in (16, 8, 4, 2, 1) if n_pages % c == 0)
    vec = pl.BlockSpec((1, DA_QK), lambda i, p, pt: (0, 0))
    rowb = pl.BlockSpec((1, DA_HEADS, LANES), lambda i, p, pt: (i, 0, 0))

    def page_spec(j):
        return pl.BlockSpec((1, page, DA_HEADS, LANES),
                            lambda i, p, pt: (pt[i * n_pages + p * pps + j], 0, 0, 0))

    grid_spec = pltpu.PrefetchScalarGridSpec(
        num_scalar_prefetch=1,
        grid=(b, n_pages // pps),
        in_specs=([vec] * 4 + [pl.BlockSpec((1, DA_V), lambda i, p, pt: (0, 0)), rowb, rowb, rowb]
                  + [page_spec(j) for j in range(pps)] * 2),
        out_specs=rowb,
        scratch_shapes=[pltpu.VMEM((2 * DA_HEADS, 1), F32)] * 2 + [pltpu.VMEM((2 * DA_HEADS, LANES), F32)],
    )
    r3 = lambda x: x.reshape(b, DA_HEADS, LANES)
    out = pl.pallas_call(
        functools.partial(_paged_kernel, pps=pps, lam_init=lam_init),
        grid_spec=grid_spec,
        out_shape=jax.ShapeDtypeStruct((b, DA_HEADS, LANES), BF16),
        compiler_params=_cparams(("parallel", "arbitrary")),
        name="diff_attn_paged",
    )(page_table.reshape(-1), *[lp[n].reshape(1, -1) for n in ('lq1', 'lk1', 'lq2', 'lk2')],
      lp['subln_g'].reshape(1, -1), r3(q), r3(k_new), r3(v_new), *([cache_k] * pps), *([cache_v] * pps))
    return out.reshape(b, DA_DIM)


def _cross_kernel(q_ref, mk_ref, mv_ref, o_ref):
    sls = [slice(h * CROSS_HD, (h + 1) * CROSS_HD) for h in range(CROSS_HEADS)]
    s = [_dot_nt(q_ref[0, :, sl], mk_ref[0, :, sl].astype(BF16)) for sl in sls]
    p = [jnp.exp(x - jnp.max(x, axis=-1, keepdims=True)) for x in s]
    l = [jnp.sum(x, axis=-1, keepdims=True) for x in p]
    o = [_dot(x.astype(BF16), mv_ref[0, :, sl].astype(BF16)) for x, sl in zip(p, sls)]
    for sl, x, d in zip(sls, o, l):
        o_ref[0, :, sl] = (x / d).astype(o_ref.dtype)


def _cross_decode_kernel(q_ref, mk_ref, mv_ref, o_ref):
    q = q_ref[0].astype(F32)
    s = jnp.sum(mk_ref[0] * q, axis=-1, keepdims=True)
    p = jnp.exp(s - jnp.max(s, axis=0, keepdims=True))
    o = jnp.sum(p * mv_ref[0], axis=0) / jnp.sum(p, axis=0)
    o_ref[0] = o.astype(o_ref.dtype)


def _cross_attn_decode(q, mk, mv):
    b = q.shape[0]
    mt = mk.shape[1]
    q_spec = pl.BlockSpec((1, CROSS_HEADS, CROSS_HD), lambda i: (i, 0, 0))
    m_spec = pl.BlockSpec((1, mt, CROSS_HEADS, CROSS_HD), lambda i: (i, 0, 0, 0))
    out = pl.pallas_call(
        _cross_decode_kernel,
        grid=(b,),
        in_specs=[q_spec, m_spec, m_spec],
        out_specs=q_spec,
        out_shape=jax.ShapeDtypeStruct((b, CROSS_HEADS, CROSS_HD), BF16),
        compiler_params=_cparams(("parallel",)),
        name="cross_attn_decode",
    )(q.reshape(b, CROSS_HEADS, CROSS_HD), mk, mv)
    return out.reshape(b, D_MODEL)


def _cross_attn(q, mk, mv, tq=512):
    b, t, _ = q.shape
    mt = mk.shape[1]
    tq = min(tq, t)
    q_spec = pl.BlockSpec((1, tq, D_MODEL), lambda i, j: (i, j, 0))
    m_spec = pl.BlockSpec((1, mt, D_MODEL), lambda i, j: (i, 0, 0))
    return pl.pallas_call(
        _cross_kernel,
        grid=(b, t // tq),
        in_specs=[q_spec, m_spec, m_spec],
        out_specs=q_spec,
        out_shape=jax.ShapeDtypeStruct(q.shape, BF16),
        compiler_params=_cparams(("parallel", "parallel")),
        name="cross_attn",
    )(q, mk, mv)


def _layer(x, batch, lp, wts, lam_init, mk, mv, rwkv_fn, attn_fn):
    m = x.shape[0]
    t = m // batch
    pr = _mm([x], [wts['in_r']], name="in_proj_rwkv", norm_g=lp['norm_mix_g'], tm=512)
    qb, kf, vf, kb, vb = _da_proj(x, lp['norm_mix_g'], wts['in_da'])
    y_r, s_new = rwkv_fn(pr)
    y_d = attn_fn(qb, kf, vf, kb, vb)
    x = _mm([y_r, y_d], [wts['o_r'], wts['o_d']], name="out_proj", res=x, tm=512)
    qc = _mm([x], [wts['cq']], name="cross_q", norm_g=lp['norm_cross_g'], scale=CROSS_HD ** -0.5, out_dtype=BF16,
             tm=1024)
    if mk.ndim == 4:
        oc = _cross_attn_decode(qc, mk, mv)
    else:
        oc = _cross_attn(qc.reshape(batch, t, D_MODEL), mk, mv).reshape(m, D_MODEL)
    x = _mm([oc], [wts['co']], name="cross_out", res=x, tm=512)
    hmid = _mm([x], [wts['gate'], wts['up']], name="ffn_swiglu", norm_g=lp['norm_ffn_g'], swiglu=True,
               out_dtype=BF16, tm=1024, tn=512)
    x = _mm_down(hmid, wts['down'], x, wts['final_g'], wts['is_last'], tm=1024, tk=hmid.shape[1] // 4)
    last_row = pr.reshape(batch, t, RWKV_PROJ)[:, -1]
    return x, kf, vf, s_new, last_row


def kernel(x_prompt, x_sample, cache_k, cache_v, cache_mem_k, cache_mem_v, state_wkv, state_shift, page_table, mem_prompt, norm_mix_g, w_in, tok_shift_mu, rwkv_w0, rwkv_w2, rwkv_a0, rwkv_a2, rwkv_g2, rwkv_k_k, rwkv_k_a, rwkv_r_k, rwkv_lnx_g, rwkv_lnx_b, diff_lq1, diff_lk1, diff_lq2, diff_lk2, diff_subln_g, w_o, norm_cross_g, norm_mem_g, w_cq, w_ck, w_cv, w_co, norm_ffn_g, w_gate, w_up, w_down, final_norm_g):
    b, t, _ = x_prompt.shape
    db, dt, _ = x_sample.shape
    depth = w_in.shape[0]
    n_mem = mem_prompt.shape[1]
    n_pool, page = cache_k.shape[1], cache_k.shape[2]
    xp = x_prompt.reshape(b * t, D_MODEL)
    xs = x_sample.reshape(db * dt, D_MODEL)
    outs = {n: [] for n in ('k_p', 'v_p', 'k_s', 'v_s', 'S_p', 'S_s', 'sh_p', 'sh_s', 'mk_p', 'mv_p')}
    for l in range(depth):
        lam_init = 0.8 - 0.6 * math.exp(-0.3 * l)
        lp = dict(norm_mix_g=norm_mix_g[l], mu=tok_shift_mu[l], w0=rwkv_w0[l], w2=rwkv_w2[l], a0=rwkv_a0[l],
                  a2=rwkv_a2[l], g2=rwkv_g2[l], k_k=rwkv_k_k[l], k_a=rwkv_k_a[l], r_k=rwkv_r_k[l],
                  lnx_g=rwkv_lnx_g[l], lnx_b=rwkv_lnx_b[l], lq1=diff_lq1[l], lk1=diff_lk1[l], lq2=diff_lq2[l],
                  lk2=diff_lk2[l], subln_g=diff_subln_g[l], norm_cross_g=norm_cross_g[l], norm_ffn_g=norm_ffn_g[l])
        wl = w_in[l].astype(BF16)
        wts = dict(in_r=wl[:, :RWKV_PROJ], in_da=wl[:, RWKV_PROJ:], o_r=w_o[l][:RWKV_DIM].astype(BF16),
                   o_d=w_o[l][RWKV_DIM:].astype(BF16), cq=w_cq[l].astype(BF16), co=w_co[l].astype(BF16),
                   gate=w_gate[l].astype(BF16), up=w_up[l].astype(BF16), down=w_down[l].astype(BF16),
                   final_g=final_norm_g, is_last=l == depth - 1)

        mem2 = mem_prompt.reshape(b * n_mem, D_MODEL)
        mk = _mm([mem2], [w_ck[l].astype(BF16)], name="mem_k", norm_g=norm_mem_g[l], tm=1024)
        mv = _mm([mem2], [w_cv[l].astype(BF16)], name="mem_v", norm_g=norm_mem_g[l], tm=1024)

        def rwkv_prompt(pr):
            return _rwkv_chunk_scan(pr, lp, b)

        def attn_prompt(qb, kf, vf, kb, vb):
            return _diff_attn_prompt(qb, kb, vb, lp, lam_init, b)

        xp, kf, vf, s_new, last = _layer(xp, b, lp, wts, lam_init, mk.reshape(b, n_mem, D_MODEL),
                                         mv.reshape(b, n_mem, D_MODEL), rwkv_prompt, attn_prompt)
        outs['k_p'].append(kf.reshape(b, t, DA_HEADS, 2 * DA_QK))
        outs['v_p'].append(vf.reshape(b, t, DA_HEADS, DA_V))
        outs['S_p'].append(s_new)
        outs['sh_p'].append(last)
        outs['mk_p'].append(mk.reshape(b, n_mem, CROSS_HEADS, CROSS_HD))
        outs['mv_p'].append(mv.reshape(b, n_mem, CROSS_HEADS, CROSS_HD))

        assert dt == 1

        def rwkv_sample(pr):
            r, k2, v, kk, a, lw, g = _rwkv_prep(pr, state_shift[l], lp)
            y, s_fin = _rwkv_step(state_wkv[l], r, k2, v, kk, a, lw)
            return _rwkv_post(y, r, k2, v, g, lp), s_fin

        def attn_sample(qb, kf, vf, kb, vb):
            return _diff_attn_paged(qb.astype(F32), kf, vf, cache_k[l], cache_v[l], page_table, lp, lam_init)

        xs, kf, vf, s_new, last = _layer(xs, db, lp, wts, lam_init, cache_mem_k[l], cache_mem_v[l],
                                         rwkv_sample, attn_sample)
        outs['k_s'].append(kf.reshape(db, dt, DA_HEADS, 2 * DA_QK))
        outs['v_s'].append(vf.reshape(db, dt, DA_HEADS, DA_V))
        outs['S_s'].append(s_new)
        outs['sh_s'].append(last)
    y_prompt = xp.reshape(b, t, D_MODEL)
    y_sample = xs.reshape(db, dt, D_MODEL)
    st = lambda n: jnp.stack(outs[n])
    return (y_prompt, y_sample, st('k_p'), st('v_p'), st('k_s'), st('v_s'), st('S_p'), st('S_s'),
            st('sh_p'), st('sh_s'), st('mk_p'), st('mv_p'))
```
